```python
import jax, jax.numpy as jnp
from jax import lax
import numpy as np

D_MODEL = 2048
BATCH = 8
SEQ = 2048
DEPTH = 1
DEC_BATCH = 4
DEC_SEQ = 2048
PAST_LEN = 128

SGU_GROUPS = 8
SGU_GROUP_DIM = 128
SGU_WIDTH = SGU_GROUPS * SGU_GROUP_DIM
SGU_CHUNK = 128
DN_HEADS = 8
DN_HEAD_DIM = 128
DN_WIDTH = DN_HEADS * DN_HEAD_DIM
DN_CHUNK = 64
CONV_WIDTH = 5
IN_COLS = 2 * SGU_WIDTH + 4 * DN_WIDTH + 4 * DN_HEADS + 2 * D_MODEL
N_EXPERTS = 256
TOP_K = 8
N_GROUPS = 8
TOPK_GROUPS = 4
EXPERT_DIM = 512
SHARED_DIM = 512
ROUTED_SCALE = 2.5
EXPERT_BLOCK = 128
RMS_EPS = 1e-6
LN_EPS = 1e-5

kernel_name = "hybrid_sgu_gdn_moe_encoder"


def _rms_norm(x, g):
    xf = x.astype(jnp.float32)
    y = xf * lax.rsqrt(jnp.mean(xf * xf, axis=-1, keepdims=True) + RMS_EPS)
    return (y * g.astype(jnp.float32)).astype(x.dtype)


def _layer_norm(x, g, b):
    xf = x.astype(jnp.float32)
    mu = jnp.mean(xf, axis=-1, keepdims=True)
    xc = xf - mu
    var = jnp.mean(xc * xc, axis=-1, keepdims=True)
    y = xc * lax.rsqrt(var + LN_EPS) * g.astype(jnp.float32) + b.astype(jnp.float32)
    return y.astype(x.dtype)


def _l2norm(x):
    return x * lax.rsqrt(jnp.sum(x * x, axis=-1, keepdims=True) + 1e-6)


def _swiglu(x, w_gate, w_up, w_down):
    return (jax.nn.silu(x @ w_gate) * (x @ w_up)) @ w_down


def _sgu(u, v, ln_g, ln_b, ws, sb):
    Bn, S, _ = u.shape
    v = _layer_norm(v, ln_g, ln_b)
    v = v.reshape(Bn, S // SGU_CHUNK, SGU_CHUNK, SGU_GROUPS, SGU_GROUP_DIM)
    v = jnp.einsum('gpq,bnqgc->bnpgc', ws, v) + sb.T[None, None, :, :, None]
    return u * v.reshape(Bn, S, SGU_WIDTH)


def _gated_delta_chunked(q, k, v, g, beta):
    Bn, S, H, Dk = q.shape
    Dv = v.shape[-1]
    C = DN_CHUNK
    N = S // C
    ch = lambda t: t.reshape(Bn, N, C, H, -1).transpose(0, 3, 1, 2, 4)
    q, k, v = ch(q), ch(k), ch(v)
    g = g.reshape(Bn, N, C, H).transpose(0, 3, 1, 2)
    beta = beta.reshape(Bn, N, C, H).transpose(0, 3, 1, 2)
    gam = jnp.cumsum(g, axis=-1)
    lower = jnp.tril(jnp.ones((C, C), dtype=bool))
    strict = jnp.tril(jnp.ones((C, C), dtype=bool), -1)
    decay_mat = jnp.exp(jnp.where(lower, gam[..., :, None] - gam[..., None, :], -jnp.inf))
    kb = k * beta[..., None]
    a_mat = jnp.where(strict, jnp.einsum('bhnid,bhnjd->bhnij', kb, k) * decay_mat, 0.0)
    t_mat = a_mat + jnp.eye(C, dtype=a_mat.dtype)
    rhs = jnp.concatenate([v * beta[..., None], kb * jnp.exp(gam)[..., None]], axis=-1)
    sol = lax.linalg.triangular_solve(t_mat, rhs, left_side=True, lower=True, unit_diagonal=True)
    u, w = sol[..., :Dv], sol[..., Dv:]
    attn = jnp.einsum('bhnid,bhnjd->bhnij', q, k) * decay_mat
    q_dec = q * jnp.exp(gam)[..., None]
    k_dec = k * jnp.exp(gam[..., -1:] - gam)[..., None]
    chunk_decay = jnp.exp(gam[..., -1])

    def step(state, xs):
        u_i, w_i, attn_i, qd_i, kd_i, cd_i = xs
        v_new = u_i - jnp.einsum('bhcd,bhde->bhce', w_i, state)
        o_i = jnp.einsum('bhcd,bhde->bhce', qd_i, state) + jnp.einsum('bhij,bhje->bhie', attn_i, v_new)
        state = state * cd_i[..., None, None] + jnp.einsum('bhcd,bhce->bhde', kd_i, v_new)
        return state, o_i

    mv = lambda t: jnp.moveaxis(t, 2, 0)
    s0 = jnp.zeros((Bn, H, Dk, Dv), jnp.float32)
    _, o = lax.scan(step, s0, (mv(u), mv(w), mv(attn), mv(q_dec), mv(k_dec), mv(chunk_decay)))
    return o.transpose(1, 0, 3, 2, 4).reshape(Bn, S, H, Dv)


def _token_mixers(h, w_in, sgu_ln_g, sgu_ln_b, sgu_ws, sgu_b, dn_conv_w, dn_a_log, dn_dt_bias,
                  dn_norm_g, w_branch, w_out):
    Bn, S, _ = h.shape
    z = h @ w_in
    idx = list(np.cumsum([SGU_WIDTH, SGU_WIDTH, 3 * DN_WIDTH, DN_WIDTH, 4 * DN_HEADS]))
    u, v, qkv, og, ab, mg = jnp.split(z, idx, axis=-1)
    a_out = _sgu(jax.nn.gelu(u), jax.nn.gelu(v), sgu_ln_g, sgu_ln_b, sgu_ws, sgu_b)
    pad = (CONV_WIDTH - 1) // 2
    qkv = lax.conv_general_dilated(qkv, dn_conv_w[:, None, :], window_strides=(1,), padding=[(pad, pad)],
                                   dimension_numbers=('NWC', 'WIO', 'NWC'), feature_group_count=3 * DN_WIDTH)
    qkv = jax.nn.silu(qkv).astype(jnp.float32)
    q, k, vv = [t.reshape(Bn, S, DN_HEADS, DN_HEAD_DIM) for t in jnp.split(qkv, 3, axis=-1)]
    q = _l2norm(q) * (DN_HEAD_DIM ** -0.5)
    k = _l2norm(k)
    ab = ab.astype(jnp.float32).reshape(Bn, S, 4, DN_HEADS)
    a_log = dn_a_log.astype(jnp.float32)
    dt_b = dn_dt_bias.astype(jnp.float32)
    g_f = -jnp.exp(a_log[0]) * jax.nn.softplus(ab[:, :, 0] + dt_b[0])
    g_b = -jnp.exp(a_log[1]) * jax.nn.softplus(ab[:, :, 1] + dt_b[1])
    beta_f = jax.nn.sigmoid(ab[:, :, 2])
    beta_b = jax.nn.sigmoid(ab[:, :, 3])
    o_f = _gated_delta_chunked(q, k, vv, g_f, beta_f)
    fl = lambda t: jnp.flip(t, axis=1)
    o_b = fl(_gated_delta_chunked(fl(q), fl(k), fl(vv), fl(g_b), fl(beta_b)))
    o = _rms_norm(o_f + o_b, dn_norm_g) * jax.nn.silu(og.astype(jnp.float32).reshape(Bn, S, DN_HEADS, DN_HEAD_DIM))
    b_out = o.reshape(Bn, S, DN_WIDTH).astype(h.dtype)
    ga, gb = jnp.split(jax.nn.sigmoid(mg), 2, axis=-1)
    ya = a_out @ w_branch[:SGU_WIDTH]
    yb = b_out @ w_branch[SGU_WIDTH:]
    return (ga * ya + gb * yb) @ w_out


def _grouped_experts(xt, eidx, wts, w_gate, w_up, w_down):
    T, D = xt.shape
    n_pairs = T * TOP_K
    flat_e = eidx.reshape(n_pairs).astype(jnp.int32)
    flat_tok = jnp.repeat(jnp.arange(T, dtype=jnp.int32), TOP_K)
    flat_w = wts.reshape(n_pairs)
    order = jnp.argsort(flat_e)
    sorted_e = flat_e[order]
    counts = jnp.bincount(flat_e, length=N_EXPERTS).astype(jnp.int32)
    starts = jnp.cumsum(counts) - counts
    pcounts = (counts + EXPERT_BLOCK - 1) // EXPERT_BLOCK * EXPERT_BLOCK
    pends = jnp.cumsum(pcounts)
    pstarts = pends - pcounts
    dest = pstarts[sorted_e] + jnp.arange(n_pairs, dtype=jnp.int32) - starts[sorted_e]
    n_blocks = (n_pairs + N_EXPERTS * (EXPERT_BLOCK - 1) + EXPERT_BLOCK - 1) // EXPERT_BLOCK
    n_rows = n_blocks * EXPERT_BLOCK
    row_tok = jnp.full((n_rows,), T, jnp.int32).at[dest].set(flat_tok[order])
    row_w = jnp.zeros((n_rows,), xt.dtype).at[dest].set(flat_w[order])
    block_e = jnp.minimum(jnp.searchsorted(pends, jnp.arange(n_blocks, dtype=jnp.int32) * EXPERT_BLOCK,
                                           side='right'), N_EXPERTS - 1).astype(jnp.int32)
    x_pad = jnp.concatenate([xt, jnp.zeros((1, D), xt.dtype)], axis=0)

    def body(acc, xs):
        tok, w, e = xs
        yb = _swiglu(x_pad[tok], w_gate[e], w_up[e], w_down[e]) * w[:, None]
        return acc.at[tok].add(yb), None

    acc, _ = lax.scan(body, jnp.zeros((T + 1, D), xt.dtype),
                      (row_tok.reshape(n_blocks, EXPERT_BLOCK), row_w.reshape(n_blocks, EXPERT_BLOCK), block_e))
    return acc[:T]


def _moe(h, router_w, router_bias, exp_w_gate, exp_w_up, exp_w_down, sh_w_gate, sh_w_up, sh_w_down):
    Bn, S, D = h.shape
    T = Bn * S
    xt = h.reshape(T, D)
    scores = jax.nn.sigmoid(xt.astype(jnp.float32) @ router_w.astype(jnp.float32))
    sel = scores + router_bias.astype(jnp.float32)
    per_group = N_EXPERTS // N_GROUPS
    grp_score = lax.top_k(sel.reshape(T, N_GROUPS, per_group), 2)[0].sum(-1)
    _, gidx = lax.top_k(grp_score, TOPK_GROUPS)
    gmask = jnp.any(gidx[:, :, None] == jnp.arange(N_GROUPS)[None, None, :], axis=1)
    masked = jnp.where(jnp.repeat(gmask, per_group, axis=1), sel, -jnp.inf)
    _, eidx = lax.top_k(masked, TOP_K)
    wts = jnp.take_along_axis(scores, eidx, axis=-1)
    wts = wts / (jnp.sum(wts, axis=-1, keepdims=True) + 1e-20) * ROUTED_SCALE
    routed = _grouped_experts(xt, eidx, wts.astype(h.dtype), exp_w_gate, exp_w_up, exp_w_down)
    shared = _swiglu(xt, sh_w_gate, sh_w_up, sh_w_down)
    return (routed + shared).reshape(Bn, S, D)


def _trunk(x, c, p):
    for l in range(DEPTH):
        mod = jax.nn.silu(c) @ p['ada_w'][l] + p['ada_b'][l]
        sh1, sc1, gt1, sh2, sc2, gt2 = [m[:, None, :] for m in jnp.split(mod, 6, axis=-1)]
        h = _rms_norm(x, p['norm_mix_g'][l]) * (1 + sc1) + sh1
        x = x + gt1 * _token_mixers(h, p['w_in'][l], p['sgu_ln_g'][l], p['sgu_ln_b'][l], p['sgu_ws'][l],
                                    p['sgu_b'][l], p['dn_conv_w'][l], p['dn_a_log'][l], p['dn_dt_bias'][l],
                                    p['dn_norm_g'][l], p['w_branch'][l], p['w_out'][l])
        h = _rms_norm(x, p['norm_ffn_g'][l]) * (1 + sc2) + sh2
        x = x + gt2 * _moe(h, p['router_w'][l], p['router_bias'][l], p['exp_w_gate'][l], p['exp_w_up'][l],
                           p['exp_w_down'][l], p['sh_w_gate'][l], p['sh_w_up'][l], p['sh_w_down'][l])
    return _rms_norm(x, p['final_g'])


def setup_inputs(seed: int = 0) -> dict:
    key = jax.random.key(seed)
    ks = jax.random.split(key, 32)
    f32 = jnp.float32
    nrm = lambda k, shape, s: jax.random.normal(k, shape, f32) * s
    L, D = DEPTH, D_MODEL
    dt = jnp.exp(jax.random.uniform(ks[14], (L, 2, DN_HEADS), f32) * (np.log(0.1) - np.log(0.001)) + np.log(0.001))
    return {
        "x_prompt": nrm(ks[0], (BATCH, SEQ, D), 1.0),
        "x_sample": nrm(ks[1], (DEC_BATCH, DEC_SEQ, D), 1.0),
        "c_prompt": nrm(ks[2], (BATCH, D), 1.0),
        "c_sample": nrm(ks[3], (DEC_BATCH, D), 1.0),
        "ada_w": nrm(ks[4], (L, D, 6 * D), 0.5 * D ** -0.5),
        "ada_b": nrm(ks[5], (L, 6 * D), 0.02),
        "norm_mix_g": 1.0 + nrm(ks[6], (L, D), 0.05),
        "norm_ffn_g": 1.0 + nrm(ks[7], (L, D), 0.05),
        "w_in": nrm(ks[8], (L, D, IN_COLS), D ** -0.5),
        "sgu_ln_g": 1.0 + nrm(ks[9], (L, SGU_WIDTH), 0.05),
        "sgu_ln_b": nrm(ks[10], (L, SGU_WIDTH), 0.02),
        "sgu_ws": nrm(ks[11], (L, SGU_GROUPS, SGU_CHUNK, SGU_CHUNK), SGU_CHUNK ** -0.5),
        "sgu_b": 1.0 + nrm(ks[12], (L, SGU_GROUPS, SGU_CHUNK), 0.1),
        "dn_conv_w": nrm(ks[13], (L, CONV_WIDTH, 3 * DN_WIDTH), CONV_WIDTH ** -0.5),
        "dn_a_log": jnp.log(jax.random.uniform(ks[15], (L, 2, DN_HEADS), f32, 1.0, 16.0)),
        "dn_dt_bias": dt + jnp.log(-jnp.expm1(-dt)),
        "dn_norm_g": 1.0 + nrm(ks[16], (L, DN_HEAD_DIM), 0.05),
        "w_branch": nrm(ks[17], (L, SGU_WIDTH + DN_WIDTH, D), SGU_WIDTH ** -0.5),
        "w_out": nrm(ks[18], (L, D, D), D ** -0.5),
        "router_w": nrm(ks[19], (L, D, N_EXPERTS), D ** -0.5),
        "router_bias": nrm(ks[20], (L, N_EXPERTS), 0.01),
        "exp_w_gate": nrm(ks[21], (L, N_EXPERTS, D, EXPERT_DIM), D ** -0.5),
        "exp_w_up": nrm(ks[22], (L, N_EXPERTS, D, EXPERT_DIM), D ** -0.5),
        "exp_w_down": nrm(ks[23], (L, N_EXPERTS, EXPERT_DIM, D), EXPERT_DIM ** -0.5),
        "sh_w_gate": nrm(ks[24], (L, D, SHARED_DIM), D ** -0.5),
        "sh_w_up": nrm(ks[25], (L, D, SHARED_DIM), D ** -0.5),
        "sh_w_down": nrm(ks[26], (L, SHARED_DIM, D), SHARED_DIM ** -0.5),
        "final_g": 1.0 + nrm(ks[27], (D,), 0.05),
    }


def reference(x_prompt, x_sample, c_prompt, c_sample, ada_w, ada_b, norm_mix_g, norm_ffn_g, w_in,
              sgu_ln_g, sgu_ln_b, sgu_ws, sgu_b, dn_conv_w, dn_a_log, dn_dt_bias, dn_norm_g, w_branch,
              w_out, router_w, router_bias, exp_w_gate, exp_w_up, exp_w_down, sh_w_gate, sh_w_up,
              sh_w_down, final_g):
    params = dict(ada_w=ada_w, ada_b=ada_b, norm_mix_g=norm_mix_g, norm_ffn_g=norm_ffn_g, w_in=w_in,
                  sgu_ln_g=sgu_ln_g, sgu_ln_b=sgu_ln_b, sgu_ws=sgu_ws, sgu_b=sgu_b, dn_conv_w=dn_conv_w,
                  dn_a_log=dn_a_log, dn_dt_bias=dn_dt_bias, dn_norm_g=dn_norm_g, w_branch=w_branch,
                  w_out=w_out, router_w=router_w, router_bias=router_bias, exp_w_gate=exp_w_gate,
                  exp_w_up=exp_w_up, exp_w_down=exp_w_down, sh_w_gate=sh_w_gate, sh_w_up=sh_w_up,
                  sh_w_down=sh_w_down, final_g=final_g)
    y_prompt = _trunk(x_prompt, c_prompt, params)
    y_sample = _trunk(x_sample, c_sample, params)
    return (y_prompt, y_sample)
```

```python
import functools

import jax
import jax.numpy as jnp
import numpy as np
from jax import lax
from jax.experimental import pallas as pl
from jax.experimental.pallas import tpu as pltpu

F32 = jnp.float32
BF16 = jnp.bfloat16
I32 = jnp.int32
U32 = jnp.uint32

RMS_EPS = 1e-6
LN_EPS = 1e-5
L2_EPS = 1e-6

SGU_GROUPS = 8
SGU_CHUNK = 128
DN_HEADS = 8
DN_HEAD_DIM = 128
DN_CHUNK = 64
CONV_WIDTH = 5
TOP_K = 8
N_GROUPS = 8
TOPK_GROUPS = 4
ROUTED_SCALE = 2.5

LANES = 128
SUBLANES = 8
VMEM_LIMIT = 56 * 1024 * 1024


def _cparams(*sem):
    return pltpu.CompilerParams(dimension_semantics=sem, vmem_limit_bytes=VMEM_LIMIT)


def _split3(x):
    hi = x.astype(BF16)
    r = x - hi.astype(F32)
    mid = r.astype(BF16)
    lo = (r - mid.astype(F32)).astype(BF16)
    return hi, mid, lo


def _dot(a, b):
    return jnp.dot(a, b, preferred_element_type=F32)


def _dot_nt(a, b):
    return lax.dot_general(a, b, (((1,), (1,)), ((), ())), preferred_element_type=F32)


def _dot_tn(a, b):
    return lax.dot_general(a, b, (((0,), (0,)), ((), ())), preferred_element_type=F32)


def _rms(x, g):
    return x * lax.rsqrt(jnp.mean(x * x, axis=-1, keepdims=True) + RMS_EPS) * g


def _pack_halves(x):
    n = x.shape[1] // 2
    return pltpu.pack_elementwise([x[:, :n], x[:, n:]], packed_dtype=BF16)


def _unpack_halves(w):
    lo = pltpu.unpack_elementwise(w, index=0, packed_dtype=BF16, unpacked_dtype=F32)
    hi = pltpu.unpack_elementwise(w, index=1, packed_dtype=BF16, unpacked_dtype=F32)
    return lo, hi


def _ada_kernel(c_ref, w_ref, b_ref, o_ref):
    c = c_ref[...]
    a = c * jax.nn.sigmoid(c)
    a_hi, a_mid, _ = _split3(a)
    w = w_ref[...]
    w_hi = w.astype(BF16)
    w_lo = (w - w_hi.astype(F32)).astype(BF16)
    o_ref[...] = _dot(a_hi, w_hi) + _dot(a_mid, w_hi) + _dot(a_hi, w_lo) + b_ref[...]


def _ada(c, ada_w, ada_b, tn=1024):
    nb, d = c.shape
    n = ada_w.shape[1]
    return pl.pallas_call(
        _ada_kernel,
        out_shape=jax.ShapeDtypeStruct((nb, n), F32),
        grid=(n // tn,),
        in_specs=[pl.BlockSpec((nb, d), lambda j: (0, 0)),
                  pl.BlockSpec((d, tn), lambda j: (0, j)),
                  pl.BlockSpec((1, tn), lambda j: (0, j))],
        out_specs=pl.BlockSpec((nb, tn), lambda j: (0, j)),
        compiler_params=_cparams("arbitrary"),
        name="ada",
    )(c, ada_w, ada_b.reshape(1, n))


def _inproj_kernel(n_gelu, n_plain, x_ref, g_ref, sc_ref, sh_ref, w_ref, wab_ref, z_ref, ab_ref, h_s):
    j = pl.program_id(1)

    @pl.when(j == 0)
    def _():
        h = _rms(x_ref[...], g_ref[...]) * (1.0 + sc_ref[0]) + sh_ref[0]
        h_s[...] = h.astype(BF16)
        ab_ref[...] = _dot(h_s[...], wab_ref[...])

    acc = _dot(h_s[...], w_ref[...])

    @pl.when(j < n_gelu)
    def _():
        z_ref[...] = jax.nn.gelu(acc).astype(BF16)

    @pl.when((j >= n_gelu) & (j < n_gelu + n_plain))
    def _():
        z_ref[...] = acc.astype(BF16)

    @pl.when(j >= n_gelu + n_plain)
    def _():
        z_ref[...] = jax.nn.sigmoid(acc).astype(BF16)


def _inproj(x, g, sc, sh, w_main, w_ab, seq, n_gelu, n_plain, tm=512, tn=1024):
    t, d = x.shape
    n = w_main.shape[1]
    tm = min(tm, seq)
    per = seq // tm
    return pl.pallas_call(
        functools.partial(_inproj_kernel, n_gelu, n_plain),
        out_shape=(jax.ShapeDtypeStruct((t, n), BF16), jax.ShapeDtypeStruct((t, LANES), F32)),
        grid=(t // tm, n // tn),
        in_specs=[pl.BlockSpec((tm, d), lambda i, j: (i, 0)),
                  pl.BlockSpec((1, d), lambda i, j: (0, 0)),
                  pl.BlockSpec((1, 1, d), lambda i, j: (i // per, 0, 0)),
                  pl.BlockSpec((1, 1, d), lambda i, j: (i // per, 0, 0)),
                  pl.BlockSpec((d, tn), lambda i, j: (0, j)),
                  pl.BlockSpec((d, LANES), lambda i, j: (0, 0))],
        out_specs=(pl.BlockSpec((tm, tn), lambda i, j: (i, j)),
                   pl.BlockSpec((tm, LANES), lambda i, j: (i, 0))),
        scratch_shapes=[pltpu.VMEM((tm, d), BF16)],
        compiler_params=_cparams("arbitrary", "arbitrary"),
        name="inproj",
    )(x, g, sc, sh, w_main, w_ab)


def _sgu_kernel(u_ref, v_ref, g_ref, b_ref, ws_ref, sb_ref, o_ref):
    v = v_ref[...].astype(F32)
    mu = jnp.mean(v, axis=-1, keepdims=True)
    vc = v - mu
    var = jnp.mean(vc * vc, axis=-1, keepdims=True)
    vn = (vc * lax.rsqrt(var + LN_EPS) * g_ref[...] + b_ref[...]).astype(BF16)
    tm = vn.shape[0]
    for c in range(tm // SGU_CHUNK):
        r = slice(c * SGU_CHUNK, (c + 1) * SGU_CHUNK)
        for gi in range(SGU_GROUPS):
            l = slice(gi * LANES, (gi + 1) * LANES)
            mixed = _dot(ws_ref[gi], vn[r, l]) + sb_ref[gi]
            o_ref[r, l] = (u_ref[r, l].astype(F32) * mixed).astype(BF16)


def _sgu(z, ln_g, ln_b, ws, sb_b, tm=512):
    t = z.shape[0]
    w = SGU_GROUPS * LANES
    tm = np.gcd(tm, t)
    return pl.pallas_call(
        _sgu_kernel,
        out_shape=jax.ShapeDtypeStruct((t, w), BF16),
        grid=(t // tm,),
        in_specs=[pl.BlockSpec((tm, w), lambda i: (i, 0)),
                  pl.BlockSpec((tm, w), lambda i: (i, 1)),
                  pl.BlockSpec((1, w), lambda i: (0, 0)),
                  pl.BlockSpec((1, w), lambda i: (0, 0)),
                  pl.BlockSpec((SGU_GROUPS, SGU_CHUNK, SGU_CHUNK), lambda i: (0, 0, 0)),
                  pl.BlockSpec((SGU_GROUPS, SGU_CHUNK, LANES), lambda i: (0, 0, 0))],
        out_specs=pl.BlockSpec((tm, w), lambda i: (i, 0)),
        compiler_params=_cparams("arbitrary"),
        name="sgu",
    )(z, z, ln_g, ln_b, ws, sb_b)


def _gate_kernel(ab_ref, alog_ref, dt_ref, gcol_ref, grow_ref):
    ab = ab_ref[...]
    tr = ab.shape[0]
    lane = lax.broadcasted_iota(I32, ab.shape, 1)
    z = ab + dt_ref[...]
    softplus = jnp.maximum(z, 0.0) + jnp.log1p(jnp.exp(-jnp.abs(z)))
    g = -jnp.exp(alog_ref[...]) * softplus
    beta = jax.nn.sigmoid(ab)
    nh = DN_HEADS
    g = jnp.where(lane < 2 * nh, g, 0.0)
    i = lax.broadcasted_iota(I32, (tr, tr), 0)
    j = lax.broadcasted_iota(I32, (tr, tr), 1)
    same = (i // DN_CHUNK) == (j // DN_CHUNK)
    lower = jnp.where(same & (j <= i), 1.0, 0.0).astype(BF16)
    upper = jnp.where(same & (j >= i), 1.0, 0.0).astype(BF16)
    g_hi, g_mid, g_lo = _split3(g)
    pre = _dot(lower, g_hi) + _dot(lower, g_mid) + _dot(lower, g_lo)
    suf = _dot(upper, g_hi) + _dot(upper, g_mid) + _dot(upper, g_lo)
    out = jnp.where(lane < nh, pre, jnp.where(lane < 2 * nh, suf, jnp.where(lane < 4 * nh, beta, 0.0)))
    gcol_ref[...] = out
    grow_ref[0] = out.T[:2 * nh, :]


def _gates(ab, alog_row, dt_row, nb, seq, tr=256):
    t = ab.shape[0]
    per = seq // tr
    return pl.pallas_call(
        _gate_kernel,
        out_shape=(jax.ShapeDtypeStruct((t, LANES), F32),
                   jax.ShapeDtypeStruct((nb, 2 * DN_HEADS, seq), F32)),
        grid=(t // tr,),
        in_specs=[pl.BlockSpec((tr, LANES), lambda i: (i, 0)),
                  pl.BlockSpec((1, LANES), lambda i: (0, 0)),
                  pl.BlockSpec((1, LANES), lambda i: (0, 0))],
        out_specs=(pl.BlockSpec((tr, LANES), lambda i: (i, 0)),
                   pl.BlockSpec((1, 2 * DN_HEADS, tr), lambda i: (i // per, 0, i % per))),
        compiler_params=_cparams("arbitrary"),
        name="gates",
    )(ab, alog_row, dt_row)


def _conv_silu(x_ref, w_ref):
    x = x_ref[...].astype(F32)
    s = x.shape[0]
    row = lax.broadcasted_iota(I32, x.shape, 0)
    pad = (CONV_WIDTH - 1) // 2
    acc = x * w_ref[pad:pad + 1, :]
    for j in range(CONV_WIDTH):
        d = j - pad
        if d == 0:
            continue
        xs = pltpu.roll(x, (-d) % s, 0)
        ok = (row + d >= 0) & (row + d < s)
        acc = acc + jnp.where(ok, xs, 0.0) * w_ref[j:j + 1, :]
    return acc * jax.nn.sigmoid(acc)


def _l2n(x):
    return x * lax.rsqrt(jnp.sum(x * x, axis=-1, keepdims=True) + L2_EPS)


def _unit_tri_inverse(a):
    c = a.shape[0]
    eye = jnp.where(lax.broadcasted_iota(I32, (c, c), 0) == lax.broadcasted_iota(I32, (c, c), 1), 1.0, 0.0)
    xp = -a
    p = eye + xp
    m = 2
    while m < c:
        xb = xp.astype(BF16)
        xp = _dot(xb, xb)
        p = p + _dot(p.astype(BF16), xp.astype(BF16))
        m *= 2
    return p


def _dn_chunk(q, k, v, gcol, grow, beta, state, forward):
    c = q.shape[0]
    i = lax.broadcasted_iota(I32, (c, c), 0)
    j = lax.broadcasted_iota(I32, (c, c), 1)
    incl = (i >= j) if forward else (i <= j)
    strict = (i > j) if forward else (i < j)
    dec = jnp.where(incl, jnp.exp(gcol[:, :c] - grow), 0.0)
    kb = k * beta
    kbf = k.astype(BF16)
    a = jnp.where(strict, _dot_nt(kb.astype(BF16), kbf) * dec, 0.0)
    tinv = _unit_tri_inverse(a).astype(BF16)
    eg = jnp.exp(gcol)
    u = _dot(tinv, (v * beta).astype(BF16))
    w = _dot(tinv, (kb * eg).astype(BF16))
    attn = _dot_nt(q.astype(BF16), kbf) * dec
    glast = gcol[c - 1:c, :] if forward else gcol[0:1, :]
    k_dec = k * jnp.exp(glast - gcol)
    sb = state.astype(BF16)
    wq = jnp.concatenate([w, q * eg], axis=0).astype(BF16)
    r = _dot(wq, sb)
    v_new = u - r[:c]
    vb = v_new.astype(BF16)
    o = r[c:] + _dot(attn.astype(BF16), vb)
    state = state * jnp.exp(glast) + _dot_tn(k_dec.astype(BF16), vb)
    return o, state


def _dn_kernel(q_ref, k_ref, v_ref, og_ref, wq_ref, wk_ref, wv_ref, gcol_ref, grow_ref, ng_ref, o_ref,
               q_s, k_s, v_s, gf_s, gb_s, bf_s, bb_s, of_s, ob_s):
    h = pl.program_id(1)
    nh = DN_HEADS
    hd = DN_HEAD_DIM
    q_s[...] = _l2n(_conv_silu(q_ref, wq_ref)) * (hd ** -0.5)
    k_s[...] = _l2n(_conv_silu(k_ref, wk_ref))
    v_s[...] = _conv_silu(v_ref, wv_ref)

    g_hi, g_mid, g_lo = _split3(gcol_ref[...])
    rr = lax.broadcasted_iota(I32, (LANES, LANES), 0)
    for off, dst in ((0, gf_s), (nh, gb_s), (2 * nh, bf_s), (3 * nh, bb_s)):
        sel = jnp.where(rr == h + off, 1.0, 0.0).astype(BF16)
        dst[...] = _dot(g_hi, sel) + _dot(g_mid, sel) + _dot(g_lo, sel)

    s = q_s.shape[0]
    c = DN_CHUNK
    n = s // c

    def body(it, carry):
        st_f, st_b = carry
        rf = pl.multiple_of(it * c, c)
        rb = pl.multiple_of((n - 1 - it) * c, c)
        o, st_f = _dn_chunk(q_s[pl.ds(rf, c), :], k_s[pl.ds(rf, c), :], v_s[pl.ds(rf, c), :],
                            gf_s[pl.ds(rf, c), :], grow_ref[0, h, pl.ds(it, 1), :],
                            bf_s[pl.ds(rf, c), :], st_f, True)
        of_s[pl.ds(rf, c), :] = o
        o, st_b = _dn_chunk(q_s[pl.ds(rb, c), :], k_s[pl.ds(rb, c), :], v_s[pl.ds(rb, c), :],
                            gb_s[pl.ds(rb, c), :], grow_ref[0, nh + h, pl.ds(n - 1 - it, 1), :],
                            bb_s[pl.ds(rb, c), :], st_b, False)
        ob_s[pl.ds(rb, c), :] = o
        return st_f, st_b

    zero = jnp.zeros((hd, hd), F32)
    lax.fori_loop(0, n, body, (zero, zero))

    o = of_s[...] + ob_s[...]
    og = og_ref[...].astype(F32)
    o_ref[...] = (_rms(o, ng_ref[...]) * (og * jax.nn.sigmoid(og))).astype(BF16)


def _deltanet(z, conv_w, gcol, grow4, norm_g, nb, seq, qkv_blk, og_blk):
    t = z.shape[0]
    nh, hd, c = DN_HEADS, DN_HEAD_DIM, DN_CHUNK
    zspec = lambda off: pl.BlockSpec((seq, hd), lambda b, h: (b, off + h))
    wspec = lambda off: pl.BlockSpec((CONV_WIDTH, hd), lambda b, h: (0, off + h))
    return pl.pallas_call(
        _dn_kernel,
        out_shape=jax.ShapeDtypeStruct((t, nh * hd), BF16),
        grid=(nb, nh),
        in_specs=[zspec(qkv_blk), zspec(qkv_blk + nh), zspec(qkv_blk + 2 * nh), zspec(og_blk),
                  wspec(0), wspec(nh), wspec(2 * nh),
                  pl.BlockSpec((seq, LANES), lambda b, h: (b, 0)),
                  pl.BlockSpec((1, 2 * nh, seq // c, c), lambda b, h: (b, 0, 0, 0)),
                  pl.BlockSpec((1, hd), lambda b, h: (0, 0))],
        out_specs=pl.BlockSpec((seq, hd), lambda b, h: (b, h)),
        scratch_shapes=[pltpu.VMEM((seq, hd), F32)] * 9,
        compiler_params=_cparams("arbitrary", "arbitrary"),
        name="deltanet",
    )(z, z, z, z, conv_w, conv_w, conv_w, gcol, grow4, norm_g)


def _merge_kernel(a_ref, b_ref, ga_ref, gb_ref, x_ref, gt_ref, wa_ref, wb_ref, wo_ref, g2_ref, sc_ref, sh_ref,
                  x1_ref, h2_ref):
    ya = _dot(a_ref[...], wa_ref[...])
    yb = _dot(b_ref[...], wb_ref[...])
    m = ga_ref[...].astype(F32) * ya + gb_ref[...].astype(F32) * yb
    y = _dot(m.astype(BF16), wo_ref[...])
    x1 = x_ref[...] + gt_ref[0] * y
    x1_ref[...] = x1
    h2 = _rms(x1, g2_ref[...]) * (1.0 + sc_ref[0]) + sh_ref[0]
    h2_ref[...] = _pack_halves(h2)


def _merge(a_out, b_out, z, x, gt1, wa, wb, wo, g2, sc2, sh2, seq, gate_blk, tm=512):
    t, d = x.shape
    wdt = a_out.shape[1]
    tm = min(tm, seq)
    per = seq // tm
    bvec = pl.BlockSpec((1, 1, d), lambda i: (i // per, 0, 0))
    const = lambda shp: pl.BlockSpec(shp, lambda i: (0, 0), pipeline_mode=pl.Buffered(1))
    return pl.pallas_call(
        _merge_kernel,
        out_shape=(jax.ShapeDtypeStruct((t, d), F32), jax.ShapeDtypeStruct((t, d // 2), U32)),
        grid=(t // tm,),
        in_specs=[pl.BlockSpec((tm, wdt), lambda i: (i, 0)),
                  pl.BlockSpec((tm, wdt), lambda i: (i, 0)),
                  pl.BlockSpec((tm, d), lambda i: (i, gate_blk)),
                  pl.BlockSpec((tm, d), lambda i: (i, gate_blk + 1)),
                  pl.BlockSpec((tm, d), lambda i: (i, 0)),
                  bvec, const((wdt, d)), const((wdt, d)), const((d, d)),
                  pl.BlockSpec((1, d), lambda i: (0, 0)), bvec, bvec],
        out_specs=(pl.BlockSpec((tm, d), lambda i: (i, 0)), pl.BlockSpec((tm, d // 2), lambda i: (i, 0))),
        compiler_params=_cparams("arbitrary"),
        name="merge",
    )(a_out, b_out, z, z, x, gt1, wa, wb, wo, g2, sc2, sh2)


def _prep_mixer(w, nb, seq):
    wdt = SGU_GROUPS * LANES
    nh, hd = DN_HEADS, DN_HEAD_DIM
    w_in = w["w_in"][0]
    d = w_in.shape[0]
    ab0 = 2 * wdt + 4 * nh * hd
    w_main = jnp.concatenate([w_in[:, :ab0], w_in[:, ab0 + 4 * nh:]], axis=1).astype(BF16)
    w_ab = jnp.pad(w_in[:, ab0:ab0 + 4 * nh], ((0, 0), (0, LANES - 4 * nh))).astype(BF16)
    pad_row = lambda v: jnp.pad(v.reshape(1, 2 * nh), ((0, 0), (0, LANES - 2 * nh)))
    sb = w["sgu_b"][0]
    wbr = w["w_branch"][0]
    return dict(
        nb=nb, seq=seq,
        norm_mix_g=w["norm_mix_g"][0].reshape(1, d), norm_ffn_g=w["norm_ffn_g"][0].reshape(1, d),
        w_main=w_main, w_ab=w_ab,
        sgu_ln_g=w["sgu_ln_g"][0].reshape(1, wdt), sgu_ln_b=w["sgu_ln_b"][0].reshape(1, wdt),
        sgu_ws=w["sgu_ws"][0].astype(BF16),
        sgu_sb=jnp.broadcast_to(sb[:, :, None], sb.shape + (LANES,)),
        alog_row=pad_row(w["dn_a_log"][0]), dt_row=pad_row(w["dn_dt_bias"][0]),
        dn_conv_w=w["dn_conv_w"][0], dn_norm_g=w["dn_norm_g"][0].reshape(1, hd),
        wa=wbr[:wdt].astype(BF16), wb=wbr[wdt:].astype(BF16), wo=w["w_out"][0].astype(BF16),
    )


def _token_mixer_stage(x, mods, p):
    nb, seq = p["nb"], p["seq"]
    t, d = x.shape
    sh1, sc1, gt1, sh2, sc2, _ = mods
    wdt = SGU_GROUPS * LANES
    nh = DN_HEADS
    z, ab = _inproj(x, p["norm_mix_g"], sc1, sh1, p["w_main"], p["w_ab"], seq,
                    n_gelu=2 * wdt // 1024, n_plain=(3 * nh * DN_HEAD_DIM + nh * DN_HEAD_DIM) // 1024)
    a_out = _sgu(z, p["sgu_ln_g"], p["sgu_ln_b"], p["sgu_ws"], p["sgu_sb"])
    gcol, grow = _gates(ab, p["alog_row"], p["dt_row"], nb, seq)
    grow4 = grow.reshape(nb, 2 * nh, seq // DN_CHUNK, DN_CHUNK)
    qkv_blk = 2 * wdt // LANES
    b_out = _deltanet(z, p["dn_conv_w"], gcol, grow4, p["dn_norm_g"], nb, seq,
                      qkv_blk=qkv_blk, og_blk=qkv_blk + 3 * nh)
    gate_blk = (2 * wdt + 4 * nh * DN_HEAD_DIM) // d
    return _merge(a_out, b_out, z, x, gt1, p["wa"], p["wb"], p["wo"], p["norm_ffn_g"], sc2, sh2, seq, gate_blk)


def _first_argmax(vals, idx, n, axis):
    mx = jnp.max(vals, axis=axis, keepdims=True)
    ix = jnp.min(jnp.where(vals == mx, idx, n), axis=axis, keepdims=True)
    return mx, ix


def _router_kernel(x_ref, g_ref, sc_ref, sh_ref, whi_ref, wlo_ref, bias_ref,
                   eidx_ref, wts_ref, rank_ref, cnt_ref, base_s):
    i = pl.program_id(0)
    ne, tm = bias_ref.shape
    neg = -jnp.inf

    @pl.when(i == 0)
    def _():
        base_s[...] = jnp.zeros_like(base_s)

    h = _rms(x_ref[...], g_ref[...]) * (1.0 + sc_ref[0]) + sh_ref[0]
    h_hi, h_mid, _ = _split3(h)
    whi = whi_ref[...]
    logits = _dot_nt(whi, h_hi) + _dot_nt(wlo_ref[...], h_hi) + _dot_nt(whi, h_mid)
    scores = jax.nn.sigmoid(logits)
    sel = scores + bias_ref[...]

    per = ne // N_GROUPS
    sel3 = sel.reshape(N_GROUPS, per, tm)
    ri = lax.broadcasted_iota(I32, sel3.shape, 1)
    m1, i1 = _first_argmax(sel3, ri, per, 1)
    m2 = jnp.max(jnp.where(ri == i1, neg, sel3), axis=1, keepdims=True)
    grp = (m1 + m2).reshape(N_GROUPS, tm)

    gi = lax.broadcasted_iota(I32, grp.shape, 0)
    chosen = jnp.zeros(grp.shape, F32)
    for _ in range(TOPK_GROUPS):
        _, ix = _first_argmax(grp, gi, N_GROUPS, 0)
        hit = gi == ix
        chosen = jnp.where(hit, 1.0, chosen)
        grp = jnp.where(hit, neg, grp)
    masked = jnp.where(chosen.reshape(N_GROUPS, 1, tm) > 0.0, sel3, neg).reshape(ne, tm)

    ei = lax.broadcasted_iota(I32, (ne, tm), 0)
    msel = jnp.zeros((ne, tm), F32)
    idx_rows, w_rows = [], []
    for _ in range(TOP_K):
        _, ix = _first_argmax(masked, ei, ne, 0)
        hit = ei == ix
        w_rows.append(jnp.sum(jnp.where(hit, scores, 0.0), axis=0, keepdims=True))
        idx_rows.append(ix)
        msel = jnp.where(hit, 1.0, msel)
        masked = jnp.where(hit, neg, masked)
    w = jnp.concatenate(w_rows, axis=0)
    eidx_ref[...] = jnp.concatenate(idx_rows, axis=0)
    wts_ref[...] = w / (jnp.sum(w, axis=0, keepdims=True) + 1e-20) * ROUTED_SCALE

    mb = msel.astype(BF16)
    a = lax.broadcasted_iota(I32, (tm, tm), 0)
    b = lax.broadcasted_iota(I32, (tm, tm), 1)
    before = _dot(mb, jnp.where(a < b, 1.0, 0.0).astype(BF16))
    pos = base_s[...] + before
    rank_ref[...] = jnp.concatenate(
        [jnp.sum(jnp.where(ei == ix, pos, 0.0), axis=0, keepdims=True) for ix in idx_rows], axis=0).astype(I32)
    base_s[...] = base_s[...] + _dot(mb, jnp.ones((tm, tm), BF16))
    cnt_ref[...] = base_s[:, :LANES].astype(I32)


def _router(x1, g2, sc2, sh2, wt_hi, wt_lo, bias, seq, tm=512):
    t, d = x1.shape
    ne = wt_hi.shape[0]
    tm = min(tm, seq)
    per = seq // tm
    bvec = pl.BlockSpec((1, 1, d), lambda i: (i // per, 0, 0))
    kt = lambda dt: jax.ShapeDtypeStruct((TOP_K, t), dt)
    kspec = pl.BlockSpec((TOP_K, tm), lambda i: (0, i))
    return pl.pallas_call(
        _router_kernel,
        out_shape=(kt(I32), kt(F32), kt(I32), jax.ShapeDtypeStruct((ne, LANES), I32)),
        grid=(t // tm,),
        in_specs=[pl.BlockSpec((tm, d), lambda i: (i, 0)),
                  pl.BlockSpec((1, d), lambda i: (0, 0)), bvec, bvec,
                  pl.BlockSpec((ne, d), lambda i: (0, 0)),
                  pl.BlockSpec((ne, d), lambda i: (0, 0)),
                  pl.BlockSpec((ne, tm), lambda i: (0, 0))],
        out_specs=(kspec, kspec, kspec, pl.BlockSpec((ne, LANES), lambda i: (0, 0))),
        scratch_shapes=[pltpu.VMEM((ne, tm), F32)],
        compiler_params=_cparams("arbitrary"),
        name="router",
    )(x1, g2, sc2, sh2, wt_hi, wt_lo, jnp.broadcast_to(bias.reshape(ne, 1), (ne, tm)))


def _row_copy_all(src_of, dst_of, sem, eidx_ref, rank_ref, pstart_ref, tm):
    def issue(t, carry):
        for k in range(TOP_K):
            row = pstart_ref[eidx_ref[k, t]] + rank_ref[k, t]
            pltpu.make_async_copy(src_of(k, t, row), dst_of(k, t, row), sem).start()
        return carry
    lax.fori_loop(0, tm, issue, 0)


def _dispatch_kernel(pstart_ref, cnt_ref, nused_ref, h_ref, eidx_ref, rank_ref, xs_ref, zbuf, sem, zsem):
    i = pl.program_id(0)
    tm = h_ref.shape[0]
    bm = zbuf.shape[0]
    ne = cnt_ref.shape[0]
    nblk = xs_ref.shape[0] // bm

    def zero_fill(act):
        def per_expert(e, carry):
            c = cnt_ref[e]
            pad = lax.rem(bm - lax.rem(c, bm), bm)
            off = pstart_ref[e] + c
            head = pad & (SUBLANES - 1)
            for r in range(SUBLANES - 1):
                @pl.when(r < head)
                def _(r=r):
                    act(pltpu.make_async_copy(zbuf.at[pl.ds(0, 1), :], xs_ref.at[pl.ds(off + r, 1), :], zsem))
            off = off + head
            s = bm // 2
            while s >= SUBLANES:
                @pl.when((pad & s) != 0)
                def _(s=s, off=off):
                    dst = xs_ref.at[pl.ds(pl.multiple_of(off, SUBLANES), s), :]
                    act(pltpu.make_async_copy(zbuf.at[pl.ds(0, s), :], dst, zsem))
                off = off + (pad & s)
                s //= 2
            return carry
        lax.fori_loop(0, ne, per_expert, 0)

        def per_block(j, carry):
            act(pltpu.make_async_copy(zbuf, xs_ref.at[pl.ds(pl.multiple_of(j * bm, bm), bm), :], zsem))
            return carry
        lax.fori_loop(nused_ref[0], nblk, per_block, 0)

    @pl.when(i == 0)
    def _():
        zbuf[...] = jnp.zeros_like(zbuf)
        zero_fill(lambda cp: cp.start())

    _row_copy_all(lambda k, t, row: h_ref.at[pl.ds(t, 1), :],
                  lambda k, t, row: xs_ref.at[pl.ds(row, 1), :],
                  sem, eidx_ref, rank_ref, pstart_ref, tm)
    for _ in range(TOP_K):
        pltpu.make_async_copy(h_ref, xs_ref.at[pl.ds(0, tm), :], sem).wait()

    @pl.when(i == 0)
    def _():
        zero_fill(lambda cp: cp.wait())


def _dispatch(h2p, eidx, rank, pstart, counts, n_used, n_rows, bm, tm=256):
    t, dw = h2p.shape
    tm = min(tm, t)
    smem = lambda: pl.BlockSpec((TOP_K, tm), lambda i, *_: (0, i), memory_space=pltpu.SMEM)
    return pl.pallas_call(
        _dispatch_kernel,
        out_shape=jax.ShapeDtypeStruct((n_rows, dw), U32),
        grid_spec=pltpu.PrefetchScalarGridSpec(
            num_scalar_prefetch=3,
            grid=(t // tm,),
            in_specs=[pl.BlockSpec((tm, dw), lambda i, *_: (i, 0)), smem(), smem()],
            out_specs=pl.BlockSpec(memory_space=pl.ANY),
            scratch_shapes=[pltpu.VMEM((bm, dw), U32), pltpu.SemaphoreType.DMA(()), pltpu.SemaphoreType.DMA(())]),
        compiler_params=_cparams("arbitrary"),
        name="dispatch",
    )(pstart, counts, n_used, h2p, eidx, rank)


def _expert_kernel(be_ref, nu_ref, xs_ref, wg_ref, wu_ref, wd_ref, y_ref):
    j = pl.program_id(0)

    @pl.when(j < nu_ref[0])
    def _():
        lo, hi = _unpack_halves(xs_ref[...])
        lo, hi = lo.astype(BF16), hi.astype(BF16)
        n = lo.shape[1]
        wg = wg_ref[...].astype(BF16)
        wu = wu_ref[...].astype(BF16)
        g = _dot(lo, wg[:n]) + _dot(hi, wg[n:])
        u = _dot(lo, wu[:n]) + _dot(hi, wu[n:])
        a = (g * jax.nn.sigmoid(g) * u).astype(BF16)
        y_ref[...] = _pack_halves(_dot(a, wd_ref[...].astype(BF16)))

    @pl.when(j >= nu_ref[0])
    def _():
        y_ref[...] = jnp.zeros_like(y_ref)


def _experts(xs, block_e, n_used, w_gate, w_up, w_down, bm):
    n_rows, dw = xs.shape
    ne, d, f = w_gate.shape
    nblk = n_rows // bm
    live = lambda j, nu: jnp.minimum(j, nu[0] - 1)
    return pl.pallas_call(
        _expert_kernel,
        out_shape=jax.ShapeDtypeStruct((n_rows, dw), U32),
        grid_spec=pltpu.PrefetchScalarGridSpec(
            num_scalar_prefetch=2,
            grid=(nblk,),
            in_specs=[pl.BlockSpec((bm, dw), lambda j, be, nu: (live(j, nu), 0)),
                      pl.BlockSpec((None, d, f), lambda j, be, nu: (be[j], 0, 0)),
                      pl.BlockSpec((None, d, f), lambda j, be, nu: (be[j], 0, 0)),
                      pl.BlockSpec((None, f, d), lambda j, be, nu: (be[j], 0, 0))],
            out_specs=pl.BlockSpec((bm, dw), lambda j, be, nu: (j, 0))),
        compiler_params=_cparams("arbitrary"),
        name="experts",
    )(block_e, n_used, xs, w_gate, w_up, w_down)


def _final_kernel(pstart_ref, x1_ref, h_ref, gt_ref, wts_ref, fg_ref, wg_ref, wu_ref, wd_ref,
                  eidx_ref, rank_ref, y_ref, o_ref, ybuf, sem):
    tm = x1_ref.shape[0]
    _row_copy_all(lambda k, t, row: y_ref.at[pl.ds(row, 1), :],
                  lambda k, t, row: ybuf.at[k, pl.ds(t, 1), :],
                  sem, eidx_ref, rank_ref, pstart_ref, tm)

    lo, hi = _unpack_halves(h_ref[...])
    lo, hi = lo.astype(BF16), hi.astype(BF16)
    n = lo.shape[1]
    g = _dot(lo, wg_ref[:n, :]) + _dot(hi, wg_ref[n:, :])
    u = _dot(lo, wu_ref[:n, :]) + _dot(hi, wu_ref[n:, :])
    shared = _dot((g * jax.nn.sigmoid(g) * u).astype(BF16), wd_ref[...])

    for k in range(TOP_K):
        pltpu.make_async_copy(y_ref.at[pl.ds(0, tm), :], ybuf.at[k], sem).wait()
    acc_lo, acc_hi = shared[:, :n], shared[:, n:]
    for k in range(TOP_K):
        ylo, yhi = _unpack_halves(ybuf[k])
        wk = wts_ref[:, k:k + 1]
        acc_lo = acc_lo + wk * ylo
        acc_hi = acc_hi + wk * yhi
    gt = gt_ref[0]
    x_lo = x1_ref[:, :n] + gt[:, :n] * acc_lo
    x_hi = x1_ref[:, n:] + gt[:, n:] * acc_hi
    ms = (jnp.sum(x_lo * x_lo, axis=-1, keepdims=True) + jnp.sum(x_hi * x_hi, axis=-1, keepdims=True)) / (2 * n)
    r = lax.rsqrt(ms + RMS_EPS)
    o_ref[:, :n] = x_lo * r * fg_ref[:, :n]
    o_ref[:, n:] = x_hi * r * fg_ref[:, n:]


def _final(x1, h2p, gt2, wts_t, final_g, sh_wg, sh_wu, sh_wd, eidx, rank, y, pstart, seq, tm=256):
    t, d = x1.shape
    dw = h2p.shape[1]
    f = sh_wg.shape[1]
    tm = min(tm, seq)
    per = seq // tm
    smem = lambda: pl.BlockSpec((TOP_K, tm), lambda i, ps: (0, i), memory_space=pltpu.SMEM)
    const = lambda shp: pl.BlockSpec(shp, lambda i, ps: (0, 0), pipeline_mode=pl.Buffered(1))
    return pl.pallas_call(
        _final_kernel,
        out_shape=jax.ShapeDtypeStruct((t, d), F32),
        grid_spec=pltpu.PrefetchScalarGridSpec(
            num_scalar_prefetch=1,
            grid=(t // tm,),
            in_specs=[pl.BlockSpec((tm, d), lambda i, ps: (i, 0)),
                      pl.BlockSpec((tm, dw), lambda i, ps: (i, 0)),
                      pl.BlockSpec((1, 1, d), lambda i, ps: (i // per, 0, 0)),
                      pl.BlockSpec((tm, TOP_K), lambda i, ps: (i, 0)),
                      pl.BlockSpec((1, d), lambda i, ps: (0, 0)),
                      const((d, f)), const((d, f)), const((f, d)),
                      smem(), smem(),
                      pl.BlockSpec(memory_space=pl.ANY)],
            out_specs=pl.BlockSpec((tm, d), lambda i, ps: (i, 0)),
            scratch_shapes=[pltpu.VMEM((TOP_K, tm, dw), U32), pltpu.SemaphoreType.DMA(())]),
        compiler_params=_cparams("arbitrary"),
        name="final",
    )(pstart, x1, h2p, gt2, wts_t, final_g, sh_wg, sh_wu, sh_wd, eidx, rank, y)


EXPERT_ROWS = 256


def _prep_moe(w):
    rw = w["router_w"][0]
    d = rw.shape[0]
    wt = rw.T
    wt_hi = wt.astype(BF16)
    return dict(
        wt_hi=wt_hi, wt_lo=(wt - wt_hi.astype(F32)).astype(BF16), bias=w["router_bias"][0],
        w_gate=w["exp_w_gate"][0], w_up=w["exp_w_up"][0], w_down=w["exp_w_down"][0],
        sh_wg=w["sh_w_gate"][0].astype(BF16), sh_wu=w["sh_w_up"][0].astype(BF16),
        sh_wd=w["sh_w_down"][0].astype(BF16), final_g=w["final_g"].reshape(1, d),
    )


def _moe_stage(x1, h2p, mods, pm, pw, seq):
    _, _, _, sh2, sc2, gt2 = mods
    t = x1.shape[0]
    ne = pw["wt_hi"].shape[0]
    bm = EXPERT_ROWS
    eidx, wts, rank, cnt = _router(x1, pm["norm_ffn_g"], sc2, sh2, pw["wt_hi"], pw["wt_lo"], pw["bias"], seq)
    counts = cnt[:, 0]
    pcounts = (counts + bm - 1) // bm * bm
    pends = jnp.cumsum(pcounts)
    pstart = (pends - pcounts).astype(I32)
    nblk = (t * TOP_K + ne * (bm - 1) + bm - 1) // bm
    n_used = (pends[-1:] // bm).astype(I32)
    jb = jnp.arange(nblk, dtype=I32)
    be = jnp.minimum(jnp.searchsorted(pends, jb * bm, side="right"), ne - 1).astype(I32)
    be = jnp.where(jb < n_used[0], be, be[jnp.maximum(n_used[0] - 1, 0)])
    xs = _dispatch(h2p, eidx, rank, pstart, counts, n_used, nblk * bm, bm)
    y = _experts(xs, be, n_used, pw["w_gate"], pw["w_up"], pw["w_down"], bm)
    return _final(x1, h2p, gt2, wts.T, pw["final_g"], pw["sh_wg"], pw["sh_wu"], pw["sh_wd"],
                  eidx, rank, y, pstart, seq)


def kernel(x_prompt, x_sample, c_prompt, c_sample, ada_w, ada_b, norm_mix_g, norm_ffn_g, w_in, sgu_ln_g,
           sgu_ln_b, sgu_ws, sgu_b, dn_conv_w, dn_a_log, dn_dt_bias, dn_norm_g, w_branch, w_out, router_w,
           router_bias, exp_w_gate, exp_w_up, exp_w_down, sh_w_gate, sh_w_up, sh_w_down, final_g):
    w = dict(ada_w=ada_w, ada_b=ada_b, norm_mix_g=norm_mix_g, norm_ffn_g=norm_ffn_g, w_in=w_in,
             sgu_ln_g=sgu_ln_g, sgu_ln_b=sgu_ln_b, sgu_ws=sgu_ws, sgu_b=sgu_b, dn_conv_w=dn_conv_w,
             dn_a_log=dn_a_log, dn_dt_bias=dn_dt_bias, dn_norm_g=dn_norm_g, w_branch=w_branch, w_out=w_out,
             router_w=router_w, router_bias=router_bias, exp_w_gate=exp_w_gate, exp_w_up=exp_w_up,
             exp_w_down=exp_w_down, sh_w_gate=sh_w_gate, sh_w_up=sh_w_up, sh_w_down=sh_w_down, final_g=final_g)
    assert w_in.shape[0] == 1, "one layer"
    bp, seq, d = x_prompt.shape
    bs = x_sample.shape[0]
    assert x_sample.shape[1] == seq
    nb = bp + bs
    x = jnp.concatenate([x_prompt.reshape(bp * seq, d), x_sample.reshape(bs * seq, d)], axis=0)
    c = jnp.concatenate([c_prompt, c_sample], axis=0)
    npad = -nb % 8
    mod = _ada(jnp.pad(c, ((0, npad), (0, 0))), ada_w[0], ada_b[0])[:nb]
    mods = [m.reshape(nb, 1, d) for m in jnp.split(mod, 6, axis=-1)]
    pm = _prep_mixer(w, nb, seq)
    pw = _prep_moe(w)
    x1, h2p = _token_mixer_stage(x, mods, pm)
    y = _moe_stage(x1, h2p, mods, pm, pw, seq)
    return (y[:bp * seq].reshape(bp, seq, d), y[bp * seq:].reshape(bs, seq, d))
```

```python
import functools

import jax
import jax.numpy as jnp
import numpy as np
from jax import lax
from jax.experimental import pallas as pl
from jax.experimental.pallas import tpu as pltpu

F32 = jnp.float32
BF16 = jnp.bfloat16
I32 = jnp.int32
U32 = jnp.uint32

RMS_EPS = 1e-6
LN_EPS = 1e-5
L2_EPS = 1e-6

SGU_GROUPS = 8
SGU_CHUNK = 128
DN_HEADS = 8
DN_HEAD_DIM = 128
DN_CHUNK = 256
DN_PREP_UNROLL = 2
CONV_WIDTH = 5
TOP_K = 8
N_GROUPS = 8
TOPK_GROUPS = 4
ROUTED_SCALE = 2.5

LANES = 128
SUBLANES = 8
VMEM_LIMIT = 56 * 1024 * 1024


def _cparams(*sem):
    return pltpu.CompilerParams(dimension_semantics=sem, vmem_limit_bytes=VMEM_LIMIT)


def _split3(x):
    hi = x.astype(BF16)
    r = x - hi.astype(F32)
    mid = r.astype(BF16)
    lo = (r - mid.astype(F32)).astype(BF16)
    return hi, mid, lo


def _dot(a, b):
    return jnp.dot(a, b, preferred_element_type=F32)


def _dot_nt(a, b):
    return lax.dot_general(a, b, (((1,), (1,)), ((), ())), preferred_element_type=F32)


def _dot_tn(a, b):
    return lax.dot_general(a, b, (((0,), (0,)), ((), ())), preferred_element_type=F32)


def _rms(x, g):
    return x * lax.rsqrt(jnp.mean(x * x, axis=-1, keepdims=True) + RMS_EPS) * g


def _pack_halves(x):
    n = x.shape[1] // 2
    return pltpu.pack_elementwise([x[:, :n], x[:, n:]], packed_dtype=BF16)


def _unpack_halves(w):
    lo = pltpu.unpack_elementwise(w, index=0, packed_dtype=BF16, unpacked_dtype=F32)
    hi = pltpu.unpack_elementwise(w, index=1, packed_dtype=BF16, unpacked_dtype=F32)
    return lo, hi


def _ada_kernel(c_ref, w_ref, b_ref, o_ref):
    c = c_ref[...]
    a = c * jax.nn.sigmoid(c)
    a_hi, a_mid, _ = _split3(a)
    w = w_ref[...]
    w_hi = w.astype(BF16)
    w_lo = (w - w_hi.astype(F32)).astype(BF16)
    o_ref[...] = _dot(a_hi, w_hi) + _dot(a_mid, w_hi) + _dot(a_hi, w_lo) + b_ref[...]


def _ada(c, ada_w, ada_b, tn=1024):
    nb, d = c.shape
    n = ada_w.shape[1]
    return pl.pallas_call(
        _ada_kernel,
        out_shape=jax.ShapeDtypeStruct((nb, n), F32),
        grid=(n // tn,),
        in_specs=[pl.BlockSpec((nb, d), lambda j: (0, 0)),
                  pl.BlockSpec((d, tn), lambda j: (0, j)),
                  pl.BlockSpec((1, tn), lambda j: (0, j))],
        out_specs=pl.BlockSpec((nb, tn), lambda j: (0, j)),
        compiler_params=_cparams("arbitrary"),
        name="ada",
    )(c, ada_w, ada_b.reshape(1, n))


def _inproj_kernel(n_gelu, n_plain, x_ref, g_ref, sc_ref, sh_ref, w_ref, wab_ref, z_ref, ab_ref, h_s):
    j = pl.program_id(1)

    @pl.when(j == 0)
    def _():
        h = _rms(x_ref[...], g_ref[...]) * (1.0 + sc_ref[0]) + sh_ref[0]
        h_s[...] = h.astype(BF16)
        ab_ref[...] = _dot(h_s[...], wab_ref[...])

    acc = _dot(h_s[...], w_ref[...])

    @pl.when(j < n_gelu)
    def _():
        z_ref[...] = jax.nn.gelu(acc).astype(BF16)

    @pl.when((j >= n_gelu) & (j < n_gelu + n_plain))
    def _():
        z_ref[...] = acc.astype(BF16)

    @pl.when(j >= n_gelu + n_plain)
    def _():
        z_ref[...] = jax.nn.sigmoid(acc).astype(BF16)


def _inproj(x, g, sc, sh, w_main, w_ab, seq, n_gelu, n_plain, tm=512, tn=1024):
    t, d = x.shape
    n = w_main.shape[1]
    tm = min(tm, seq)
    per = seq // tm
    return pl.pallas_call(
        functools.partial(_inproj_kernel, n_gelu, n_plain),
        out_shape=(jax.ShapeDtypeStruct((t, n), BF16), jax.ShapeDtypeStruct((t, LANES), F32)),
        grid=(t // tm, n // tn),
        in_specs=[pl.BlockSpec((tm, d), lambda i, j: (i, 0)),
                  pl.BlockSpec((1, d), lambda i, j: (0, 0)),
                  pl.BlockSpec((1, 1, d), lambda i, j: (i // per, 0, 0)),
                  pl.BlockSpec((1, 1, d), lambda i, j: (i // per, 0, 0)),
                  pl.BlockSpec((d, tn), lambda i, j: (0, j)),
                  pl.BlockSpec((d, LANES), lambda i, j: (0, 0))],
        out_specs=(pl.BlockSpec((tm, tn), lambda i, j: (i, j)),
                   pl.BlockSpec((tm, LANES), lambda i, j: (i, 0))),
        scratch_shapes=[pltpu.VMEM((tm, d), BF16)],
        compiler_params=_cparams("arbitrary", "arbitrary"),
        name="inproj",
    )(x, g, sc, sh, w_main, w_ab)


def _sgu_kernel(u_ref, v_ref, g_ref, b_ref, ws_ref, sb_ref, o_ref):
    v = v_ref[...].astype(F32)
    mu = jnp.mean(v, axis=-1, keepdims=True)
    vc = v - mu
    var = jnp.mean(vc * vc, axis=-1, keepdims=True)
    vn = (vc * lax.rsqrt(var + LN_EPS) * g_ref[...] + b_ref[...]).astype(BF16)
    tm = vn.shape[0]
    for c in range(tm // SGU_CHUNK):
        r = slice(c * SGU_CHUNK, (c + 1) * SGU_CHUNK)
        for gi in range(SGU_GROUPS):
            l = slice(gi * LANES, (gi + 1) * LANES)
            mixed = _dot(ws_ref[gi], vn[r, l]) + sb_ref[gi]
            o_ref[r, l] = (u_ref[r, l].astype(F32) * mixed).astype(BF16)


def _sgu(z, ln_g, ln_b, ws, sb_b, tm=512):
    t = z.shape[0]
    w = SGU_GROUPS * LANES
    tm = np.gcd(tm, t)
    return pl.pallas_call(
        _sgu_kernel,
        out_shape=jax.ShapeDtypeStruct((t, w), BF16),
        grid=(t // tm,),
        in_specs=[pl.BlockSpec((tm, w), lambda i: (i, 0)),
                  pl.BlockSpec((tm, w), lambda i: (i, 1)),
                  pl.BlockSpec((1, w), lambda i: (0, 0)),
                  pl.BlockSpec((1, w), lambda i: (0, 0)),
                  pl.BlockSpec((SGU_GROUPS, SGU_CHUNK, SGU_CHUNK), lambda i: (0, 0, 0)),
                  pl.BlockSpec((SGU_GROUPS, SGU_CHUNK, LANES), lambda i: (0, 0, 0))],
        out_specs=pl.BlockSpec((tm, w), lambda i: (i, 0)),
        compiler_params=_cparams("arbitrary"),
        name="sgu",
    )(z, z, ln_g, ln_b, ws, sb_b)


def _gate_kernel(ab_ref, alog_ref, dt_ref, gcol_ref, grow_ref):
    ab = ab_ref[...]
    tr = ab.shape[0]
    lane = lax.broadcasted_iota(I32, ab.shape, 1)
    z = ab + dt_ref[...]
    softplus = jnp.maximum(z, 0.0) + jnp.log1p(jnp.exp(-jnp.abs(z)))
    g = -jnp.exp(alog_ref[...]) * softplus
    beta = jax.nn.sigmoid(ab)
    nh = DN_HEADS
    g = jnp.where(lane < 2 * nh, g, 0.0)
    i = lax.broadcasted_iota(I32, (tr, tr), 0)
    j = lax.broadcasted_iota(I32, (tr, tr), 1)
    same = (i // DN_CHUNK) == (j // DN_CHUNK)
    lower = jnp.where(same & (j <= i), 1.0, 0.0).astype(BF16)
    upper = jnp.where(same & (j >= i), 1.0, 0.0).astype(BF16)
    g_hi, g_mid, g_lo = _split3(g)
    pre = _dot(lower, g_hi) + _dot(lower, g_mid) + _dot(lower, g_lo)
    suf = _dot(upper, g_hi) + _dot(upper, g_mid) + _dot(upper, g_lo)
    out = jnp.where(lane < nh, pre, jnp.where(lane < 2 * nh, suf, jnp.where(lane < 4 * nh, beta, 0.0)))
    gcol_ref[...] = out
    grow_ref[0] = out.T[:2 * nh, :]


def _gates(ab, alog_row, dt_row, nb, seq, tr=256):
    t = ab.shape[0]
    per = seq // tr
    return pl.pallas_call(
        _gate_kernel,
        out_shape=(jax.ShapeDtypeStruct((t, LANES), F32),
                   jax.ShapeDtypeStruct((nb, 2 * DN_HEADS, seq), F32)),
        grid=(t // tr,),
        in_specs=[pl.BlockSpec((tr, LANES), lambda i: (i, 0)),
                  pl.BlockSpec((1, LANES), lambda i: (0, 0)),
                  pl.BlockSpec((1, LANES), lambda i: (0, 0))],
        out_specs=(pl.BlockSpec((tr, LANES), lambda i: (i, 0)),
                   pl.BlockSpec((1, 2 * DN_HEADS, tr), lambda i: (i // per, 0, i % per))),
        compiler_params=_cparams("arbitrary"),
        name="gates",
    )(ab, alog_row, dt_row)


def _conv_silu(x_ref, w_ref):
    x = x_ref[...].astype(F32)
    s = x.shape[0]
    row = lax.broadcasted_iota(I32, x.shape, 0)
    pad = (CONV_WIDTH - 1) // 2
    acc = x * w_ref[pad:pad + 1, :]
    for j in range(CONV_WIDTH):
        d = j - pad
        if d == 0:
            continue
        xs = pltpu.roll(x, (-d) % s, 0)
        ok = (row + d >= 0) & (row + d < s)
        acc = acc + jnp.where(ok, xs, 0.0) * w_ref[j:j + 1, :]
    return acc * jax.nn.sigmoid(acc)


def _l2n(x):
    return x * lax.rsqrt(jnp.sum(x * x, axis=-1, keepdims=True) + L2_EPS)


def _unit_tri_inverses(mats, i, j):
    c = mats[0].shape[0]
    x = i ^ j
    eye = jnp.where(i == j, 1.0, 0.0)
    tinv = [eye - jnp.where(x < 2, a, 0.0) for a in mats]
    s = 2
    while s < c:
        level = (x >= s) & (x < 2 * s)
        tb = [t.astype(BF16) for t in tinv]
        m = [_dot(t, jnp.where(level, a, 0.0).astype(BF16)).astype(BF16) for t, a in zip(tb, mats)]
        tinv = [t - _dot(mm, b) for t, mm, b in zip(tinv, m, tb)]
        s *= 2
    return tinv


def _dn_prep(items):
    c, hd = items[0][1].shape
    i = lax.broadcasted_iota(I32, (c, c), 0)
    j = lax.broadcasted_iota(I32, (c, c), 1)
    decs, kbs, kbfs = [], [], []
    for forward, q, k, v, gam, grow, beta in items:
        incl = (i >= j) if forward else (i <= j)
        gi = jnp.concatenate([gam] * (c // LANES), axis=1)
        decs.append(jnp.where(incl, jnp.exp(gi - grow), 0.0))
        kbs.append(k * beta)
        kbfs.append(k.astype(BF16))
    kk = [_dot_nt(kb.astype(BF16), kbf) for kb, kbf in zip(kbs, kbfs)]
    mats = [jnp.where((i > j) if it[0] else (i < j), m * dec, 0.0) for it, m, dec in zip(items, kk, decs)]
    tinvs = _unit_tri_inverses(mats, i, j)
    egs = [jnp.exp(it[4]) for it in items]
    uws = [_dot(t.astype(BF16), jnp.concatenate([it[3] * it[6], kb * eg], axis=1).astype(BF16))
           for t, it, kb, eg in zip(tinvs, items, kbs, egs)]
    attns = [_dot_nt(it[1].astype(BF16), kbf) * dec for it, kbf, dec in zip(items, kbfs, decs)]
    outs = []
    for (forward, q, k, v, gam, grow, beta), uw, eg, attn in zip(items, uws, egs, attns):
        glast = gam[c - 1:c, :] if forward else gam[0:1, :]
        kd = k * jnp.exp(glast - gam)
        outs.append((uw[:, :hd], uw[:, hd:].astype(BF16), (q * eg).astype(BF16), kd.T.astype(BF16),
                     attn.astype(BF16)))
    return outs


def _dn_kernel(q_ref, k_ref, v_ref, og_ref, wq_ref, wk_ref, wv_ref, gcol_ref, grow_ref, ng_ref, o_ref,
               q_s, k_s, v_s, gam_s, beta_s, u_s, w_s, qd_s, kdt_s, attn_s, o_s):
    h = pl.program_id(1)
    nh = DN_HEADS
    hd = DN_HEAD_DIM
    q_s[...] = _l2n(_conv_silu(q_ref, wq_ref)) * (hd ** -0.5)
    k_s[...] = _l2n(_conv_silu(k_ref, wk_ref))
    v_s[...] = _conv_silu(v_ref, wv_ref)

    g_hi, g_mid, g_lo = _split3(gcol_ref[...])
    rr = lax.broadcasted_iota(I32, (LANES, 2 * LANES), 0)
    cc = lax.broadcasted_iota(I32, (LANES, 2 * LANES), 1)
    sel = lambda off: jnp.where(rr == h + off + jnp.where(cc < LANES, 0, nh), 1.0, 0.0).astype(BF16)
    sel_g = sel(0)
    gam2 = (_dot(jnp.concatenate([g_hi, g_mid], axis=1), jnp.concatenate([sel_g, sel_g], axis=0))
            + _dot(g_lo, sel_g))
    beta2 = _dot(g_hi, sel(2 * nh))
    for d in range(2):
        gam_s[d] = gam2[:, d * LANES:(d + 1) * LANES]
        beta_s[d] = beta2[:, d * LANES:(d + 1) * LANES]

    s = q_s.shape[0]
    c = DN_CHUNK
    n = s // c

    def prep(it, carry):
        where, items = [], []
        for sub in range(DN_PREP_UNROLL):
            rows = pl.ds(pl.multiple_of((it * DN_PREP_UNROLL + sub) * c, c), c)
            for d in range(2):
                where.append((d, rows))
                items.append((d == 0, q_s[rows, :], k_s[rows, :], v_s[rows, :], gam_s[d, rows, :],
                              grow_ref[0, pl.ds(d * nh + h, 1), rows], beta_s[d, rows, :]))
        for (d, rows), (u, w, qd, kdt, attn) in zip(where, _dn_prep(items)):
            u_s[d, rows, :] = u
            w_s[d, rows, :] = w
            qd_s[d, rows, :] = qd
            kdt_s[d, :, rows] = kdt
            attn_s[d, rows, :] = attn
        return carry

    lax.fori_loop(0, n // DN_PREP_UNROLL, prep, 0)

    def scan(it, states):
        r0s = [pl.multiple_of((it if d == 0 else n - 1 - it) * c, c) for d in range(2)]
        rows = [pl.ds(r0, c) for r0 in r0s]
        rs = [_dot(jnp.concatenate([w_s[d, rows[d], :], qd_s[d, rows[d], :]], axis=0), states[d].astype(BF16))
              for d in range(2)]
        vbs = [(u_s[d, rows[d], :] - rs[d][:c]).astype(BF16) for d in range(2)]
        for d in range(2):
            o_s[d, rows[d], :] = rs[d][c:] + _dot(attn_s[d, rows[d], :], vbs[d])
        new = []
        for d in range(2):
            glast = gam_s[d, pl.ds(r0s[d] + (c - 1 if d == 0 else 0), 1), :]
            new.append(states[d] * jnp.exp(glast) + _dot(kdt_s[d, :, rows[d]], vbs[d]))
        return tuple(new)

    zero = jnp.zeros((hd, hd), F32)
    lax.fori_loop(0, n, scan, (zero, zero))

    o = o_s[0] + o_s[1]
    og = og_ref[...].astype(F32)
    o_ref[...] = (_rms(o, ng_ref[...]) * (og * jax.nn.sigmoid(og))).astype(BF16)


def _deltanet(z, conv_w, gcol, grow, norm_g, nb, seq, qkv_blk, og_blk):
    t = z.shape[0]
    nh, hd, c = DN_HEADS, DN_HEAD_DIM, DN_CHUNK
    assert seq % (c * DN_PREP_UNROLL) == 0
    zspec = lambda off: pl.BlockSpec((seq, hd), lambda b, h: (b, off + h))
    wspec = lambda off: pl.BlockSpec((CONV_WIDTH, hd), lambda b, h: (0, off + h))
    return pl.pallas_call(
        _dn_kernel,
        out_shape=jax.ShapeDtypeStruct((t, nh * hd), BF16),
        grid=(nb, nh),
        in_specs=[zspec(qkv_blk), zspec(qkv_blk + nh), zspec(qkv_blk + 2 * nh), zspec(og_blk),
                  wspec(0), wspec(nh), wspec(2 * nh),
                  pl.BlockSpec((seq, LANES), lambda b, h: (b, 0)),
                  pl.BlockSpec((1, 2 * nh, seq), lambda b, h: (b, 0, 0)),
                  pl.BlockSpec((1, hd), lambda b, h: (0, 0))],
        out_specs=pl.BlockSpec((seq, hd), lambda b, h: (b, h)),
        scratch_shapes=[pltpu.VMEM((seq, hd), F32)] * 3
                       + [pltpu.VMEM((2, seq, hd), F32)] * 3
                       + [pltpu.VMEM((2, seq, hd), BF16)] * 2
                       + [pltpu.VMEM((2, hd, seq), BF16), pltpu.VMEM((2, seq, c), BF16),
                          pltpu.VMEM((2, seq, hd), F32)],
        compiler_params=_cparams("arbitrary", "arbitrary"),
        name="deltanet",
    )(z, z, z, z, conv_w, conv_w, conv_w, gcol, grow, norm_g)


def _merge_kernel(a_ref, b_ref, ga_ref, gb_ref, x_ref, gt_ref, wa_ref, wb_ref, wo_ref, g2_ref, sc_ref, sh_ref,
                  x1_ref, h2_ref):
    ya = _dot(a_ref[...], wa_ref[...])
    yb = _dot(b_ref[...], wb_ref[...])
    m = ga_ref[...].astype(F32) * ya + gb_ref[...].astype(F32) * yb
    y = _dot(m.astype(BF16), wo_ref[...])
    x1 = x_ref[...] + gt_ref[0] * y
    x1_ref[...] = x1
    h2 = _rms(x1, g2_ref[...]) * (1.0 + sc_ref[0]) + sh_ref[0]
    h2_ref[...] = _pack_halves(h2)


def _merge(a_out, b_out, z, x, gt1, wa, wb, wo, g2, sc2, sh2, seq, gate_blk, tm=512):
    t, d = x.shape
    wdt = a_out.shape[1]
    tm = min(tm, seq)
    per = seq // tm
    bvec = pl.BlockSpec((1, 1, d), lambda i: (i // per, 0, 0))
    const = lambda shp: pl.BlockSpec(shp, lambda i: (0, 0), pipeline_mode=pl.Buffered(1))
    return pl.pallas_call(
        _merge_kernel,
        out_shape=(jax.ShapeDtypeStruct((t, d), F32), jax.ShapeDtypeStruct((t, d // 2), U32)),
        grid=(t // tm,),
        in_specs=[pl.BlockSpec((tm, wdt), lambda i: (i, 0)),
                  pl.BlockSpec((tm, wdt), lambda i: (i, 0)),
                  pl.BlockSpec((tm, d), lambda i: (i, gate_blk)),
                  pl.BlockSpec((tm, d), lambda i: (i, gate_blk + 1)),
                  pl.BlockSpec((tm, d), lambda i: (i, 0)),
                  bvec, const((wdt, d)), const((wdt, d)), const((d, d)),
                  pl.BlockSpec((1, d), lambda i: (0, 0)), bvec, bvec],
        out_specs=(pl.BlockSpec((tm, d), lambda i: (i, 0)), pl.BlockSpec((tm, d // 2), lambda i: (i, 0))),
        compiler_params=_cparams("arbitrary"),
        name="merge",
    )(a_out, b_out, z, z, x, gt1, wa, wb, wo, g2, sc2, sh2)


def _prep_mixer(w, nb, seq):
    wdt = SGU_GROUPS * LANES
    nh, hd = DN_HEADS, DN_HEAD_DIM
    w_in = w["w_in"][0]
    d = w_in.shape[0]
    ab0 = 2 * wdt + 4 * nh * hd
    w_main = jnp.concatenate([w_in[:, :ab0], w_in[:, ab0 + 4 * nh:]], axis=1).astype(BF16)
    w_ab = jnp.pad(w_in[:, ab0:ab0 + 4 * nh], ((0, 0), (0, LANES - 4 * nh))).astype(BF16)
    pad_row = lambda v: jnp.pad(v.reshape(1, 2 * nh), ((0, 0), (0, LANES - 2 * nh)))
    sb = w["sgu_b"][0]
    wbr = w["w_branch"][0]
    return dict(
        nb=nb, seq=seq,
        norm_mix_g=w["norm_mix_g"][0].reshape(1, d), norm_ffn_g=w["norm_ffn_g"][0].reshape(1, d),
        w_main=w_main, w_ab=w_ab,
        sgu_ln_g=w["sgu_ln_g"][0].reshape(1, wdt), sgu_ln_b=w["sgu_ln_b"][0].reshape(1, wdt),
        sgu_ws=w["sgu_ws"][0].astype(BF16),
        sgu_sb=jnp.broadcast_to(sb[:, :, None], sb.shape + (LANES,)),
        alog_row=pad_row(w["dn_a_log"][0]), dt_row=pad_row(w["dn_dt_bias"][0]),
        dn_conv_w=w["dn_conv_w"][0], dn_norm_g=w["dn_norm_g"][0].reshape(1, hd),
        wa=wbr[:wdt].astype(BF16), wb=wbr[wdt:].astype(BF16), wo=w["w_out"][0].astype(BF16),
    )


def _token_mixer_stage(x, mods, p):
    nb, seq = p["nb"], p["seq"]
    t, d = x.shape
    sh1, sc1, gt1, sh2, sc2, _ = mods
    wdt = SGU_GROUPS * LANES
    nh = DN_HEADS
    z, ab = _inproj(x, p["norm_mix_g"], sc1, sh1, p["w_main"], p["w_ab"], seq,
                    n_gelu=2 * wdt // 1024, n_plain=(3 * nh * DN_HEAD_DIM + nh * DN_HEAD_DIM) // 1024)
    a_out = _sgu(z, p["sgu_ln_g"], p["sgu_ln_b"], p["sgu_ws"], p["sgu_sb"])
    gcol, grow = _gates(ab, p["alog_row"], p["dt_row"], nb, seq)
    qkv_blk = 2 * wdt // LANES
    b_out = _deltanet(z, p["dn_conv_w"], gcol, grow, p["dn_norm_g"], nb, seq,
                      qkv_blk=qkv_blk, og_blk=qkv_blk + 3 * nh)
    gate_blk = (2 * wdt + 4 * nh * DN_HEAD_DIM) // d
    return _merge(a_out, b_out, z, x, gt1, p["wa"], p["wb"], p["wo"], p["norm_ffn_g"], sc2, sh2, seq, gate_blk)


def _first_argmax(vals, idx, n, axis):
    mx = jnp.max(vals, axis=axis, keepdims=True)
    ix = jnp.min(jnp.where(vals == mx, idx, n), axis=axis, keepdims=True)
    return mx, ix


def _router_kernel(x_ref, g_ref, sc_ref, sh_ref, whi_ref, wlo_ref, bias_ref,
                   eidx_ref, wts_ref, rank_ref, cnt_ref, base_s):
    i = pl.program_id(0)
    ne, tm = bias_ref.shape
    neg = -jnp.inf

    @pl.when(i == 0)
    def _():
        base_s[...] = jnp.zeros_like(base_s)

    h = _rms(x_ref[...], g_ref[...]) * (1.0 + sc_ref[0]) + sh_ref[0]
    h_hi, h_mid, _ = _split3(h)
    whi = whi_ref[...]
    logits = _dot_nt(whi, h_hi) + _dot_nt(wlo_ref[...], h_hi) + _dot_nt(whi, h_mid)
    scores = jax.nn.sigmoid(logits)
    sel = scores + bias_ref[...]

    per = ne // N_GROUPS
    sel3 = sel.reshape(N_GROUPS, per, tm)
    ri = lax.broadcasted_iota(I32, sel3.shape, 1)
    m1, i1 = _first_argmax(sel3, ri, per, 1)
    m2 = jnp.max(jnp.where(ri == i1, neg, sel3), axis=1, keepdims=True)
    grp = (m1 + m2).reshape(N_GROUPS, tm)

    gi = lax.broadcasted_iota(I32, grp.shape, 0)
    chosen = jnp.zeros(grp.shape, F32)
    for _ in range(TOPK_GROUPS):
        _, ix = _first_argmax(grp, gi, N_GROUPS, 0)
        hit = gi == ix
        chosen = jnp.where(hit, 1.0, chosen)
        grp = jnp.where(hit, neg, grp)
    masked = jnp.where(chosen.reshape(N_GROUPS, 1, tm) > 0.0, sel3, neg).reshape(ne, tm)

    ei = lax.broadcasted_iota(I32, (ne, tm), 0)
    msel = jnp.zeros((ne, tm), F32)
    idx_rows, w_rows = [], []
    for _ in range(TOP_K):
        _, ix = _first_argmax(masked, ei, ne, 0)
        hit = ei == ix
        w_rows.append(jnp.sum(jnp.where(hit, scores, 0.0), axis=0, keepdims=True))
        idx_rows.append(ix)
        msel = jnp.where(hit, 1.0, msel)
        masked = jnp.where(hit, neg, masked)
    w = jnp.concatenate(w_rows, axis=0)
    eidx_ref[...] = jnp.concatenate(idx_rows, axis=0)
    wts_ref[...] = w / (jnp.sum(w, axis=0, keepdims=True) + 1e-20) * ROUTED_SCALE

    mb = msel.astype(BF16)
    a = lax.broadcasted_iota(I32, (tm, tm), 0)
    b = lax.broadcasted_iota(I32, (tm, tm), 1)
    before = _dot(mb, jnp.where(a < b, 1.0, 0.0).astype(BF16))
    pos = base_s[...] + before
    rank_ref[...] = jnp.concatenate(
        [jnp.sum(jnp.where(ei == ix, pos, 0.0), axis=0, keepdims=True) for ix in idx_rows], axis=0).astype(I32)
    base_s[...] = base_s[...] + _dot(mb, jnp.ones((tm, tm), BF16))
    cnt_ref[...] = base_s[:, :LANES].astype(I32)


def _router(x1, g2, sc2, sh2, wt_hi, wt_lo, bias, seq, tm=512):
    t, d = x1.shape
    ne = wt_hi.shape[0]
    tm = min(tm, seq)
    per = seq // tm
    bvec = pl.BlockSpec((1, 1, d), lambda i: (i // per, 0, 0))
    kt = lambda dt: jax.ShapeDtypeStruct((TOP_K, t), dt)
    kspec = pl.BlockSpec((TOP_K, tm), lambda i: (0, i))
    return pl.pallas_call(
        _router_kernel,
        out_shape=(kt(I32), kt(F32), kt(I32), jax.ShapeDtypeStruct((ne, LANES), I32)),
        grid=(t // tm,),
        in_specs=[pl.BlockSpec((tm, d), lambda i: (i, 0)),
                  pl.BlockSpec((1, d), lambda i: (0, 0)), bvec, bvec,
                  pl.BlockSpec((ne, d), lambda i: (0, 0)),
                  pl.BlockSpec((ne, d), lambda i: (0, 0)),
                  pl.BlockSpec((ne, tm), lambda i: (0, 0))],
        out_specs=(kspec, kspec, kspec, pl.BlockSpec((ne, LANES), lambda i: (0, 0))),
        scratch_shapes=[pltpu.VMEM((ne, tm), F32)],
        compiler_params=_cparams("arbitrary"),
        name="router",
    )(x1, g2, sc2, sh2, wt_hi, wt_lo, jnp.broadcast_to(bias.reshape(ne, 1), (ne, tm)))


def _row_copy_all(src_of, dst_of, sem, eidx_ref, rank_ref, pstart_ref, tm):
    def issue(t, carry):
        for k in range(TOP_K):
            row = pstart_ref[eidx_ref[k, t]] + rank_ref[k, t]
            pltpu.make_async_copy(src_of(k, t, row), dst_of(k, t, row), sem).start()
        return carry
    lax.fori_loop(0, tm, issue, 0)


def _dispatch_kernel(pstart_ref, cnt_ref, nused_ref, h_ref, eidx_ref, rank_ref, xs_ref, zbuf, sem, zsem):
    i = pl.program_id(0)
    tm = h_ref.shape[0]
    bm = zbuf.shape[0]
    ne = cnt_ref.shape[0]
    nblk = xs_ref.shape[0] // bm

    def zero_fill(act):
        def per_expert(e, carry):
            c = cnt_ref[e]
            pad = lax.rem(bm - lax.rem(c, bm), bm)
            off = pstart_ref[e] + c
            head = pad & (SUBLANES - 1)
            for r in range(SUBLANES - 1):
                @pl.when(r < head)
                def _(r=r):
                    act(pltpu.make_async_copy(zbuf.at[pl.ds(0, 1), :], xs_ref.at[pl.ds(off + r, 1), :], zsem))
            off = off + head
            s = bm // 2
            while s >= SUBLANES:
                @pl.when((pad & s) != 0)
                def _(s=s, off=off):
                    dst = xs_ref.at[pl.ds(pl.multiple_of(off, SUBLANES), s), :]
                    act(pltpu.make_async_copy(zbuf.at[pl.ds(0, s), :], dst, zsem))
                off = off + (pad & s)
                s //= 2
            return carry
        lax.fori_loop(0, ne, per_expert, 0)

        def per_block(j, carry):
            act(pltpu.make_async_copy(zbuf, xs_ref.at[pl.ds(pl.multiple_of(j * bm, bm), bm), :], zsem))
            return carry
        lax.fori_loop(nused_ref[0], nblk, per_block, 0)

    @pl.when(i == 0)
    def _():
        zbuf[...] = jnp.zeros_like(zbuf)
        zero_fill(lambda cp: cp.start())

    _row_copy_all(lambda k, t, row: h_ref.at[pl.ds(t, 1), :],
                  lambda k, t, row: xs_ref.at[pl.ds(row, 1), :],
                  sem, eidx_ref, rank_ref, pstart_ref, tm)
    for _ in range(TOP_K):
        pltpu.make_async_copy(h_ref, xs_ref.at[pl.ds(0, tm), :], sem).wait()

    @pl.when(i == 0)
    def _():
        zero_fill(lambda cp: cp.wait())


def _dispatch(h2p, eidx, rank, pstart, counts, n_used, n_rows, bm, tm=256):
    t, dw = h2p.shape
    tm = min(tm, t)
    smem = lambda: pl.BlockSpec((TOP_K, tm), lambda i, *_: (0, i), memory_space=pltpu.SMEM)
    return pl.pallas_call(
        _dispatch_kernel,
        out_shape=jax.ShapeDtypeStruct((n_rows, dw), U32),
        grid_spec=pltpu.PrefetchScalarGridSpec(
            num_scalar_prefetch=3,
            grid=(t // tm,),
            in_specs=[pl.BlockSpec((tm, dw), lambda i, *_: (i, 0)), smem(), smem()],
            out_specs=pl.BlockSpec(memory_space=pl.ANY),
            scratch_shapes=[pltpu.VMEM((bm, dw), U32), pltpu.SemaphoreType.DMA(()), pltpu.SemaphoreType.DMA(())]),
        compiler_params=_cparams("arbitrary"),
        name="dispatch",
    )(pstart, counts, n_used, h2p, eidx, rank)


def _expert_kernel(be_ref, first_ref, nxt_ref, slot_ref, nu_ref, xs_ref, wg_hbm, wu_hbm, wd_hbm, y_ref,
                   wg_buf, wu_buf, wd_buf, sems):
    j = pl.program_id(0)
    nu = nu_ref[0]

    def weight_copies(e, s):
        return [pltpu.make_async_copy(hbm.at[e], buf.at[s], sems.at[s, i])
                for i, (hbm, buf) in enumerate(((wg_hbm, wg_buf), (wu_hbm, wu_buf), (wd_hbm, wd_buf)))]

    @pl.when((j == 0) & (nu > 0))
    def _():
        for cp in weight_copies(be_ref[0], slot_ref[0]):
            cp.start()

    @pl.when((j < nu) & (first_ref[j] == 1))
    def _():
        for cp in weight_copies(be_ref[j], slot_ref[j]):
            cp.wait()

        @pl.when(nxt_ref[j] >= 0)
        def _():
            for cp in weight_copies(nxt_ref[j], 1 - slot_ref[j]):
                cp.start()

    @pl.when(j < nu)
    def _():
        s = slot_ref[j]
        lo, hi = _unpack_halves(xs_ref[...])
        lo, hi = lo.astype(BF16), hi.astype(BF16)
        n = lo.shape[1]
        wg = wg_buf[s].astype(BF16)
        wu = wu_buf[s].astype(BF16)
        g = _dot(lo, wg[:n]) + _dot(hi, wg[n:])
        u = _dot(lo, wu[:n]) + _dot(hi, wu[n:])
        a = (g * jax.nn.sigmoid(g) * u).astype(BF16)
        y_ref[...] = _pack_halves(_dot(a, wd_buf[s].astype(BF16)))

    @pl.when(j >= nu)
    def _():
        y_ref[...] = jnp.zeros_like(y_ref)


def _experts(xs, block_e, block_first, block_next, block_slot, n_used, w_gate, w_up, w_down, bm):
    n_rows, dw = xs.shape
    ne, d, f = w_gate.shape
    nblk = n_rows // bm
    return pl.pallas_call(
        _expert_kernel,
        out_shape=jax.ShapeDtypeStruct((n_rows, dw), U32),
        grid_spec=pltpu.PrefetchScalarGridSpec(
            num_scalar_prefetch=5,
            grid=(nblk,),
            in_specs=[pl.BlockSpec((bm, dw), lambda j, be, fi, nx, sl, nu: (jnp.minimum(j, nu[0] - 1), 0)),
                      pl.BlockSpec(memory_space=pl.ANY),
                      pl.BlockSpec(memory_space=pl.ANY),
                      pl.BlockSpec(memory_space=pl.ANY)],
            out_specs=pl.BlockSpec((bm, dw), lambda j, *_: (j, 0)),
            scratch_shapes=[pltpu.VMEM((2, d, f), F32), pltpu.VMEM((2, d, f), F32), pltpu.VMEM((2, f, d), F32),
                            pltpu.SemaphoreType.DMA((2, 3))]),
        compiler_params=_cparams("arbitrary"),
        name="experts",
    )(block_e, block_first, block_next, block_slot, n_used, xs, w_gate, w_up, w_down)


def _final_kernel(pstart_ref, x1_ref, h_ref, gt_ref, wts_ref, fg_ref, wg_ref, wu_ref, wd_ref,
                  eidx_ref, rank_ref, y_ref, o_ref, ybuf, sem):
    tm = x1_ref.shape[0]
    _row_copy_all(lambda k, t, row: y_ref.at[pl.ds(row, 1), :],
                  lambda k, t, row: ybuf.at[k, pl.ds(t, 1), :],
                  sem, eidx_ref, rank_ref, pstart_ref, tm)

    lo, hi = _unpack_halves(h_ref[...])
    lo, hi = lo.astype(BF16), hi.astype(BF16)
    n = lo.shape[1]
    g = _dot(lo, wg_ref[:n, :]) + _dot(hi, wg_ref[n:, :])
    u = _dot(lo, wu_ref[:n, :]) + _dot(hi, wu_ref[n:, :])
    shared = _dot((g * jax.nn.sigmoid(g) * u).astype(BF16), wd_ref[...])

    for k in range(TOP_K):
        pltpu.make_async_copy(y_ref.at[pl.ds(0, tm), :], ybuf.at[k], sem).wait()
    acc_lo, acc_hi = shared[:, :n], shared[:, n:]
    for k in range(TOP_K):
        ylo, yhi = _unpack_halves(ybuf[k])
        wk = wts_ref[:, k:k + 1]
        acc_lo = acc_lo + wk * ylo
        acc_hi = acc_hi + wk * yhi
    gt = gt_ref[0]
    x_lo = x1_ref[:, :n] + gt[:, :n] * acc_lo
    x_hi = x1_ref[:, n:] + gt[:, n:] * acc_hi
    ms = (jnp.sum(x_lo * x_lo, axis=-1, keepdims=True) + jnp.sum(x_hi * x_hi, axis=-1, keepdims=True)) / (2 * n)
    r = lax.rsqrt(ms + RMS_EPS)
    o_ref[:, :n] = x_lo * r * fg_ref[:, :n]
    o_ref[:, n:] = x_hi * r * fg_ref[:, n:]


def _final(x1, h2p, gt2, wts_t, final_g, sh_wg, sh_wu, sh_wd, eidx, rank, y, pstart, seq, tm=256):
    t, d = x1.shape
    dw = h2p.shape[1]
    f = sh_wg.shape[1]
    tm = min(tm, seq)
    per = seq // tm
    smem = lambda: pl.BlockSpec((TOP_K, tm), lambda i, ps: (0, i), memory_space=pltpu.SMEM)
    const = lambda shp: pl.BlockSpec(shp, lambda i, ps: (0, 0), pipeline_mode=pl.Buffered(1))
    return pl.pallas_call(
        _final_kernel,
        out_shape=jax.ShapeDtypeStruct((t, d), F32),
        grid_spec=pltpu.PrefetchScalarGridSpec(
            num_scalar_prefetch=1,
            grid=(t // tm,),
            in_specs=[pl.BlockSpec((tm, d), lambda i, ps: (i, 0)),
                      pl.BlockSpec((tm, dw), lambda i, ps: (i, 0)),
                      pl.BlockSpec((1, 1, d), lambda i, ps: (i // per, 0, 0)),
                      pl.BlockSpec((tm, TOP_K), lambda i, ps: (i, 0)),
                      pl.BlockSpec((1, d), lambda i, ps: (0, 0)),
                      const((d, f)), const((d, f)), const((f, d)),
                      smem(), smem(),
                      pl.BlockSpec(memory_space=pl.ANY)],
            out_specs=pl.BlockSpec((tm, d), lambda i, ps: (i, 0)),
            scratch_shapes=[pltpu.VMEM((TOP_K, tm, dw), U32), pltpu.SemaphoreType.DMA(())]),
        compiler_params=_cparams("arbitrary"),
        name="final",
    )(pstart, x1, h2p, gt2, wts_t, final_g, sh_wg, sh_wu, sh_wd, eidx, rank, y)


EXPERT_ROWS = 256


def _prep_moe(w):
    rw = w["router_w"][0]
    d = rw.shape[0]
    wt = rw.T
    wt_hi = wt.astype(BF16)
    return dict(
        wt_hi=wt_hi, wt_lo=(wt - wt_hi.astype(F32)).astype(BF16), bias=w["router_bias"][0],
        w_gate=w["exp_w_gate"][0], w_up=w["exp_w_up"][0], w_down=w["exp_w_down"][0],
        sh_wg=w["sh_w_gate"][0].astype(BF16), sh_wu=w["sh_w_up"][0].astype(BF16),
        sh_wd=w["sh_w_down"][0].astype(BF16), final_g=w["final_g"].reshape(1, d),
    )


def _moe_stage(x1, h2p, mods, pm, pw, seq):
    _, _, _, sh2, sc2, gt2 = mods
    t = x1.shape[0]
    ne = pw["wt_hi"].shape[0]
    bm = EXPERT_ROWS
    eidx, wts, rank, cnt = _router(x1, pm["norm_ffn_g"], sc2, sh2, pw["wt_hi"], pw["wt_lo"], pw["bias"], seq)
    counts = cnt[:, 0]
    pcounts = (counts + bm - 1) // bm * bm
    pends = jnp.cumsum(pcounts)
    pstart = (pends - pcounts).astype(I32)
    nblk = (t * TOP_K + ne * (bm - 1) + bm - 1) // bm
    n_used = (pends[-1:] // bm).astype(I32)
    jb = jnp.arange(nblk, dtype=I32)
    be = jnp.minimum(jnp.searchsorted(pends, jb * bm, side="right"), ne - 1).astype(I32)
    first = (jb * bm == pstart[be]).astype(I32)
    live = counts > 0
    ordinal = jnp.cumsum(live.astype(I32)) - 1
    ids = jnp.arange(ne, dtype=I32)
    after = lax.cummin(jnp.where(live, ids, ne)[::-1])[::-1]
    next_live = jnp.concatenate([after[1:], jnp.full((1,), ne, I32)])
    nxt = jnp.where(next_live[be] < ne, next_live[be], -1).astype(I32)
    slot = (ordinal[be] % 2).astype(I32)
    xs = _dispatch(h2p, eidx, rank, pstart, counts, n_used, nblk * bm, bm)
    y = _experts(xs, be, first, nxt, slot, n_used, pw["w_gate"], pw["w_up"], pw["w_down"], bm)
    return _final(x1, h2p, gt2, wts.T, pw["final_g"], pw["sh_wg"], pw["sh_wu"], pw["sh_wd"],
                  eidx, rank, y, pstart, seq)


def kernel(x_prompt, x_sample, c_prompt, c_sample, ada_w, ada_b, norm_mix_g, norm_ffn_g, w_in, sgu_ln_g,
           sgu_ln_b, sgu_ws, sgu_b, dn_conv_w, dn_a_log, dn_dt_bias, dn_norm_g, w_branch, w_out, router_w,
           router_bias, exp_w_gate, exp_w_up, exp_w_down, sh_w_gate, sh_w_up, sh_w_down, final_g):
    w = dict(ada_w=ada_w, ada_b=ada_b, norm_mix_g=norm_mix_g, norm_ffn_g=norm_ffn_g, w_in=w_in,
             sgu_ln_g=sgu_ln_g, sgu_ln_b=sgu_ln_b, sgu_ws=sgu_ws, sgu_b=sgu_b, dn_conv_w=dn_conv_w,
             dn_a_log=dn_a_log, dn_dt_bias=dn_dt_bias, dn_norm_g=dn_norm_g, w_branch=w_branch, w_out=w_out,
             router_w=router_w, router_bias=router_bias, exp_w_gate=exp_w_gate, exp_w_up=exp_w_up,
             exp_w_down=exp_w_down, sh_w_gate=sh_w_gate, sh_w_up=sh_w_up, sh_w_down=sh_w_down, final_g=final_g)
    assert w_in.shape[0] == 1, "one layer"
    bp, seq, d = x_prompt.shape
    bs = x_sample.shape[0]
    assert x_sample.shape[1] == seq
    nb = bp + bs
    x = jnp.concatenate([x_prompt.reshape(bp * seq, d), x_sample.reshape(bs * seq, d)], axis=0)
    c = jnp.concatenate([c_prompt, c_sample], axis=0)
    npad = -nb % 8
    mod = _ada(jnp.pad(c, ((0, npad), (0, 0))), ada_w[0], ada_b[0])[:nb]
    mods = [m.reshape(nb, 1, d) for m in jnp.split(mod, 6, axis=-1)]
    pm = _prep_mixer(w, nb, seq)
    pw = _prep_moe(w)
    x1, h2p = _token_mixer_stage(x, mods, pm)
    y = _moe_stage(x1, h2p, mods, pm, pw, seq)
    return (y[:bp * seq].reshape(bp, seq, d), y[bp * seq:].reshape(bs, seq, d))
```

```python
import functools

import jax
import jax.numpy as jnp
import numpy as np
from jax import lax
from jax.experimental import pallas as pl
from jax.experimental.pallas import tpu as pltpu

F32 = jnp.float32
BF16 = jnp.bfloat16
I32 = jnp.int32
U32 = jnp.uint32

RMS_EPS = 1e-6
LN_EPS = 1e-5
L2_EPS = 1e-6

SGU_GROUPS = 8
SGU_CHUNK = 128
DN_HEADS = 8
DN_HEAD_DIM = 128
DN_CHUNK = 256
DN_PREP_UNROLL = 2
CONV_WIDTH = 5
TOP_K = 8
N_GROUPS = 8
TOPK_GROUPS = 4
ROUTED_SCALE = 2.5

LANES = 128
SUBLANES = 8
VMEM_LIMIT = 56 * 1024 * 1024


def _cparams(*sem):
    return pltpu.CompilerParams(dimension_semantics=sem, vmem_limit_bytes=VMEM_LIMIT)


def _split3(x):
    hi = x.astype(BF16)
    r = x - hi.astype(F32)
    mid = r.astype(BF16)
    lo = (r - mid.astype(F32)).astype(BF16)
    return hi, mid, lo


def _dot(a, b):
    return jnp.dot(a, b, preferred_element_type=F32)


def _dot_nt(a, b):
    return lax.dot_general(a, b, (((1,), (1,)), ((), ())), preferred_element_type=F32)


def _dot_tn(a, b):
    return lax.dot_general(a, b, (((0,), (0,)), ((), ())), preferred_element_type=F32)


def _rms(x, g):
    return x * lax.rsqrt(jnp.mean(x * x, axis=-1, keepdims=True) + RMS_EPS) * g


def _pack_halves(x):
    n = x.shape[1] // 2
    return pltpu.pack_elementwise([x[:, :n], x[:, n:]], packed_dtype=BF16)


def _unpack_halves(w):
    lo = pltpu.unpack_elementwise(w, index=0, packed_dtype=BF16, unpacked_dtype=F32)
    hi = pltpu.unpack_elementwise(w, index=1, packed_dtype=BF16, unpacked_dtype=F32)
    return lo, hi


def _store_token_tiles(ref, x):
    m = x.shape[0]
    for s in range(SUBLANES):
        ref[pl.ds(s, m, stride=SUBLANES), :] = x[:, s * LANES:(s + 1) * LANES]


def _load_token_tiles(ref):
    m = ref.shape[0] // SUBLANES
    return jnp.concatenate([ref[pl.ds(s, m, stride=SUBLANES), :] for s in range(SUBLANES)], axis=1)


def _ada_kernel(c_ref, w_ref, b_ref, o_ref):
    c = c_ref[...]
    a = c * jax.nn.sigmoid(c)
    a_hi, a_mid, _ = _split3(a)
    w = w_ref[...]
    w_hi = w.astype(BF16)
    w_lo = (w - w_hi.astype(F32)).astype(BF16)
    o_ref[...] = _dot(a_hi, w_hi) + _dot(a_mid, w_hi) + _dot(a_hi, w_lo) + b_ref[...]


def _ada(c, ada_w, ada_b, tn=1024):
    nb, d = c.shape
    n = ada_w.shape[1]
    return pl.pallas_call(
        _ada_kernel,
        out_shape=jax.ShapeDtypeStruct((nb, n), F32),
        grid=(n // tn,),
        in_specs=[pl.BlockSpec((nb, d), lambda j: (0, 0)),
                  pl.BlockSpec((d, tn), lambda j: (0, j)),
                  pl.BlockSpec((1, tn), lambda j: (0, j))],
        out_specs=pl.BlockSpec((nb, tn), lambda j: (0, j)),
        compiler_params=_cparams("arbitrary"),
        name="ada",
    )(c, ada_w, ada_b.reshape(1, n))


GELU_C = float(np.sqrt(2.0 / np.pi))


def _two_trunk_specs(block, n_first):
    first = pl.BlockSpec(block, lambda i, *_: (jnp.minimum(i, n_first - 1), 0))
    second = pl.BlockSpec(block, lambda i, *_: (jnp.maximum(i - n_first, 0), 0))
    return first, second


def _inproj_kernel(n_first, n_gelu, n_plain, xa_ref, xb_ref, g_ref, sc_ref, sh_ref, w_ref, wab_ref, z_ref, ab_ref,
                   h_s):
    i = pl.program_id(0)
    j = pl.program_id(1)

    def prologue(x_ref):
        h = _rms(x_ref[...], g_ref[...]) * (1.0 + sc_ref[0]) + sh_ref[0]
        h_s[...] = h.astype(BF16)
        ab_ref[...] = _dot(h_s[...], wab_ref[...])

    pl.when((j == 0) & (i < n_first))(lambda: prologue(xa_ref))
    pl.when((j == 0) & (i >= n_first))(lambda: prologue(xb_ref))

    is_gelu = j < n_gelu
    is_plain = (j >= n_gelu) & (j < n_gelu + n_plain)
    cw = 2 * LANES
    nc = w_ref.shape[1] // cw
    cols = [slice(c * cw, (c + 1) * cw) for c in range(nc)]
    nxt = _dot(h_s[...], w_ref[:, cols[0]])
    for c in range(nc):
        acc = nxt
        if c + 1 < nc:
            nxt = _dot(h_s[...], w_ref[:, cols[c + 1]])
        t = jnp.tanh(jnp.where(is_gelu, GELU_C * (acc + 0.044715 * (acc * acc * acc)), 0.5 * acc))
        act = 0.5 * (1.0 + t) * jnp.where(is_gelu, acc, 1.0)
        z_ref[:, cols[c]] = jnp.where(is_plain, acc, act).astype(BF16)


INPROJ_TN = 2048


def _inproj(xa, xb, g, sc, sh, w_main, w_ab, seq, n_gelu, n_plain, tm=512, tn=INPROJ_TN):
    d = xa.shape[1]
    t = xa.shape[0] + xb.shape[0]
    n = w_main.shape[1]
    tm = min(tm, seq)
    per = seq // tm
    n_first = xa.shape[0] // tm
    spec_a, spec_b = _two_trunk_specs((tm, d), n_first)
    return pl.pallas_call(
        functools.partial(_inproj_kernel, n_first, n_gelu, n_plain),
        out_shape=(jax.ShapeDtypeStruct((t, n), BF16), jax.ShapeDtypeStruct((t, LANES), F32)),
        grid=(t // tm, n // tn),
        in_specs=[spec_a, spec_b,
                  pl.BlockSpec((1, d), lambda i, j: (0, 0)),
                  pl.BlockSpec((1, 1, d), lambda i, j: (i // per, 0, 0)),
                  pl.BlockSpec((1, 1, d), lambda i, j: (i // per, 0, 0)),
                  pl.BlockSpec((d, tn), lambda i, j: (0, j)),
                  pl.BlockSpec((d, LANES), lambda i, j: (0, 0))],
        out_specs=(pl.BlockSpec((tm, tn), lambda i, j: (i, j)),
                   pl.BlockSpec((tm, LANES), lambda i, j: (i, 0))),
        scratch_shapes=[pltpu.VMEM((tm, d), BF16)],
        compiler_params=_cparams("arbitrary", "arbitrary"),
        name="inproj",
    )(xa, xb, g, sc, sh, w_main, w_ab)


def _sgu_kernel(u_ref, v_ref, g_ref, b_ref, ws_ref, sb_ref, o_ref):
    v = v_ref[...].astype(F32)
    mu = jnp.mean(v, axis=-1, keepdims=True)
    vc = v - mu
    var = jnp.mean(vc * vc, axis=-1, keepdims=True)
    vn = (vc * lax.rsqrt(var + LN_EPS) * g_ref[...] + b_ref[...]).astype(BF16)
    tm = vn.shape[0]
    for c in range(tm // SGU_CHUNK):
        r = slice(c * SGU_CHUNK, (c + 1) * SGU_CHUNK)
        for gi in range(SGU_GROUPS):
            l = slice(gi * LANES, (gi + 1) * LANES)
            mixed = _dot(ws_ref[gi], vn[r, l]) + sb_ref[gi]
            o_ref[r, l] = (u_ref[r, l].astype(F32) * mixed).astype(BF16)


def _sgu(z, ln_g, ln_b, ws, sb_b, tm=512):
    t = z.shape[0]
    w = SGU_GROUPS * LANES
    tm = np.gcd(tm, t)
    return pl.pallas_call(
        _sgu_kernel,
        out_shape=jax.ShapeDtypeStruct((t, w), BF16),
        grid=(t // tm,),
        in_specs=[pl.BlockSpec((tm, w), lambda i: (i, 0)),
                  pl.BlockSpec((tm, w), lambda i: (i, 1)),
                  pl.BlockSpec((1, w), lambda i: (0, 0)),
                  pl.BlockSpec((1, w), lambda i: (0, 0)),
                  pl.BlockSpec((SGU_GROUPS, SGU_CHUNK, SGU_CHUNK), lambda i: (0, 0, 0)),
                  pl.BlockSpec((SGU_GROUPS, SGU_CHUNK, LANES), lambda i: (0, 0, 0))],
        out_specs=pl.BlockSpec((tm, w), lambda i: (i, 0)),
        compiler_params=_cparams("arbitrary"),
        name="sgu",
    )(z, z, ln_g, ln_b, ws, sb_b)


def _gate_kernel(ab_ref, alog_ref, dt_ref, gcol_ref, grow_ref):
    ab = ab_ref[...]
    tr = ab.shape[0]
    lane = lax.broadcasted_iota(I32, ab.shape, 1)
    z = ab + dt_ref[...]
    softplus = jnp.maximum(z, 0.0) + jnp.log1p(jnp.exp(-jnp.abs(z)))
    g = -jnp.exp(alog_ref[...]) * softplus
    beta = jax.nn.sigmoid(ab)
    nh = DN_HEADS
    g = jnp.where(lane < 2 * nh, g, 0.0)
    i = lax.broadcasted_iota(I32, (tr, tr), 0)
    j = lax.broadcasted_iota(I32, (tr, tr), 1)
    same = (i // DN_CHUNK) == (j // DN_CHUNK)
    lower = jnp.where(same & (j <= i), 1.0, 0.0).astype(BF16)
    upper = jnp.where(same & (j >= i), 1.0, 0.0).astype(BF16)
    g_hi, g_mid, g_lo = _split3(g)
    pre = _dot(lower, g_hi) + _dot(lower, g_mid) + _dot(lower, g_lo)
    suf = _dot(upper, g_hi) + _dot(upper, g_mid) + _dot(upper, g_lo)
    out = jnp.where(lane < nh, pre, jnp.where(lane < 2 * nh, suf, jnp.where(lane < 4 * nh, beta, 0.0)))
    gcol_ref[...] = out
    grow_ref[0] = out.T[:2 * nh, :]


def _gates(ab, alog_row, dt_row, nb, seq, tr=256):
    t = ab.shape[0]
    per = seq // tr
    return pl.pallas_call(
        _gate_kernel,
        out_shape=(jax.ShapeDtypeStruct((t, LANES), F32),
                   jax.ShapeDtypeStruct((nb, 2 * DN_HEADS, seq), F32)),
        grid=(t // tr,),
        in_specs=[pl.BlockSpec((tr, LANES), lambda i: (i, 0)),
                  pl.BlockSpec((1, LANES), lambda i: (0, 0)),
                  pl.BlockSpec((1, LANES), lambda i: (0, 0))],
        out_specs=(pl.BlockSpec((tr, LANES), lambda i: (i, 0)),
                   pl.BlockSpec((1, 2 * DN_HEADS, tr), lambda i: (i // per, 0, i % per))),
        compiler_params=_cparams("arbitrary"),
        name="gates",
    )(ab, alog_row, dt_row)


def _conv_silu(x_ref, w_ref):
    x = x_ref[...].astype(F32)
    s = x.shape[0]
    row = lax.broadcasted_iota(I32, x.shape, 0)
    pad = (CONV_WIDTH - 1) // 2
    acc = x * w_ref[pad:pad + 1, :]
    for j in range(CONV_WIDTH):
        d = j - pad
        if d == 0:
            continue
        xs = pltpu.roll(x, (-d) % s, 0)
        ok = (row + d >= 0) & (row + d < s)
        acc = acc + jnp.where(ok, xs, 0.0) * w_ref[j:j + 1, :]
    return acc * jax.nn.sigmoid(acc)


def _l2n(x):
    return x * lax.rsqrt(jnp.sum(x * x, axis=-1, keepdims=True) + L2_EPS)


def _unit_tri_inverses(mats, i, j):
    c = mats[0].shape[0]
    x = i ^ j
    eye = jnp.where(i == j, 1.0, 0.0)
    tinv = [eye - jnp.where(x < 2, a, 0.0) for a in mats]
    s = 2
    while s < c:
        level = (x >= s) & (x < 2 * s)
        tb = [t.astype(BF16) for t in tinv]
        m = [_dot(t, jnp.where(level, a, 0.0).astype(BF16)).astype(BF16) for t, a in zip(tb, mats)]
        tinv = [t - _dot(mm, b) for t, mm, b in zip(tinv, m, tb)]
        s *= 2
    return tinv


def _dn_prep(items):
    c, hd = items[0][1].shape
    i = lax.broadcasted_iota(I32, (c, c), 0)
    j = lax.broadcasted_iota(I32, (c, c), 1)
    decs, kbs, kbfs = [], [], []
    for forward, q, k, v, gam, grow, beta in items:
        incl = (i >= j) if forward else (i <= j)
        gi = jnp.concatenate([gam] * (c // LANES), axis=1)
        decs.append(jnp.where(incl, jnp.exp(gi - grow), 0.0))
        kbs.append(k * beta)
        kbfs.append(k.astype(BF16))
    kk = [_dot_nt(kb.astype(BF16), kbf) for kb, kbf in zip(kbs, kbfs)]
    mats = [jnp.where((i > j) if it[0] else (i < j), m * dec, 0.0) for it, m, dec in zip(items, kk, decs)]
    tinvs = _unit_tri_inverses(mats, i, j)
    egs = [jnp.exp(it[4]) for it in items]
    uws = [_dot(t.astype(BF16), jnp.concatenate([it[3] * it[6], kb * eg], axis=1).astype(BF16))
           for t, it, kb, eg in zip(tinvs, items, kbs, egs)]
    attns = [_dot_nt(it[1].astype(BF16), kbf) * dec for it, kbf, dec in zip(items, kbfs, decs)]
    outs = []
    for (forward, q, k, v, gam, grow, beta), uw, eg, attn in zip(items, uws, egs, attns):
        glast = gam[c - 1:c, :] if forward else gam[0:1, :]
        kd = k * jnp.exp(glast - gam)
        outs.append((uw[:, :hd], uw[:, hd:].astype(BF16), (q * eg).astype(BF16), kd.T.astype(BF16),
                     attn.astype(BF16)))
    return outs


def _dn_kernel(q_ref, k_ref, v_ref, og_ref, wq_ref, wk_ref, wv_ref, gcol_ref, grow_ref, ng_ref, o_ref,
               q_s, k_s, v_s, gam_s, beta_s, u_s, w_s, qd_s, kdt_s, attn_s, o_s):
    h = pl.program_id(1)
    nh = DN_HEADS
    hd = DN_HEAD_DIM
    q_s[...] = _l2n(_conv_silu(q_ref, wq_ref)) * (hd ** -0.5)
    k_s[...] = _l2n(_conv_silu(k_ref, wk_ref))
    v_s[...] = _conv_silu(v_ref, wv_ref)

    g_hi, g_mid, g_lo = _split3(gcol_ref[...])
    rr = lax.broadcasted_iota(I32, (LANES, 2 * LANES), 0)
    cc = lax.broadcasted_iota(I32, (LANES, 2 * LANES), 1)
    sel = lambda off: jnp.where(rr == h + off + jnp.where(cc < LANES, 0, nh), 1.0, 0.0).astype(BF16)
    sel_g = sel(0)
    gam2 = (_dot(jnp.concatenate([g_hi, g_mid], axis=1), jnp.concatenate([sel_g, sel_g], axis=0))
            + _dot(g_lo, sel_g))
    beta2 = _dot(g_hi, sel(2 * nh))
    for d in range(2):
        gam_s[d] = gam2[:, d * LANES:(d + 1) * LANES]
        beta_s[d] = beta2[:, d * LANES:(d + 1) * LANES]

    s = q_s.shape[0]
    c = DN_CHUNK
    n = s // c

    def prep(it, carry):
        where, items = [], []
        for sub in range(DN_PREP_UNROLL):
            rows = pl.ds(pl.multiple_of((it * DN_PREP_UNROLL + sub) * c, c), c)
            for d in range(2):
                where.append((d, rows))
                items.append((d == 0, q_s[rows, :], k_s[rows, :], v_s[rows, :], gam_s[d, rows, :],
                              grow_ref[0, pl.ds(d * nh + h, 1), rows], beta_s[d, rows, :]))
        for (d, rows), (u, w, qd, kdt, attn) in zip(where, _dn_prep(items)):
            u_s[d, rows, :] = u
            w_s[d, rows, :] = w
            qd_s[d, rows, :] = qd
            kdt_s[d, :, rows] = kdt
            attn_s[d, rows, :] = attn
        return carry

    lax.fori_loop(0, n // DN_PREP_UNROLL, prep, 0)

    def scan(it, states):
        r0s = [pl.multiple_of((it if d == 0 else n - 1 - it) * c, c) for d in range(2)]
        rows = [pl.ds(r0, c) for r0 in r0s]
        rs = [_dot(jnp.concatenate([w_s[d, rows[d], :], qd_s[d, rows[d], :]], axis=0), states[d].astype(BF16))
              for d in range(2)]
        vbs = [(u_s[d, rows[d], :] - rs[d][:c]).astype(BF16) for d in range(2)]
        for d in range(2):
            o_s[d, rows[d], :] = rs[d][c:] + _dot(attn_s[d, rows[d], :], vbs[d])
        new = []
        for d in range(2):
            glast = gam_s[d, pl.ds(r0s[d] + (c - 1 if d == 0 else 0), 1), :]
            new.append(states[d] * jnp.exp(glast) + _dot(kdt_s[d, :, rows[d]], vbs[d]))
        return tuple(new)

    zero = jnp.zeros((hd, hd), F32)
    lax.fori_loop(0, n, scan, (zero, zero))

    o = o_s[0] + o_s[1]
    og = og_ref[...].astype(F32)
    o_ref[...] = (_rms(o, ng_ref[...]) * (og * jax.nn.sigmoid(og))).astype(BF16)


def _deltanet(z, conv_w, gcol, grow, norm_g, nb, seq, qkv_blk, og_blk):
    t = z.shape[0]
    nh, hd, c = DN_HEADS, DN_HEAD_DIM, DN_CHUNK
    assert seq % (c * DN_PREP_UNROLL) == 0
    zspec = lambda off: pl.BlockSpec((seq, hd), lambda b, h: (b, off + h))
    wspec = lambda off: pl.BlockSpec((CONV_WIDTH, hd), lambda b, h: (0, off + h))
    return pl.pallas_call(
        _dn_kernel,
        out_shape=jax.ShapeDtypeStruct((t, nh * hd), BF16),
        grid=(nb, nh),
        in_specs=[zspec(qkv_blk), zspec(qkv_blk + nh), zspec(qkv_blk + 2 * nh), zspec(og_blk),
                  wspec(0), wspec(nh), wspec(2 * nh),
                  pl.BlockSpec((seq, LANES), lambda b, h: (b, 0)),
                  pl.BlockSpec((1, 2 * nh, seq), lambda b, h: (b, 0, 0)),
                  pl.BlockSpec((1, hd), lambda b, h: (0, 0))],
        out_specs=pl.BlockSpec((seq, hd), lambda b, h: (b, h)),
        scratch_shapes=[pltpu.VMEM((seq, hd), F32)] * 3
                       + [pltpu.VMEM((2, seq, hd), F32)] * 3
                       + [pltpu.VMEM((2, seq, hd), BF16)] * 2
                       + [pltpu.VMEM((2, hd, seq), BF16), pltpu.VMEM((2, seq, c), BF16),
                          pltpu.VMEM((2, seq, hd), F32)],
        compiler_params=_cparams("arbitrary", "arbitrary"),
        name="deltanet",
    )(z, z, z, z, conv_w, conv_w, conv_w, gcol, grow, norm_g)


def _merge_kernel(n_first, a_ref, b_ref, ga_ref, gb_ref, xa_ref, xb_ref, gt_ref, wa_ref, wb_ref, wo_ref, g2_ref,
                  sc_ref, sh_ref, x1_ref, h2_ref):
    i = pl.program_id(0)
    ya = _dot(a_ref[...], wa_ref[...])
    yb = _dot(b_ref[...], wb_ref[...])
    m = ga_ref[...].astype(F32) * ya + gb_ref[...].astype(F32) * yb
    y = _dot(m.astype(BF16), wo_ref[...])

    def epilogue(x_ref):
        x1 = x_ref[...] + gt_ref[0] * y
        x1_ref[...] = x1
        h2 = _rms(x1, g2_ref[...]) * (1.0 + sc_ref[0]) + sh_ref[0]
        _store_token_tiles(h2_ref, _pack_halves(h2))

    pl.when(i < n_first)(lambda: epilogue(xa_ref))
    pl.when(i >= n_first)(lambda: epilogue(xb_ref))


def _merge(a_out, b_out, z, xa, xb, gt1, wa, wb, wo, g2, sc2, sh2, seq, gate_blk, tm=256):
    d = xa.shape[1]
    t = xa.shape[0] + xb.shape[0]
    wdt = a_out.shape[1]
    tm = min(tm, seq)
    per = seq // tm
    n_first = xa.shape[0] // tm
    spec_a, spec_b = _two_trunk_specs((tm, d), n_first)
    bvec = pl.BlockSpec((1, 1, d), lambda i: (i // per, 0, 0))
    const = lambda shp: pl.BlockSpec(shp, lambda i: (0, 0), pipeline_mode=pl.Buffered(1))
    return pl.pallas_call(
        functools.partial(_merge_kernel, n_first),
        out_shape=(jax.ShapeDtypeStruct((t, d), F32), jax.ShapeDtypeStruct((t * SUBLANES, LANES), U32)),
        grid=(t // tm,),
        in_specs=[pl.BlockSpec((tm, wdt), lambda i: (i, 0)),
                  pl.BlockSpec((tm, wdt), lambda i: (i, 0)),
                  pl.BlockSpec((tm, d), lambda i: (i, gate_blk)),
                  pl.BlockSpec((tm, d), lambda i: (i, gate_blk + 1)),
                  spec_a, spec_b,
                  bvec, const((wdt, d)), const((wdt, d)), const((d, d)),
                  pl.BlockSpec((1, d), lambda i: (0, 0)), bvec, bvec],
        out_specs=(pl.BlockSpec((tm, d), lambda i: (i, 0)),
                   pl.BlockSpec((tm * SUBLANES, LANES), lambda i: (i, 0))),
        compiler_params=_cparams("arbitrary"),
        name="merge",
    )(a_out, b_out, z, z, xa, xb, gt1, wa, wb, wo, g2, sc2, sh2)


def _prep_mixer(w, nb, seq):
    wdt = SGU_GROUPS * LANES
    nh, hd = DN_HEADS, DN_HEAD_DIM
    w_in = w["w_in"][0]
    d = w_in.shape[0]
    ab0 = 2 * wdt + 4 * nh * hd
    w_main = jnp.concatenate([w_in[:, :ab0], w_in[:, ab0 + 4 * nh:]], axis=1).astype(BF16)
    w_ab = jnp.pad(w_in[:, ab0:ab0 + 4 * nh], ((0, 0), (0, LANES - 4 * nh))).astype(BF16)
    pad_row = lambda v: jnp.pad(v.reshape(1, 2 * nh), ((0, 0), (0, LANES - 2 * nh)))
    sb = w["sgu_b"][0]
    wbr = w["w_branch"][0]
    return dict(
        nb=nb, seq=seq,
        norm_mix_g=w["norm_mix_g"][0].reshape(1, d), norm_ffn_g=w["norm_ffn_g"][0].reshape(1, d),
        w_main=w_main, w_ab=w_ab,
        sgu_ln_g=w["sgu_ln_g"][0].reshape(1, wdt), sgu_ln_b=w["sgu_ln_b"][0].reshape(1, wdt),
        sgu_ws=w["sgu_ws"][0].astype(BF16),
        sgu_sb=jnp.broadcast_to(sb[:, :, None], sb.shape + (LANES,)),
        alog_row=pad_row(w["dn_a_log"][0]), dt_row=pad_row(w["dn_dt_bias"][0]),
        dn_conv_w=w["dn_conv_w"][0], dn_norm_g=w["dn_norm_g"][0].reshape(1, hd),
        wa=wbr[:wdt].astype(BF16), wb=wbr[wdt:].astype(BF16), wo=w["w_out"][0].astype(BF16),
    )


def _token_mixer_stage(xa, xb, mods, p):
    nb, seq = p["nb"], p["seq"]
    d = xa.shape[1]
    sh1, sc1, gt1, sh2, sc2, _ = mods
    wdt = SGU_GROUPS * LANES
    nh = DN_HEADS
    z, ab = _inproj(xa, xb, p["norm_mix_g"], sc1, sh1, p["w_main"], p["w_ab"], seq,
                    n_gelu=2 * wdt // INPROJ_TN, n_plain=4 * nh * DN_HEAD_DIM // INPROJ_TN)
    a_out = _sgu(z, p["sgu_ln_g"], p["sgu_ln_b"], p["sgu_ws"], p["sgu_sb"])
    gcol, grow = _gates(ab, p["alog_row"], p["dt_row"], nb, seq)
    qkv_blk = 2 * wdt // LANES
    b_out = _deltanet(z, p["dn_conv_w"], gcol, grow, p["dn_norm_g"], nb, seq,
                      qkv_blk=qkv_blk, og_blk=qkv_blk + 3 * nh)
    gate_blk = (2 * wdt + 4 * nh * DN_HEAD_DIM) // d
    return _merge(a_out, b_out, z, xa, xb, gt1, p["wa"], p["wb"], p["wo"], p["norm_ffn_g"], sc2, sh2, seq,
                  gate_blk)


def _first_argmax(vals, idx, n, axis):
    mx = jnp.max(vals, axis=axis, keepdims=True)
    ix = jnp.min(jnp.where(vals == mx, idx, n), axis=axis, keepdims=True)
    return mx, ix


def _router_kernel(x_ref, g_ref, sc_ref, sh_ref, whi_ref, wlo_ref, bias_ref,
                   eidx_ref, wts_ref, rank_ref, cnt_ref, base_s):
    i = pl.program_id(0)
    ne, tm = bias_ref.shape
    neg = -jnp.inf

    @pl.when(i == 0)
    def _():
        base_s[...] = jnp.zeros_like(base_s)

    h = _rms(x_ref[...], g_ref[...]) * (1.0 + sc_ref[0]) + sh_ref[0]
    h_hi, h_mid, _ = _split3(h)
    whi = whi_ref[...]
    logits = _dot_nt(whi, h_hi) + _dot_nt(wlo_ref[...], h_hi) + _dot_nt(whi, h_mid)
    scores = jax.nn.sigmoid(logits)
    sel = scores + bias_ref[...]

    per = ne // N_GROUPS
    sel3 = sel.reshape(N_GROUPS, per, tm)
    ri = lax.broadcasted_iota(I32, sel3.shape, 1)
    m1, i1 = _first_argmax(sel3, ri, per, 1)
    m2 = jnp.max(jnp.where(ri == i1, neg, sel3), axis=1, keepdims=True)
    grp = (m1 + m2).reshape(N_GROUPS, tm)

    gi = lax.broadcasted_iota(I32, grp.shape, 0)
    chosen = jnp.zeros(grp.shape, F32)
    for _ in range(TOPK_GROUPS):
        _, ix = _first_argmax(grp, gi, N_GROUPS, 0)
        hit = gi == ix
        chosen = jnp.where(hit, 1.0, chosen)
        grp = jnp.where(hit, neg, grp)
    masked = jnp.where(chosen.reshape(N_GROUPS, 1, tm) > 0.0, sel3, neg).reshape(ne, tm)

    ei = lax.broadcasted_iota(I32, (ne, tm), 0)
    msel = jnp.zeros((ne, tm), F32)
    idx_rows, w_rows = [], []
    for _ in range(TOP_K):
        _, ix = _first_argmax(masked, ei, ne, 0)
        hit = ei == ix
        w_rows.append(jnp.sum(jnp.where(hit, scores, 0.0), axis=0, keepdims=True))
        idx_rows.append(ix)
        msel = jnp.where(hit, 1.0, msel)
        masked = jnp.where(hit, neg, masked)
    w = jnp.concatenate(w_rows, axis=0)
    eidx_ref[...] = jnp.concatenate(idx_rows, axis=0)
    wts_ref[...] = w / (jnp.sum(w, axis=0, keepdims=True) + 1e-20) * ROUTED_SCALE

    mb = msel.astype(BF16)
    a = lax.broadcasted_iota(I32, (tm, tm), 0)
    b = lax.broadcasted_iota(I32, (tm, tm), 1)
    before = _dot(mb, jnp.where(a < b, 1.0, 0.0).astype(BF16))
    pos = base_s[...] + before
    rank_ref[...] = jnp.concatenate(
        [jnp.sum(jnp.where(ei == ix, pos, 0.0), axis=0, keepdims=True) for ix in idx_rows], axis=0).astype(I32)
    base_s[...] = base_s[...] + _dot(mb, jnp.ones((tm, tm), BF16))
    cnt_ref[...] = base_s[:, :LANES].astype(I32)


def _router(x1, g2, sc2, sh2, wt_hi, wt_lo, bias, seq, tm=512):
    t, d = x1.shape
    ne = wt_hi.shape[0]
    tm = min(tm, seq)
    per = seq // tm
    bvec = pl.BlockSpec((1, 1, d), lambda i: (i // per, 0, 0))
    kt = lambda dt: jax.ShapeDtypeStruct((TOP_K, t), dt)
    kspec = pl.BlockSpec((TOP_K, tm), lambda i: (0, i))
    return pl.pallas_call(
        _router_kernel,
        out_shape=(kt(I32), kt(F32), kt(I32), jax.ShapeDtypeStruct((ne, LANES), I32)),
        grid=(t // tm,),
        in_specs=[pl.BlockSpec((tm, d), lambda i: (i, 0)),
                  pl.BlockSpec((1, d), lambda i: (0, 0)), bvec, bvec,
                  pl.BlockSpec((ne, d), lambda i: (0, 0)),
                  pl.BlockSpec((ne, d), lambda i: (0, 0)),
                  pl.BlockSpec((ne, tm), lambda i: (0, 0))],
        out_specs=(kspec, kspec, kspec, pl.BlockSpec((ne, LANES), lambda i: (0, 0))),
        scratch_shapes=[pltpu.VMEM((ne, tm), F32)],
        compiler_params=_cparams("arbitrary"),
        name="router",
    )(x1, g2, sc2, sh2, wt_hi, wt_lo, jnp.broadcast_to(bias.reshape(ne, 1), (ne, tm)))


def _tile(ref, r, n=1):
    return ref.at[pl.ds(pl.multiple_of(r * SUBLANES, SUBLANES), n * SUBLANES), :]


def _token_copy_all(src_of, dst_of, sem, rows_ref, tm):
    def issue(t, carry):
        for k in range(TOP_K):
            row = rows_ref[k, t]
            pltpu.make_async_copy(src_of(k, t, row), dst_of(k, t, row), sem).start()
        return carry
    lax.fori_loop(0, tm, issue, 0)


def _dispatch_kernel(pstart_ref, cnt_ref, nused_ref, h_ref, rows_ref, xs_ref, zbuf, sem, zsem):
    i = pl.program_id(0)
    tm = h_ref.shape[0] // SUBLANES
    bm = zbuf.shape[0] // SUBLANES
    ne = cnt_ref.shape[0]
    nblk = xs_ref.shape[0] // (bm * SUBLANES)

    def zero_fill(act):
        def per_expert(e, carry):
            c = cnt_ref[e]
            pad = lax.rem(bm - lax.rem(c, bm), bm)
            off = pstart_ref[e] + c
            s = bm // 2
            while s >= 1:
                @pl.when((pad & s) != 0)
                def _(s=s, off=off):
                    act(pltpu.make_async_copy(_tile(zbuf, 0, s), _tile(xs_ref, off, s), zsem))
                off = off + (pad & s)
                s //= 2
            return carry
        lax.fori_loop(0, ne, per_expert, 0)

        def per_block(j, carry):
            act(pltpu.make_async_copy(zbuf, _tile(xs_ref, j * bm, bm), zsem))
            return carry
        lax.fori_loop(nused_ref[0], nblk, per_block, 0)

    @pl.when(i == 0)
    def _():
        zbuf[...] = jnp.zeros_like(zbuf)
        zero_fill(lambda cp: cp.start())

    _token_copy_all(lambda k, t, row: _tile(h_ref, t), lambda k, t, row: _tile(xs_ref, row), sem, rows_ref, tm)
    for _ in range(TOP_K):
        pltpu.make_async_copy(h_ref, _tile(xs_ref, 0, tm), sem).wait()

    @pl.when(i == 0)
    def _():
        zero_fill(lambda cp: cp.wait())


def _dispatch(h2p, rows, pstart, counts, n_used, n_rows, bm, tm=256):
    t = h2p.shape[0] // SUBLANES
    tm = min(tm, t)
    return pl.pallas_call(
        _dispatch_kernel,
        out_shape=jax.ShapeDtypeStruct((n_rows * SUBLANES, LANES), U32),
        grid_spec=pltpu.PrefetchScalarGridSpec(
            num_scalar_prefetch=3,
            grid=(t // tm,),
            in_specs=[pl.BlockSpec((tm * SUBLANES, LANES), lambda i, *_: (i, 0)),
                      pl.BlockSpec((TOP_K, tm), lambda i, *_: (0, i), memory_space=pltpu.SMEM)],
            out_specs=pl.BlockSpec(memory_space=pl.ANY),
            scratch_shapes=[pltpu.VMEM((bm * SUBLANES, LANES), U32), pltpu.SemaphoreType.DMA(()),
                            pltpu.SemaphoreType.DMA(())]),
        compiler_params=_cparams("arbitrary"),
        name="dispatch",
    )(pstart, counts, n_used, h2p, rows)


def _expert_kernel(be_ref, first_ref, nxt_ref, slot_ref, nu_ref, xs_ref, wg_hbm, wu_hbm, wd_hbm, y_ref,
                   wg_buf, wu_buf, wd_buf, sems):
    j = pl.program_id(0)
    nu = nu_ref[0]

    def weight_copies(e, s):
        return [pltpu.make_async_copy(hbm.at[e], buf.at[s], sems.at[s, i])
                for i, (hbm, buf) in enumerate(((wg_hbm, wg_buf), (wu_hbm, wu_buf), (wd_hbm, wd_buf)))]

    @pl.when((j == 0) & (nu > 0))
    def _():
        for cp in weight_copies(be_ref[0], slot_ref[0]):
            cp.start()

    @pl.when((j < nu) & (first_ref[j] == 1))
    def _():
        for cp in weight_copies(be_ref[j], slot_ref[j]):
            cp.wait()

        @pl.when(nxt_ref[j] >= 0)
        def _():
            for cp in weight_copies(nxt_ref[j], 1 - slot_ref[j]):
                cp.start()

    @pl.when(j < nu)
    def _():
        s = slot_ref[j]
        lo, hi = _unpack_halves(_load_token_tiles(xs_ref))
        lo, hi = lo.astype(BF16), hi.astype(BF16)
        n = lo.shape[1]
        wg = wg_buf[s].astype(BF16)
        wu = wu_buf[s].astype(BF16)
        g = _dot(lo, wg[:n]) + _dot(hi, wg[n:])
        u = _dot(lo, wu[:n]) + _dot(hi, wu[n:])
        a = (g * jax.nn.sigmoid(g) * u).astype(BF16)
        _store_token_tiles(y_ref, _pack_halves(_dot(a, wd_buf[s].astype(BF16))))

    @pl.when(j >= nu)
    def _():
        y_ref[...] = jnp.zeros_like(y_ref)


def _experts(xs, block_e, block_first, block_next, block_slot, n_used, w_gate, w_up, w_down, bm):
    n_rows = xs.shape[0] // SUBLANES
    ne, d, f = w_gate.shape
    nblk = n_rows // bm
    blk = (bm * SUBLANES, LANES)
    return pl.pallas_call(
        _expert_kernel,
        out_shape=jax.ShapeDtypeStruct(xs.shape, U32),
        grid_spec=pltpu.PrefetchScalarGridSpec(
            num_scalar_prefetch=5,
            grid=(nblk,),
            in_specs=[pl.BlockSpec(blk, lambda j, be, fi, nx, sl, nu: (jnp.minimum(j, nu[0] - 1), 0)),
                      pl.BlockSpec(memory_space=pl.ANY),
                      pl.BlockSpec(memory_space=pl.ANY),
                      pl.BlockSpec(memory_space=pl.ANY)],
            out_specs=pl.BlockSpec(blk, lambda j, *_: (j, 0)),
            scratch_shapes=[pltpu.VMEM((2, d, f), F32), pltpu.VMEM((2, d, f), F32), pltpu.VMEM((2, f, d), F32),
                            pltpu.SemaphoreType.DMA((2, 3))]),
        compiler_params=_cparams("arbitrary"),
        name="experts",
    )(block_e, block_first, block_next, block_slot, n_used, xs, w_gate, w_up, w_down)


def _final_kernel(n_first, x1_ref, h_ref, gt_ref, wts_ref, fg_ref, wg_ref, wu_ref, wd_ref, rows_ref, y_ref,
                  oa_ref, ob_ref, ybuf, sem):
    i = pl.program_id(0)
    tm = x1_ref.shape[0]
    _token_copy_all(lambda k, t, row: _tile(y_ref, row), lambda k, t, row: _tile(ybuf.at[k], t), sem, rows_ref, tm)

    lo, hi = _unpack_halves(_load_token_tiles(h_ref))
    lo, hi = lo.astype(BF16), hi.astype(BF16)
    n = lo.shape[1]
    g = _dot(lo, wg_ref[:n, :]) + _dot(hi, wg_ref[n:, :])
    u = _dot(lo, wu_ref[:n, :]) + _dot(hi, wu_ref[n:, :])
    shared = _dot((g * jax.nn.sigmoid(g) * u).astype(BF16), wd_ref[...])

    for k in range(TOP_K):
        pltpu.make_async_copy(_tile(y_ref, 0, tm), ybuf.at[k], sem).wait()
    acc_lo, acc_hi = shared[:, :n], shared[:, n:]
    for k in range(TOP_K):
        ylo, yhi = _unpack_halves(_load_token_tiles(ybuf.at[k]))
        wk = wts_ref[:, k:k + 1]
        acc_lo = acc_lo + wk * ylo
        acc_hi = acc_hi + wk * yhi
    gt = gt_ref[0]
    x_lo = x1_ref[:, :n] + gt[:, :n] * acc_lo
    x_hi = x1_ref[:, n:] + gt[:, n:] * acc_hi
    ms = (jnp.sum(x_lo * x_lo, axis=-1, keepdims=True) + jnp.sum(x_hi * x_hi, axis=-1, keepdims=True)) / (2 * n)
    r = lax.rsqrt(ms + RMS_EPS)

    def write(o_ref):
        o_ref[:, :n] = x_lo * r * fg_ref[:, :n]
        o_ref[:, n:] = x_hi * r * fg_ref[:, n:]

    pl.when(i < n_first)(lambda: write(oa_ref))
    pl.when(i >= n_first)(lambda: write(ob_ref))


def _final(x1, h2p, gt2, wts_t, final_g, sh_wg, sh_wu, sh_wd, rows, y, rows_first, seq, tm=256):
    t, d = x1.shape
    f = sh_wg.shape[1]
    tm = min(tm, seq)
    per = seq // tm
    n_first = rows_first // tm
    out_a, out_b = _two_trunk_specs((tm, d), n_first)
    const = lambda shp: pl.BlockSpec(shp, lambda i: (0, 0), pipeline_mode=pl.Buffered(1))
    return pl.pallas_call(
        functools.partial(_final_kernel, n_first),
        out_shape=(jax.ShapeDtypeStruct((rows_first, d), F32), jax.ShapeDtypeStruct((t - rows_first, d), F32)),
        grid=(t // tm,),
        in_specs=[pl.BlockSpec((tm, d), lambda i: (i, 0)),
                  pl.BlockSpec((tm * SUBLANES, LANES), lambda i: (i, 0)),
                  pl.BlockSpec((1, 1, d), lambda i: (i // per, 0, 0)),
                  pl.BlockSpec((tm, TOP_K), lambda i: (i, 0)),
                  pl.BlockSpec((1, d), lambda i: (0, 0)),
                  const((d, f)), const((d, f)), const((f, d)),
                  pl.BlockSpec((TOP_K, tm), lambda i: (0, i), memory_space=pltpu.SMEM),
                  pl.BlockSpec(memory_space=pl.ANY)],
        out_specs=(out_a, out_b),
        scratch_shapes=[pltpu.VMEM((TOP_K, tm * SUBLANES, LANES), U32), pltpu.SemaphoreType.DMA(())],
        compiler_params=_cparams("arbitrary"),
        name="final",
    )(x1, h2p, gt2, wts_t, final_g, sh_wg, sh_wu, sh_wd, rows, y)


EXPERT_ROWS = 256


def _prep_moe(w):
    rw = w["router_w"][0]
    d = rw.shape[0]
    wt = rw.T
    wt_hi = wt.astype(BF16)
    return dict(
        wt_hi=wt_hi, wt_lo=(wt - wt_hi.astype(F32)).astype(BF16), bias=w["router_bias"][0],
        w_gate=w["exp_w_gate"][0], w_up=w["exp_w_up"][0], w_down=w["exp_w_down"][0],
        sh_wg=w["sh_w_gate"][0].astype(BF16), sh_wu=w["sh_w_up"][0].astype(BF16),
        sh_wd=w["sh_w_down"][0].astype(BF16), final_g=w["final_g"].reshape(1, d),
    )


def _moe_stage(x1, h2p, mods, pm, pw, seq, rows_first):
    _, _, _, sh2, sc2, gt2 = mods
    t = x1.shape[0]
    ne = pw["wt_hi"].shape[0]
    bm = EXPERT_ROWS
    eidx, wts, rank, cnt = _router(x1, pm["norm_ffn_g"], sc2, sh2, pw["wt_hi"], pw["wt_lo"], pw["bias"], seq)
    counts = cnt[:, 0]
    pcounts = (counts + bm - 1) // bm * bm
    pends = jnp.cumsum(pcounts)
    pstart = (pends - pcounts).astype(I32)
    nblk = (t * TOP_K + ne * (bm - 1) + bm - 1) // bm
    n_used = (pends[-1:] // bm).astype(I32)
    jb = jnp.arange(nblk, dtype=I32)
    be = jnp.minimum(jnp.searchsorted(pends, jb * bm, side="right"), ne - 1).astype(I32)
    first = (jb * bm == pstart[be]).astype(I32)
    live = counts > 0
    ordinal = jnp.cumsum(live.astype(I32)) - 1
    ids = jnp.arange(ne, dtype=I32)
    after = lax.cummin(jnp.where(live, ids, ne)[::-1])[::-1]
    next_live = jnp.concatenate([after[1:], jnp.full((1,), ne, I32)])
    nxt = jnp.where(next_live[be] < ne, next_live[be], -1).astype(I32)
    slot = (ordinal[be] % 2).astype(I32)
    rows = pstart[eidx] + rank
    xs = _dispatch(h2p, rows, pstart, counts, n_used, nblk * bm, bm)
    y = _experts(xs, be, first, nxt, slot, n_used, pw["w_gate"], pw["w_up"], pw["w_down"], bm)
    return _final(x1, h2p, gt2, wts.T, pw["final_g"], pw["sh_wg"], pw["sh_wu"], pw["sh_wd"],
                  rows, y, rows_first, seq)


def kernel(x_prompt, x_sample, c_prompt, c_sample, ada_w, ada_b, norm_mix_g, norm_ffn_g, w_in, sgu_ln_g,
           sgu_ln_b, sgu_ws, sgu_b, dn_conv_w, dn_a_log, dn_dt_bias, dn_norm_g, w_branch, w_out, router_w,
           router_bias, exp_w_gate, exp_w_up, exp_w_down, sh_w_gate, sh_w_up, sh_w_down, final_g):
    w = dict(ada_w=ada_w, ada_b=ada_b, norm_mix_g=norm_mix_g, norm_ffn_g=norm_ffn_g, w_in=w_in,
             sgu_ln_g=sgu_ln_g, sgu_ln_b=sgu_ln_b, sgu_ws=sgu_ws, sgu_b=sgu_b, dn_conv_w=dn_conv_w,
             dn_a_log=dn_a_log, dn_dt_bias=dn_dt_bias, dn_norm_g=dn_norm_g, w_branch=w_branch, w_out=w_out,
             router_w=router_w, router_bias=router_bias, exp_w_gate=exp_w_gate, exp_w_up=exp_w_up,
             exp_w_down=exp_w_down, sh_w_gate=sh_w_gate, sh_w_up=sh_w_up, sh_w_down=sh_w_down, final_g=final_g)
    assert w_in.shape[0] == 1, "one layer"
    bp, seq, d = x_prompt.shape
    bs = x_sample.shape[0]
    assert x_sample.shape[1] == seq
    nb = bp + bs
    xa = x_prompt.reshape(bp * seq, d)
    xb = x_sample.reshape(bs * seq, d)
    c = jnp.concatenate([c_prompt, c_sample], axis=0)
    npad = -nb % 8
    mod = _ada(jnp.pad(c, ((0, npad), (0, 0))), ada_w[0], ada_b[0])[:nb]
    mods = [m.reshape(nb, 1, d) for m in jnp.split(mod, 6, axis=-1)]
    pm = _prep_mixer(w, nb, seq)
    pw = _prep_moe(w)
    x1, h2p = _token_mixer_stage(xa, xb, mods, pm)
    ya, yb = _moe_stage(x1, h2p, mods, pm, pw, seq, bp * seq)
    return (ya.reshape(bp, seq, d), yb.reshape(bs, seq, d))
```

```python
import functools

import jax
import jax.numpy as jnp
import numpy as np
from jax import lax
from jax.experimental import pallas as pl
from jax.experimental.pallas import tpu as pltpu

F32 = jnp.float32
BF16 = jnp.bfloat16
I32 = jnp.int32
U32 = jnp.uint32

RMS_EPS = 1e-6
LN_EPS = 1e-5
L2_EPS = 1e-6

SGU_GROUPS = 8
SGU_CHUNK = 128
DN_HEADS = 8
DN_HEAD_DIM = 128
DN_CHUNK = 256
DN_PREP_UNROLL = 2
CONV_WIDTH = 5
TOP_K = 8
N_GROUPS = 8
TOPK_GROUPS = 4
ROUTED_SCALE = 2.5

LANES = 128
SUBLANES = 8
VMEM_LIMIT = 56 * 1024 * 1024


def _cparams(*sem):
    return pltpu.CompilerParams(dimension_semantics=sem, vmem_limit_bytes=VMEM_LIMIT)


def _split3(x):
    hi = x.astype(BF16)
    r = x - hi.astype(F32)
    mid = r.astype(BF16)
    lo = (r - mid.astype(F32)).astype(BF16)
    return hi, mid, lo


def _dot(a, b):
    return jnp.dot(a, b, preferred_element_type=F32)


def _dot_nt(a, b):
    return lax.dot_general(a, b, (((1,), (1,)), ((), ())), preferred_element_type=F32)


def _dot_tn(a, b):
    return lax.dot_general(a, b, (((0,), (0,)), ((), ())), preferred_element_type=F32)


def _rms(x, g):
    return x * lax.rsqrt(jnp.mean(x * x, axis=-1, keepdims=True) + RMS_EPS) * g


def _pack_halves(x):
    n = x.shape[1] // 2
    return pltpu.pack_elementwise([x[:, :n], x[:, n:]], packed_dtype=BF16)


def _unpack_halves(w):
    lo = pltpu.unpack_elementwise(w, index=0, packed_dtype=BF16, unpacked_dtype=F32)
    hi = pltpu.unpack_elementwise(w, index=1, packed_dtype=BF16, unpacked_dtype=F32)
    return lo, hi


def _store_token_tiles(ref, x):
    m = x.shape[0]
    for s in range(SUBLANES):
        ref[pl.ds(s, m, stride=SUBLANES), :] = x[:, s * LANES:(s + 1) * LANES]


def _load_token_tiles(ref):
    m = ref.shape[0] // SUBLANES
    return jnp.concatenate([ref[pl.ds(s, m, stride=SUBLANES), :] for s in range(SUBLANES)], axis=1)


def _ada_kernel(c_ref, w_ref, b_ref, o_ref):
    c = c_ref[...]
    a = c * jax.nn.sigmoid(c)
    a_hi, a_mid, _ = _split3(a)
    w = w_ref[...]
    w_hi = w.astype(BF16)
    w_lo = (w - w_hi.astype(F32)).astype(BF16)
    o_ref[...] = _dot(a_hi, w_hi) + _dot(a_mid, w_hi) + _dot(a_hi, w_lo) + b_ref[...]


def _ada(c, ada_w, ada_b, tn=1024):
    nb, d = c.shape
    n = ada_w.shape[1]
    return pl.pallas_call(
        _ada_kernel,
        out_shape=jax.ShapeDtypeStruct((nb, n), F32),
        grid=(n // tn,),
        in_specs=[pl.BlockSpec((nb, d), lambda j: (0, 0)),
                  pl.BlockSpec((d, tn), lambda j: (0, j)),
                  pl.BlockSpec((1, tn), lambda j: (0, j))],
        out_specs=pl.BlockSpec((nb, tn), lambda j: (0, j)),
        compiler_params=_cparams("arbitrary"),
        name="ada",
    )(c, ada_w, ada_b.reshape(1, n))


GELU_C = float(np.sqrt(2.0 / np.pi))


def _two_trunk_specs(block, n_first):
    first = pl.BlockSpec(block, lambda i, *_: (jnp.minimum(i, n_first - 1), 0))
    second = pl.BlockSpec(block, lambda i, *_: (jnp.maximum(i - n_first, 0), 0))
    return first, second


def _inproj_kernel(n_first, n_gelu, n_plain, xa_ref, xb_ref, g_ref, sc_ref, sh_ref, w_ref, wab_ref, z_ref, ab_ref,
                   h_s):
    i = pl.program_id(0)
    j = pl.program_id(1)

    def prologue(x_ref):
        h = _rms(x_ref[...], g_ref[...]) * (1.0 + sc_ref[0]) + sh_ref[0]
        h_s[...] = h.astype(BF16)
        ab_ref[...] = _dot(h_s[...], wab_ref[...])

    pl.when((j == 0) & (i < n_first))(lambda: prologue(xa_ref))
    pl.when((j == 0) & (i >= n_first))(lambda: prologue(xb_ref))

    is_gelu = j < n_gelu
    is_plain = (j >= n_gelu) & (j < n_gelu + n_plain)
    cw = 2 * LANES
    nc = w_ref.shape[1] // cw
    cols = [slice(c * cw, (c + 1) * cw) for c in range(nc)]
    nxt = _dot(h_s[...], w_ref[:, cols[0]])
    for c in range(nc):
        acc = nxt
        if c + 1 < nc:
            nxt = _dot(h_s[...], w_ref[:, cols[c + 1]])
        t = jnp.tanh(jnp.where(is_gelu, GELU_C * (acc + 0.044715 * (acc * acc * acc)), 0.5 * acc))
        act = 0.5 * (1.0 + t) * jnp.where(is_gelu, acc, 1.0)
        z_ref[:, cols[c]] = jnp.where(is_plain, acc, act).astype(BF16)


INPROJ_TN = 2048


def _inproj(xa, xb, g, sc, sh, w_main, w_ab, seq, n_gelu, n_plain, tm=512, tn=INPROJ_TN):
    d = xa.shape[1]
    t = xa.shape[0] + xb.shape[0]
    n = w_main.shape[1]
    tm = min(tm, seq)
    per = seq // tm
    n_first = xa.shape[0] // tm
    spec_a, spec_b = _two_trunk_specs((tm, d), n_first)
    return pl.pallas_call(
        functools.partial(_inproj_kernel, n_first, n_gelu, n_plain),
        out_shape=(jax.ShapeDtypeStruct((t, n), BF16), jax.ShapeDtypeStruct((t, LANES), F32)),
        grid=(t // tm, n // tn),
        in_specs=[spec_a, spec_b,
                  pl.BlockSpec((1, d), lambda i, j: (0, 0)),
                  pl.BlockSpec((1, 1, d), lambda i, j: (i // per, 0, 0)),
                  pl.BlockSpec((1, 1, d), lambda i, j: (i // per, 0, 0)),
                  pl.BlockSpec((d, tn), lambda i, j: (0, j)),
                  pl.BlockSpec((d, LANES), lambda i, j: (0, 0))],
        out_specs=(pl.BlockSpec((tm, tn), lambda i, j: (i, j)),
                   pl.BlockSpec((tm, LANES), lambda i, j: (i, 0))),
        scratch_shapes=[pltpu.VMEM((tm, d), BF16)],
        compiler_params=_cparams("arbitrary", "arbitrary"),
        name="inproj",
    )(xa, xb, g, sc, sh, w_main, w_ab)


def _sgu_kernel(u_ref, v_ref, g_ref, b_ref, ws_ref, sb_ref, o_ref):
    v = v_ref[...].astype(F32)
    mu = jnp.mean(v, axis=-1, keepdims=True)
    vc = v - mu
    var = jnp.mean(vc * vc, axis=-1, keepdims=True)
    vn = (vc * lax.rsqrt(var + LN_EPS) * g_ref[...] + b_ref[...]).astype(BF16)
    tm = vn.shape[0]
    for c in range(tm // SGU_CHUNK):
        r = slice(c * SGU_CHUNK, (c + 1) * SGU_CHUNK)
        for gi in range(SGU_GROUPS):
            l = slice(gi * LANES, (gi + 1) * LANES)
            mixed = _dot(ws_ref[gi], vn[r, l]) + sb_ref[gi]
            o_ref[r, l] = (u_ref[r, l].astype(F32) * mixed).astype(BF16)


def _sgu(z, ln_g, ln_b, ws, sb_b, tm=512):
    t = z.shape[0]
    w = SGU_GROUPS * LANES
    tm = np.gcd(tm, t)
    return pl.pallas_call(
        _sgu_kernel,
        out_shape=jax.ShapeDtypeStruct((t, w), BF16),
        grid=(t // tm,),
        in_specs=[pl.BlockSpec((tm, w), lambda i: (i, 0)),
                  pl.BlockSpec((tm, w), lambda i: (i, 1)),
                  pl.BlockSpec((1, w), lambda i: (0, 0)),
                  pl.BlockSpec((1, w), lambda i: (0, 0)),
                  pl.BlockSpec((SGU_GROUPS, SGU_CHUNK, SGU_CHUNK), lambda i: (0, 0, 0)),
                  pl.BlockSpec((SGU_GROUPS, SGU_CHUNK, LANES), lambda i: (0, 0, 0))],
        out_specs=pl.BlockSpec((tm, w), lambda i: (i, 0)),
        compiler_params=_cparams("arbitrary"),
        name="sgu",
    )(z, z, ln_g, ln_b, ws, sb_b)


def _gate_kernel(ab_ref, alog_ref, dt_ref, gcol_ref, grow_ref):
    ab = ab_ref[...]
    tr = ab.shape[0]
    lane = lax.broadcasted_iota(I32, ab.shape, 1)
    z = ab + dt_ref[...]
    softplus = jnp.maximum(z, 0.0) + jnp.log1p(jnp.exp(-jnp.abs(z)))
    g = -jnp.exp(alog_ref[...]) * softplus
    beta = jax.nn.sigmoid(ab)
    nh = DN_HEADS
    g = jnp.where(lane < 2 * nh, g, 0.0)
    i = lax.broadcasted_iota(I32, (tr, tr), 0)
    j = lax.broadcasted_iota(I32, (tr, tr), 1)
    same = (i // DN_CHUNK) == (j // DN_CHUNK)
    lower = jnp.where(same & (j <= i), 1.0, 0.0).astype(BF16)
    upper = jnp.where(same & (j >= i), 1.0, 0.0).astype(BF16)
    g_hi, g_mid, g_lo = _split3(g)
    pre = _dot(lower, g_hi) + _dot(lower, g_mid) + _dot(lower, g_lo)
    suf = _dot(upper, g_hi) + _dot(upper, g_mid) + _dot(upper, g_lo)
    out = jnp.where(lane < nh, pre, jnp.where(lane < 2 * nh, suf, jnp.where(lane < 4 * nh, beta, 0.0)))
    gcol_ref[...] = out
    grow_ref[0] = out.T[:2 * nh, :]


def _gates(ab, alog_row, dt_row, nb, seq, tr=256):
    t = ab.shape[0]
    per = seq // tr
    return pl.pallas_call(
        _gate_kernel,
        out_shape=(jax.ShapeDtypeStruct((t, LANES), F32),
                   jax.ShapeDtypeStruct((nb, 2 * DN_HEADS, seq), F32)),
        grid=(t // tr,),
        in_specs=[pl.BlockSpec((tr, LANES), lambda i: (i, 0)),
                  pl.BlockSpec((1, LANES), lambda i: (0, 0)),
                  pl.BlockSpec((1, LANES), lambda i: (0, 0))],
        out_specs=(pl.BlockSpec((tr, LANES), lambda i: (i, 0)),
                   pl.BlockSpec((1, 2 * DN_HEADS, tr), lambda i: (i // per, 0, i % per))),
        compiler_params=_cparams("arbitrary"),
        name="gates",
    )(ab, alog_row, dt_row)


def _conv_silu(x_ref, w_ref):
    x = x_ref[...].astype(F32)
    s = x.shape[0]
    row = lax.broadcasted_iota(I32, x.shape, 0)
    pad = (CONV_WIDTH - 1) // 2
    acc = x * w_ref[pad:pad + 1, :]
    for j in range(CONV_WIDTH):
        d = j - pad
        if d == 0:
            continue
        xs = pltpu.roll(x, (-d) % s, 0)
        ok = (row + d >= 0) & (row + d < s)
        acc = acc + jnp.where(ok, xs, 0.0) * w_ref[j:j + 1, :]
    return acc * jax.nn.sigmoid(acc)


def _l2n(x):
    return x * lax.rsqrt(jnp.sum(x * x, axis=-1, keepdims=True) + L2_EPS)


def _unit_tri_inverses(mats, i, j):
    c = mats[0].shape[0]
    x = i ^ j
    eye = jnp.where(i == j, 1.0, 0.0)
    tinv = [eye - jnp.where(x < 2, a, 0.0) for a in mats]
    s = 2
    while s < c:
        level = (x >= s) & (x < 2 * s)
        tb = [t.astype(BF16) for t in tinv]
        m = [_dot(t, jnp.where(level, a, 0.0).astype(BF16)).astype(BF16) for t, a in zip(tb, mats)]
        tinv = [t - _dot(mm, b) for t, mm, b in zip(tinv, m, tb)]
        s *= 2
    return tinv


def _dn_prep(items):
    c, hd = items[0][1].shape
    i = lax.broadcasted_iota(I32, (c, c), 0)
    j = lax.broadcasted_iota(I32, (c, c), 1)
    decs, kbs, kbfs = [], [], []
    for forward, q, k, v, gam, grow, beta in items:
        incl = (i >= j) if forward else (i <= j)
        gi = jnp.concatenate([gam] * (c // LANES), axis=1)
        decs.append(jnp.where(incl, jnp.exp(gi - grow), 0.0))
        kbs.append(k * beta)
        kbfs.append(k.astype(BF16))
    kk = [_dot_nt(kb.astype(BF16), kbf) for kb, kbf in zip(kbs, kbfs)]
    mats = [jnp.where((i > j) if it[0] else (i < j), m * dec, 0.0) for it, m, dec in zip(items, kk, decs)]
    tinvs = _unit_tri_inverses(mats, i, j)
    egs = [jnp.exp(it[4]) for it in items]
    uws = [_dot(t.astype(BF16), jnp.concatenate([it[3] * it[6], kb * eg], axis=1).astype(BF16))
           for t, it, kb, eg in zip(tinvs, items, kbs, egs)]
    attns = [_dot_nt(it[1].astype(BF16), kbf) * dec for it, kbf, dec in zip(items, kbfs, decs)]
    outs = []
    for (forward, q, k, v, gam, grow, beta), uw, eg, attn in zip(items, uws, egs, attns):
        glast = gam[c - 1:c, :] if forward else gam[0:1, :]
        kd = k * jnp.exp(glast - gam)
        outs.append((uw[:, :hd], uw[:, hd:].astype(BF16), (q * eg).astype(BF16), kd.T.astype(BF16),
                     attn.astype(BF16)))
    return outs


def _dn_kernel(q_ref, k_ref, v_ref, og_ref, wq_ref, wk_ref, wv_ref, gcol_ref, grow_ref, ng_ref, o_ref,
               q_s, k_s, v_s, gam_s, beta_s, u_s, w_s, qd_s, kdt_s, attn_s, o_s):
    h = pl.program_id(1)
    nh = DN_HEADS
    hd = DN_HEAD_DIM
    q_s[...] = _l2n(_conv_silu(q_ref, wq_ref)) * (hd ** -0.5)
    k_s[...] = _l2n(_conv_silu(k_ref, wk_ref))
    v_s[...] = _conv_silu(v_ref, wv_ref)

    g_hi, g_mid, g_lo = _split3(gcol_ref[...])
    rr = lax.broadcasted_iota(I32, (LANES, 2 * LANES), 0)
    cc = lax.broadcasted_iota(I32, (LANES, 2 * LANES), 1)
    sel = lambda off: jnp.where(rr == h + off + jnp.where(cc < LANES, 0, nh), 1.0, 0.0).astype(BF16)
    sel_g = sel(0)
    gam2 = (_dot(jnp.concatenate([g_hi, g_mid], axis=1), jnp.concatenate([sel_g, sel_g], axis=0))
            + _dot(g_lo, sel_g))
    beta2 = _dot(g_hi, sel(2 * nh))
    for d in range(2):
        gam_s[d] = gam2[:, d * LANES:(d + 1) * LANES]
        beta_s[d] = beta2[:, d * LANES:(d + 1) * LANES]

    s = q_s.shape[0]
    c = DN_CHUNK
    n = s // c

    def prep(it, carry):
        where, items = [], []
        for sub in range(DN_PREP_UNROLL):
            rows = pl.ds(pl.multiple_of((it * DN_PREP_UNROLL + sub) * c, c), c)
            for d in range(2):
                where.append((d, rows))
                items.append((d == 0, q_s[rows, :], k_s[rows, :], v_s[rows, :], gam_s[d, rows, :],
                              grow_ref[0, pl.ds(d * nh + h, 1), rows], beta_s[d, rows, :]))
        for (d, rows), (u, w, qd, kdt, attn) in zip(where, _dn_prep(items)):
            u_s[d, rows, :] = u
            w_s[d, rows, :] = w
            qd_s[d, rows, :] = qd
            kdt_s[d, :, rows] = kdt
            attn_s[d, rows, :] = attn
        return carry

    lax.fori_loop(0, n // DN_PREP_UNROLL, prep, 0)

    def scan(it, states):
        r0s = [pl.multiple_of((it if d == 0 else n - 1 - it) * c, c) for d in range(2)]
        rows = [pl.ds(r0, c) for r0 in r0s]
        rs = [_dot(jnp.concatenate([w_s[d, rows[d], :], qd_s[d, rows[d], :]], axis=0), states[d].astype(BF16))
              for d in range(2)]
        vbs = [(u_s[d, rows[d], :] - rs[d][:c]).astype(BF16) for d in range(2)]
        for d in range(2):
            o_s[d, rows[d], :] = rs[d][c:] + _dot(attn_s[d, rows[d], :], vbs[d])
        new = []
        for d in range(2):
            glast = gam_s[d, pl.ds(r0s[d] + (c - 1 if d == 0 else 0), 1), :]
            new.append(states[d] * jnp.exp(glast) + _dot(kdt_s[d, :, rows[d]], vbs[d]))
        return tuple(new)

    zero = jnp.zeros((hd, hd), F32)
    lax.fori_loop(0, n, scan, (zero, zero))

    o = o_s[0] + o_s[1]
    og = og_ref[...].astype(F32)
    o_ref[...] = (_rms(o, ng_ref[...]) * (og * jax.nn.sigmoid(og))).astype(BF16)


def _deltanet(z, conv_w, gcol, grow, norm_g, nb, seq, qkv_blk, og_blk):
    t = z.shape[0]
    nh, hd, c = DN_HEADS, DN_HEAD_DIM, DN_CHUNK
    assert seq % (c * DN_PREP_UNROLL) == 0
    zspec = lambda off: pl.BlockSpec((seq, hd), lambda b, h: (b, off + h))
    wspec = lambda off: pl.BlockSpec((CONV_WIDTH, hd), lambda b, h: (0, off + h))
    return pl.pallas_call(
        _dn_kernel,
        out_shape=jax.ShapeDtypeStruct((t, nh * hd), BF16),
        grid=(nb, nh),
        in_specs=[zspec(qkv_blk), zspec(qkv_blk + nh), zspec(qkv_blk + 2 * nh), zspec(og_blk),
                  wspec(0), wspec(nh), wspec(2 * nh),
                  pl.BlockSpec((seq, LANES), lambda b, h: (b, 0)),
                  pl.BlockSpec((1, 2 * nh, seq), lambda b, h: (b, 0, 0)),
                  pl.BlockSpec((1, hd), lambda b, h: (0, 0))],
        out_specs=pl.BlockSpec((seq, hd), lambda b, h: (b, h)),
        scratch_shapes=[pltpu.VMEM((seq, hd), F32)] * 3
                       + [pltpu.VMEM((2, seq, hd), F32)] * 3
                       + [pltpu.VMEM((2, seq, hd), BF16)] * 2
                       + [pltpu.VMEM((2, hd, seq), BF16), pltpu.VMEM((2, seq, c), BF16),
                          pltpu.VMEM((2, seq, hd), F32)],
        compiler_params=_cparams("arbitrary", "arbitrary"),
        name="deltanet",
    )(z, z, z, z, conv_w, conv_w, conv_w, gcol, grow, norm_g)


def _merge_kernel(n_first, a_ref, b_ref, ga_ref, gb_ref, xa_ref, xb_ref, gt_ref, wa_ref, wb_ref, wo_ref, g2_ref,
                  sc_ref, sh_ref, x1_ref, h2_ref):
    i = pl.program_id(0)
    ya = _dot(a_ref[...], wa_ref[...])
    yb = _dot(b_ref[...], wb_ref[...])
    m = ga_ref[...].astype(F32) * ya + gb_ref[...].astype(F32) * yb
    y = _dot(m.astype(BF16), wo_ref[...])

    def epilogue(x_ref):
        x1 = x_ref[...] + gt_ref[0] * y
        x1_ref[...] = x1
        h2 = _rms(x1, g2_ref[...]) * (1.0 + sc_ref[0]) + sh_ref[0]
        _store_token_tiles(h2_ref, _pack_halves(h2))

    pl.when(i < n_first)(lambda: epilogue(xa_ref))
    pl.when(i >= n_first)(lambda: epilogue(xb_ref))


def _merge(a_out, b_out, z, xa, xb, gt1, wa, wb, wo, g2, sc2, sh2, seq, gate_blk, tm=256):
    d = xa.shape[1]
    t = xa.shape[0] + xb.shape[0]
    wdt = a_out.shape[1]
    tm = min(tm, seq)
    per = seq // tm
    n_first = xa.shape[0] // tm
    spec_a, spec_b = _two_trunk_specs((tm, d), n_first)
    bvec = pl.BlockSpec((1, 1, d), lambda i: (i // per, 0, 0))
    const = lambda shp: pl.BlockSpec(shp, lambda i: (0, 0), pipeline_mode=pl.Buffered(1))
    return pl.pallas_call(
        functools.partial(_merge_kernel, n_first),
        out_shape=(jax.ShapeDtypeStruct((t, d), F32), jax.ShapeDtypeStruct((t * SUBLANES, LANES), U32)),
        grid=(t // tm,),
        in_specs=[pl.BlockSpec((tm, wdt), lambda i: (i, 0)),
                  pl.BlockSpec((tm, wdt), lambda i: (i, 0)),
                  pl.BlockSpec((tm, d), lambda i: (i, gate_blk)),
                  pl.BlockSpec((tm, d), lambda i: (i, gate_blk + 1)),
                  spec_a, spec_b,
                  bvec, const((wdt, d)), const((wdt, d)), const((d, d)),
                  pl.BlockSpec((1, d), lambda i: (0, 0)), bvec, bvec],
        out_specs=(pl.BlockSpec((tm, d), lambda i: (i, 0)),
                   pl.BlockSpec((tm * SUBLANES, LANES), lambda i: (i, 0))),
        compiler_params=_cparams("arbitrary"),
        name="merge",
    )(a_out, b_out, z, z, xa, xb, gt1, wa, wb, wo, g2, sc2, sh2)


def _prep_mixer(w, nb, seq):
    wdt = SGU_GROUPS * LANES
    nh, hd = DN_HEADS, DN_HEAD_DIM
    w_in = w["w_in"][0]
    d = w_in.shape[0]
    ab0 = 2 * wdt + 4 * nh * hd
    w_main = jnp.concatenate([w_in[:, :ab0], w_in[:, ab0 + 4 * nh:]], axis=1).astype(BF16)
    w_ab = jnp.pad(w_in[:, ab0:ab0 + 4 * nh], ((0, 0), (0, LANES - 4 * nh))).astype(BF16)
    pad_row = lambda v: jnp.pad(v.reshape(1, 2 * nh), ((0, 0), (0, LANES - 2 * nh)))
    sb = w["sgu_b"][0]
    wbr = w["w_branch"][0]
    return dict(
        nb=nb, seq=seq,
        norm_mix_g=w["norm_mix_g"][0].reshape(1, d), norm_ffn_g=w["norm_ffn_g"][0].reshape(1, d),
        w_main=w_main, w_ab=w_ab,
        sgu_ln_g=w["sgu_ln_g"][0].reshape(1, wdt), sgu_ln_b=w["sgu_ln_b"][0].reshape(1, wdt),
        sgu_ws=w["sgu_ws"][0].astype(BF16),
        sgu_sb=jnp.broadcast_to(sb[:, :, None], sb.shape + (LANES,)),
        alog_row=pad_row(w["dn_a_log"][0]), dt_row=pad_row(w["dn_dt_bias"][0]),
        dn_conv_w=w["dn_conv_w"][0], dn_norm_g=w["dn_norm_g"][0].reshape(1, hd),
        wa=wbr[:wdt].astype(BF16), wb=wbr[wdt:].astype(BF16), wo=w["w_out"][0].astype(BF16),
    )


def _token_mixer_stage(xa, xb, mods, p):
    nb, seq = p["nb"], p["seq"]
    d = xa.shape[1]
    sh1, sc1, gt1, sh2, sc2, _ = mods
    wdt = SGU_GROUPS * LANES
    nh = DN_HEADS
    z, ab = _inproj(xa, xb, p["norm_mix_g"], sc1, sh1, p["w_main"], p["w_ab"], seq,
                    n_gelu=2 * wdt // INPROJ_TN, n_plain=4 * nh * DN_HEAD_DIM // INPROJ_TN)
    a_out = _sgu(z, p["sgu_ln_g"], p["sgu_ln_b"], p["sgu_ws"], p["sgu_sb"])
    gcol, grow = _gates(ab, p["alog_row"], p["dt_row"], nb, seq)
    qkv_blk = 2 * wdt // LANES
    b_out = _deltanet(z, p["dn_conv_w"], gcol, grow, p["dn_norm_g"], nb, seq,
                      qkv_blk=qkv_blk, og_blk=qkv_blk + 3 * nh)
    gate_blk = (2 * wdt + 4 * nh * DN_HEAD_DIM) // d
    return _merge(a_out, b_out, z, xa, xb, gt1, p["wa"], p["wb"], p["wo"], p["norm_ffn_g"], sc2, sh2, seq,
                  gate_blk)


def _first_argmax(vals, idx, n, axis):
    mx = jnp.max(vals, axis=axis, keepdims=True)
    ix = jnp.min(jnp.where(vals == mx, idx, n), axis=axis, keepdims=True)
    return mx, ix


def _router_kernel(x_ref, g_ref, sc_ref, sh_ref, whi_ref, wlo_ref, bias_ref,
                   eidx_ref, wts_ref, rank_ref, cnt_ref, base_s):
    i = pl.program_id(0)
    ne, tm = bias_ref.shape
    neg = -jnp.inf

    @pl.when(i == 0)
    def _():
        base_s[...] = jnp.zeros_like(base_s)

    h = _rms(x_ref[...], g_ref[...]) * (1.0 + sc_ref[0]) + sh_ref[0]
    h_hi, h_mid, _ = _split3(h)
    whi = whi_ref[...]
    logits = _dot_nt(whi, h_hi) + _dot_nt(wlo_ref[...], h_hi) + _dot_nt(whi, h_mid)
    scores = jax.nn.sigmoid(logits)
    sel = scores + bias_ref[...]

    per = ne // N_GROUPS
    sel3 = sel.reshape(N_GROUPS, per, tm)
    ri = lax.broadcasted_iota(I32, sel3.shape, 1)
    m1, i1 = _first_argmax(sel3, ri, per, 1)
    m2 = jnp.max(jnp.where(ri == i1, neg, sel3), axis=1, keepdims=True)
    grp = (m1 + m2).reshape(N_GROUPS, tm)

    gi = lax.broadcasted_iota(I32, grp.shape, 0)
    chosen = jnp.zeros(grp.shape, F32)
    for _ in range(TOPK_GROUPS):
        _, ix = _first_argmax(grp, gi, N_GROUPS, 0)
        hit = gi == ix
        chosen = jnp.where(hit, 1.0, chosen)
        grp = jnp.where(hit, neg, grp)
    masked = jnp.where(chosen.reshape(N_GROUPS, 1, tm) > 0.0, sel3, neg).reshape(ne, tm)

    ei = lax.broadcasted_iota(I32, (ne, tm), 0)
    msel = jnp.zeros((ne, tm), F32)
    idx_rows, w_rows = [], []
    for _ in range(TOP_K):
        _, ix = _first_argmax(masked, ei, ne, 0)
        hit = ei == ix
        w_rows.append(jnp.sum(jnp.where(hit, scores, 0.0), axis=0, keepdims=True))
        idx_rows.append(ix)
        msel = jnp.where(hit, 1.0, msel)
        masked = jnp.where(hit, neg, masked)
    w = jnp.concatenate(w_rows, axis=0)
    eidx_ref[...] = jnp.concatenate(idx_rows, axis=0)
    wts_ref[...] = w / (jnp.sum(w, axis=0, keepdims=True) + 1e-20) * ROUTED_SCALE

    mb = msel.astype(BF16)
    a = lax.broadcasted_iota(I32, (tm, tm), 0)
    b = lax.broadcasted_iota(I32, (tm, tm), 1)
    before = _dot(mb, jnp.where(a < b, 1.0, 0.0).astype(BF16))
    pos = base_s[...] + before
    rank_ref[...] = jnp.concatenate(
        [jnp.sum(jnp.where(ei == ix, pos, 0.0), axis=0, keepdims=True) for ix in idx_rows], axis=0).astype(I32)
    base_s[...] = base_s[...] + _dot(mb, jnp.ones((tm, tm), BF16))
    cnt_ref[...] = base_s[:, :LANES].astype(I32)


def _router(x1, g2, sc2, sh2, wt_hi, wt_lo, bias, seq, tm=512):
    t, d = x1.shape
    ne = wt_hi.shape[0]
    tm = min(tm, seq)
    per = seq // tm
    bvec = pl.BlockSpec((1, 1, d), lambda i: (i // per, 0, 0))
    kt = lambda dt: jax.ShapeDtypeStruct((TOP_K, t), dt)
    kspec = pl.BlockSpec((TOP_K, tm), lambda i: (0, i))
    return pl.pallas_call(
        _router_kernel,
        out_shape=(kt(I32), kt(F32), kt(I32), jax.ShapeDtypeStruct((ne, LANES), I32)),
        grid=(t // tm,),
        in_specs=[pl.BlockSpec((tm, d), lambda i: (i, 0)),
                  pl.BlockSpec((1, d), lambda i: (0, 0)), bvec, bvec,
                  pl.BlockSpec((ne, d), lambda i: (0, 0)),
                  pl.BlockSpec((ne, d), lambda i: (0, 0)),
                  pl.BlockSpec((ne, tm), lambda i: (0, 0))],
        out_specs=(kspec, kspec, kspec, pl.BlockSpec((ne, LANES), lambda i: (0, 0))),
        scratch_shapes=[pltpu.VMEM((ne, tm), F32)],
        compiler_params=_cparams("arbitrary"),
        name="router",
    )(x1, g2, sc2, sh2, wt_hi, wt_lo, jnp.broadcast_to(bias.reshape(ne, 1), (ne, tm)))


def _tile(ref, r, n=1):
    return ref.at[pl.ds(pl.multiple_of(r * SUBLANES, SUBLANES), n * SUBLANES), :]


def _token_copy_all(src_of, dst_of, sem, eidx_ref, rank_ref, pstart_ref, tm):
    def issue(t, carry):
        for k in range(TOP_K):
            row = pstart_ref[eidx_ref[k, t]] + rank_ref[k, t]
            pltpu.make_async_copy(src_of(k, t, row), dst_of(k, t, row), sem).start(priority=k % 2)
        return carry
    lax.fori_loop(0, tm, issue, 0)


def _dispatch_kernel(pstart_ref, cnt_ref, nused_ref, h_ref, eidx_ref, rank_ref, xs_ref, zbuf, sem, zsem):
    i = pl.program_id(0)
    tm = h_ref.shape[0] // SUBLANES
    bm = zbuf.shape[0] // SUBLANES
    ne = cnt_ref.shape[0]
    nblk = xs_ref.shape[0] // (bm * SUBLANES)

    def zero_fill(act):
        def per_expert(e, carry):
            c = cnt_ref[e]
            pad = lax.rem(bm - lax.rem(c, bm), bm)
            off = pstart_ref[e] + c
            s = bm // 2
            while s >= 1:
                @pl.when((pad & s) != 0)
                def _(s=s, off=off):
                    act(pltpu.make_async_copy(_tile(zbuf, 0, s), _tile(xs_ref, off, s), zsem))
                off = off + (pad & s)
                s //= 2
            return carry
        lax.fori_loop(0, ne, per_expert, 0)

        def per_block(j, carry):
            act(pltpu.make_async_copy(zbuf, _tile(xs_ref, j * bm, bm), zsem))
            return carry
        lax.fori_loop(nused_ref[0], nblk, per_block, 0)

    @pl.when(i == 0)
    def _():
        zbuf[...] = jnp.zeros_like(zbuf)
        zero_fill(lambda cp: cp.start())

    _token_copy_all(lambda k, t, row: _tile(h_ref, t), lambda k, t, row: _tile(xs_ref, row), sem,
                    eidx_ref, rank_ref, pstart_ref, tm)
    for _ in range(TOP_K):
        pltpu.make_async_copy(h_ref, _tile(xs_ref, 0, tm), sem).wait()

    @pl.when(i == 0)
    def _():
        zero_fill(lambda cp: cp.wait())


def _dispatch(h2p, eidx, rank, pstart, counts, n_used, n_rows, bm, tm=256):
    t = h2p.shape[0] // SUBLANES
    tm = min(tm, t)
    smem = lambda: pl.BlockSpec((TOP_K, tm), lambda i, *_: (0, i), memory_space=pltpu.SMEM)
    return pl.pallas_call(
        _dispatch_kernel,
        out_shape=jax.ShapeDtypeStruct((n_rows * SUBLANES, LANES), U32),
        grid_spec=pltpu.PrefetchScalarGridSpec(
            num_scalar_prefetch=3,
            grid=(t // tm,),
            in_specs=[pl.BlockSpec((tm * SUBLANES, LANES), lambda i, *_: (i, 0)), smem(), smem()],
            out_specs=pl.BlockSpec(memory_space=pl.ANY),
            scratch_shapes=[pltpu.VMEM((bm * SUBLANES, LANES), U32), pltpu.SemaphoreType.DMA(()),
                            pltpu.SemaphoreType.DMA(())]),
        compiler_params=_cparams("arbitrary"),
        name="dispatch",
    )(pstart, counts, n_used, h2p, eidx, rank)


def _expert_kernel(be_ref, first_ref, nxt_ref, slot_ref, nu_ref, xs_ref, wg_hbm, wu_hbm, wd_hbm, y_ref,
                   wg_buf, wu_buf, wd_buf, sems):
    j = pl.program_id(0)
    nu = nu_ref[0]

    def weight_copies(e, s):
        return [pltpu.make_async_copy(hbm.at[e], buf.at[s], sems.at[s, i])
                for i, (hbm, buf) in enumerate(((wg_hbm, wg_buf), (wu_hbm, wu_buf), (wd_hbm, wd_buf)))]

    @pl.when((j == 0) & (nu > 0))
    def _():
        for cp in weight_copies(be_ref[0], slot_ref[0]):
            cp.start()

    @pl.when((j < nu) & (first_ref[j] == 1))
    def _():
        for cp in weight_copies(be_ref[j], slot_ref[j]):
            cp.wait()

        @pl.when(nxt_ref[j] >= 0)
        def _():
            for cp in weight_copies(nxt_ref[j], 1 - slot_ref[j]):
                cp.start()

    @pl.when(j < nu)
    def _():
        s = slot_ref[j]
        lo, hi = _unpack_halves(_load_token_tiles(xs_ref))
        lo, hi = lo.astype(BF16), hi.astype(BF16)
        n = lo.shape[1]
        wg = wg_buf[s].astype(BF16)
        wu = wu_buf[s].astype(BF16)
        g = _dot(lo, wg[:n]) + _dot(hi, wg[n:])
        u = _dot(lo, wu[:n]) + _dot(hi, wu[n:])
        a = (g * jax.nn.sigmoid(g) * u).astype(BF16)
        _store_token_tiles(y_ref, _pack_halves(_dot(a, wd_buf[s].astype(BF16))))

    @pl.when(j >= nu)
    def _():
        y_ref[...] = jnp.zeros_like(y_ref)


def _experts(xs, block_e, block_first, block_next, block_slot, n_used, w_gate, w_up, w_down, bm):
    n_rows = xs.shape[0] // SUBLANES
    ne, d, f = w_gate.shape
    nblk = n_rows // bm
    blk = (bm * SUBLANES, LANES)
    return pl.pallas_call(
        _expert_kernel,
        out_shape=jax.ShapeDtypeStruct(xs.shape, U32),
        grid_spec=pltpu.PrefetchScalarGridSpec(
            num_scalar_prefetch=5,
            grid=(nblk,),
            in_specs=[pl.BlockSpec(blk, lambda j, be, fi, nx, sl, nu: (jnp.minimum(j, nu[0] - 1), 0)),
                      pl.BlockSpec(memory_space=pl.ANY),
                      pl.BlockSpec(memory_space=pl.ANY),
                      pl.BlockSpec(memory_space=pl.ANY)],
            out_specs=pl.BlockSpec(blk, lambda j, *_: (j, 0)),
            scratch_shapes=[pltpu.VMEM((2, d, f), F32), pltpu.VMEM((2, d, f), F32), pltpu.VMEM((2, f, d), F32),
                            pltpu.SemaphoreType.DMA((2, 3))]),
        compiler_params=_cparams("arbitrary"),
        name="experts",
    )(block_e, block_first, block_next, block_slot, n_used, xs, w_gate, w_up, w_down)


def _final_kernel(n_first, pstart_ref, x1_ref, h_ref, gt_ref, wts_ref, fg_ref, wg_ref, wu_ref, wd_ref,
                  eidx_ref, rank_ref, y_ref, oa_ref, ob_ref, ybuf, sem):
    i = pl.program_id(0)
    tm = x1_ref.shape[0]
    _token_copy_all(lambda k, t, row: _tile(y_ref, row), lambda k, t, row: _tile(ybuf.at[k], t), sem,
                    eidx_ref, rank_ref, pstart_ref, tm)

    lo, hi = _unpack_halves(_load_token_tiles(h_ref))
    lo, hi = lo.astype(BF16), hi.astype(BF16)
    n = lo.shape[1]
    g = _dot(lo, wg_ref[:n, :]) + _dot(hi, wg_ref[n:, :])
    u = _dot(lo, wu_ref[:n, :]) + _dot(hi, wu_ref[n:, :])
    shared = _dot((g * jax.nn.sigmoid(g) * u).astype(BF16), wd_ref[...])

    for k in range(TOP_K):
        pltpu.make_async_copy(_tile(y_ref, 0, tm), ybuf.at[k], sem).wait()
    acc_lo, acc_hi = shared[:, :n], shared[:, n:]
    wts = wts_ref[...].T
    for k in range(TOP_K):
        ylo, yhi = _unpack_halves(_load_token_tiles(ybuf.at[k]))
        wk = wts[:, k:k + 1]
        acc_lo = acc_lo + wk * ylo
        acc_hi = acc_hi + wk * yhi
    gt = gt_ref[0]
    x_lo = x1_ref[:, :n] + gt[:, :n] * acc_lo
    x_hi = x1_ref[:, n:] + gt[:, n:] * acc_hi
    ms = (jnp.sum(x_lo * x_lo, axis=-1, keepdims=True) + jnp.sum(x_hi * x_hi, axis=-1, keepdims=True)) / (2 * n)
    r = lax.rsqrt(ms + RMS_EPS)

    def write(o_ref):
        o_ref[:, :n] = x_lo * r * fg_ref[:, :n]
        o_ref[:, n:] = x_hi * r * fg_ref[:, n:]

    pl.when(i < n_first)(lambda: write(oa_ref))
    pl.when(i >= n_first)(lambda: write(ob_ref))


def _final(x1, h2p, gt2, wts, final_g, sh_wg, sh_wu, sh_wd, eidx, rank, pstart, y, rows_first, seq, tm=256):
    t, d = x1.shape
    f = sh_wg.shape[1]
    tm = min(tm, seq)
    per = seq // tm
    n_first = rows_first // tm
    out_a, out_b = _two_trunk_specs((tm, d), n_first)
    const = lambda shp: pl.BlockSpec(shp, lambda i, ps: (0, 0), pipeline_mode=pl.Buffered(1))
    kspec = lambda **kw: pl.BlockSpec((TOP_K, tm), lambda i, ps: (0, i), **kw)
    return pl.pallas_call(
        functools.partial(_final_kernel, n_first),
        out_shape=(jax.ShapeDtypeStruct((rows_first, d), F32), jax.ShapeDtypeStruct((t - rows_first, d), F32)),
        grid_spec=pltpu.PrefetchScalarGridSpec(
            num_scalar_prefetch=1,
            grid=(t // tm,),
            in_specs=[pl.BlockSpec((tm, d), lambda i, ps: (i, 0)),
                      pl.BlockSpec((tm * SUBLANES, LANES), lambda i, ps: (i, 0)),
                      pl.BlockSpec((1, 1, d), lambda i, ps: (i // per, 0, 0)),
                      kspec(),
                      pl.BlockSpec((1, d), lambda i, ps: (0, 0)),
                      const((d, f)), const((d, f)), const((f, d)),
                      kspec(memory_space=pltpu.SMEM), kspec(memory_space=pltpu.SMEM),
                      pl.BlockSpec(memory_space=pl.ANY)],
            out_specs=(out_a, out_b),
            scratch_shapes=[pltpu.VMEM((TOP_K, tm * SUBLANES, LANES), U32), pltpu.SemaphoreType.DMA(())]),
        compiler_params=_cparams("arbitrary"),
        name="final",
    )(pstart, x1, h2p, gt2, wts, final_g, sh_wg, sh_wu, sh_wd, eidx, rank, y)


EXPERT_ROWS = 256


def _prep_moe(w):
    rw = w["router_w"][0]
    d = rw.shape[0]
    wt = rw.T
    wt_hi = wt.astype(BF16)
    return dict(
        wt_hi=wt_hi, wt_lo=(wt - wt_hi.astype(F32)).astype(BF16), bias=w["router_bias"][0],
        w_gate=w["exp_w_gate"][0], w_up=w["exp_w_up"][0], w_down=w["exp_w_down"][0],
        sh_wg=w["sh_w_gate"][0].astype(BF16), sh_wu=w["sh_w_up"][0].astype(BF16),
        sh_wd=w["sh_w_down"][0].astype(BF16), final_g=w["final_g"].reshape(1, d),
    )


def _moe_stage(x1, h2p, mods, pm, pw, seq, rows_first):
    _, _, _, sh2, sc2, gt2 = mods
    t = x1.shape[0]
    ne = pw["wt_hi"].shape[0]
    bm = EXPERT_ROWS
    eidx, wts, rank, cnt = _router(x1, pm["norm_ffn_g"], sc2, sh2, pw["wt_hi"], pw["wt_lo"], pw["bias"], seq)
    counts = cnt[:, 0]
    pcounts = (counts + bm - 1) // bm * bm
    pends = jnp.cumsum(pcounts)
    pstart = (pends - pcounts).astype(I32)
    nblk = (t * TOP_K + ne * (bm - 1) + bm - 1) // bm
    n_used = (pends[-1:] // bm).astype(I32)
    jb = jnp.arange(nblk, dtype=I32)
    be = jnp.minimum(jnp.sum((pends[None, :] <= (jb * bm)[:, None]).astype(I32), axis=1), ne - 1)
    ids = jnp.arange(ne, dtype=I32)
    onehot = be[:, None] == ids[None, :]
    look = lambda table: jnp.sum(jnp.where(onehot, table[None, :], 0), axis=1).astype(I32)
    first = (jb * bm == look(pstart)).astype(I32)
    live = counts > 0
    ordinal = jnp.cumsum(live.astype(I32)) - 1
    after = lax.cummin(jnp.where(live, ids, ne), reverse=True)
    next_live = jnp.concatenate([after[1:], jnp.full((1,), ne, I32)])
    nxt = look(jnp.where(next_live < ne, next_live, -1))
    slot = look(ordinal) % 2
    xs = _dispatch(h2p, eidx, rank, pstart, counts, n_used, nblk * bm, bm)
    y = _experts(xs, be, first, nxt, slot, n_used, pw["w_gate"], pw["w_up"], pw["w_down"], bm)
    return _final(x1, h2p, gt2, wts, pw["final_g"], pw["sh_wg"], pw["sh_wu"], pw["sh_wd"],
                  eidx, rank, pstart, y, rows_first, seq)


def kernel(x_prompt, x_sample, c_prompt, c_sample, ada_w, ada_b, norm_mix_g, norm_ffn_g, w_in, sgu_ln_g,
           sgu_ln_b, sgu_ws, sgu_b, dn_conv_w, dn_a_log, dn_dt_bias, dn_norm_g, w_branch, w_out, router_w,
           router_bias, exp_w_gate, exp_w_up, exp_w_down, sh_w_gate, sh_w_up, sh_w_down, final_g):
    w = dict(ada_w=ada_w, ada_b=ada_b, norm_mix_g=norm_mix_g, norm_ffn_g=norm_ffn_g, w_in=w_in,
             sgu_ln_g=sgu_ln_g, sgu_ln_b=sgu_ln_b, sgu_ws=sgu_ws, sgu_b=sgu_b, dn_conv_w=dn_conv_w,
             dn_a_log=dn_a_log, dn_dt_bias=dn_dt_bias, dn_norm_g=dn_norm_g, w_branch=w_branch, w_out=w_out,
             router_w=router_w, router_bias=router_bias, exp_w_gate=exp_w_gate, exp_w_up=exp_w_up,
             exp_w_down=exp_w_down, sh_w_gate=sh_w_gate, sh_w_up=sh_w_up, sh_w_down=sh_w_down, final_g=final_g)
    assert w_in.shape[0] == 1, "one layer"
    bp, seq, d = x_prompt.shape
    bs = x_sample.shape[0]
    assert x_sample.shape[1] == seq
    nb = bp + bs
    xa = x_prompt.reshape(bp * seq, d)
    xb = x_sample.reshape(bs * seq, d)
    c = jnp.concatenate([c_prompt, c_sample], axis=0)
    npad = -nb % 8
    mod = _ada(jnp.pad(c, ((0, npad), (0, 0))), ada_w[0], ada_b[0])[:nb]
    mods = [m.reshape(nb, 1, d) for m in jnp.split(mod, 6, axis=-1)]
    pm = _prep_mixer(w, nb, seq)
    pw = _prep_moe(w)
    x1, h2p = _token_mixer_stage(xa, xb, mods, pm)
    ya, yb = _moe_stage(x1, h2p, mods, pm, pw, seq, bp * seq)
    return (ya.reshape(bp, seq, d), yb.reshape(bs, seq, d))
```

```python
import functools

import jax
import jax.numpy as jnp
import numpy as np
from jax import lax
from jax.experimental import pallas as pl
from jax.experimental.pallas import tpu as pltpu

F32 = jnp.float32
BF16 = jnp.bfloat16
I32 = jnp.int32
U32 = jnp.uint32

RMS_EPS = 1e-6
LN_EPS = 1e-5
L2_EPS = 1e-6

SGU_GROUPS = 8
SGU_CHUNK = 128
DN_HEADS = 8
DN_HEAD_DIM = 128
DN_CHUNK = 256
DN_PREP_UNROLL = 2
CONV_WIDTH = 5
TOP_K = 8
N_GROUPS = 8
TOPK_GROUPS = 4
ROUTED_SCALE = 2.5

LANES = 128
SUBLANES = 8
VMEM_LIMIT = 56 * 1024 * 1024


def _cparams(*sem):
    return pltpu.CompilerParams(dimension_semantics=sem, vmem_limit_bytes=VMEM_LIMIT)


def _split3(x):
    hi = x.astype(BF16)
    r = x - hi.astype(F32)
    mid = r.astype(BF16)
    lo = (r - mid.astype(F32)).astype(BF16)
    return hi, mid, lo


def _dot(a, b):
    return jnp.dot(a, b, preferred_element_type=F32)


def _dot_nt(a, b):
    return lax.dot_general(a, b, (((1,), (1,)), ((), ())), preferred_element_type=F32)


def _dot_tn(a, b):
    return lax.dot_general(a, b, (((0,), (0,)), ((), ())), preferred_element_type=F32)


def _rms(x, g):
    return x * lax.rsqrt(jnp.mean(x * x, axis=-1, keepdims=True) + RMS_EPS) * g


def _pack_halves(x):
    n = x.shape[1] // 2
    return pltpu.pack_elementwise([x[:, :n], x[:, n:]], packed_dtype=BF16)


def _unpack_halves(w):
    lo = pltpu.unpack_elementwise(w, index=0, packed_dtype=BF16, unpacked_dtype=F32)
    hi = pltpu.unpack_elementwise(w, index=1, packed_dtype=BF16, unpacked_dtype=F32)
    return lo, hi


def _store_token_tiles(ref, x):
    m = x.shape[0]
    for s in range(SUBLANES):
        ref[pl.ds(s, m, stride=SUBLANES), :] = x[:, s * LANES:(s + 1) * LANES]


def _load_token_tiles(ref):
    m = ref.shape[0] // SUBLANES
    return jnp.concatenate([ref[pl.ds(s, m, stride=SUBLANES), :] for s in range(SUBLANES)], axis=1)


def _ada_kernel(c_ref, w_ref, b_ref, o_ref):
    c = c_ref[...]
    a = c * jax.nn.sigmoid(c)
    a_hi, a_mid, _ = _split3(a)
    w = w_ref[...]
    w_hi = w.astype(BF16)
    w_lo = (w - w_hi.astype(F32)).astype(BF16)
    o_ref[...] = _dot(a_hi, w_hi) + _dot(a_mid, w_hi) + _dot(a_hi, w_lo) + b_ref[...]


def _ada(c, ada_w, ada_b, tn=1024):
    nb, d = c.shape
    n = ada_w.shape[1]
    return pl.pallas_call(
        _ada_kernel,
        out_shape=jax.ShapeDtypeStruct((nb, n), F32),
        grid=(n // tn,),
        in_specs=[pl.BlockSpec((nb, d), lambda j: (0, 0)),
                  pl.BlockSpec((d, tn), lambda j: (0, j)),
                  pl.BlockSpec((1, tn), lambda j: (0, j))],
        out_specs=pl.BlockSpec((nb, tn), lambda j: (0, j)),
        compiler_params=_cparams("arbitrary"),
        name="ada",
    )(c, ada_w, ada_b.reshape(1, n))


def _two_trunk_specs(block, n_first):
    first = pl.BlockSpec(block, lambda i, *_: (jnp.minimum(i, n_first - 1), 0))
    second = pl.BlockSpec(block, lambda i, *_: (jnp.maximum(i - n_first, 0), 0))
    return first, second


def _inproj_kernel(n_first, xa_ref, xb_ref, g_ref, sc_ref, sh_ref, w_ref, wab_ref, z_ref, ab_ref, h_s):
    i = pl.program_id(0)
    j = pl.program_id(1)

    def prologue(x_ref):
        h = _rms(x_ref[...], g_ref[...]) * (1.0 + sc_ref[0]) + sh_ref[0]
        h_s[...] = h.astype(BF16)
        ab_ref[...] = _dot(h_s[...], wab_ref[...])

    pl.when((j == 0) & (i < n_first))(lambda: prologue(xa_ref))
    pl.when((j == 0) & (i >= n_first))(lambda: prologue(xb_ref))
    z_ref[...] = _dot(h_s[...], w_ref[...]).astype(BF16)


def _inproj(xa, xb, g, sc, sh, w_main, w_ab, seq, tm=512, tn=1024):
    d = xa.shape[1]
    t = xa.shape[0] + xb.shape[0]
    n = w_main.shape[1]
    tm = min(tm, seq)
    per = seq // tm
    n_first = xa.shape[0] // tm
    spec_a, spec_b = _two_trunk_specs((tm, d), n_first)
    return pl.pallas_call(
        functools.partial(_inproj_kernel, n_first),
        out_shape=(jax.ShapeDtypeStruct((t, n), BF16), jax.ShapeDtypeStruct((t, LANES), F32)),
        grid=(t // tm, n // tn),
        in_specs=[spec_a, spec_b,
                  pl.BlockSpec((1, d), lambda i, j: (0, 0)),
                  pl.BlockSpec((1, 1, d), lambda i, j: (i // per, 0, 0)),
                  pl.BlockSpec((1, 1, d), lambda i, j: (i // per, 0, 0)),
                  pl.BlockSpec((d, tn), lambda i, j: (0, j)),
                  pl.BlockSpec((d, LANES), lambda i, j: (0, 0))],
        out_specs=(pl.BlockSpec((tm, tn), lambda i, j: (i, j)),
                   pl.BlockSpec((tm, LANES), lambda i, j: (i, 0))),
        scratch_shapes=[pltpu.VMEM((tm, d), BF16)],
        compiler_params=_cparams("arbitrary", "arbitrary"),
        name="inproj",
    )(xa, xb, g, sc, sh, w_main, w_ab)


def _sgu_kernel(u_ref, v_ref, g_ref, b_ref, ws_ref, sb_ref, o_ref):
    v = jax.nn.gelu(v_ref[...].astype(F32))
    mu = jnp.mean(v, axis=-1, keepdims=True)
    vc = v - mu
    var = jnp.mean(vc * vc, axis=-1, keepdims=True)
    vn = (vc * lax.rsqrt(var + LN_EPS) * g_ref[...] + b_ref[...]).astype(BF16)
    tm = vn.shape[0]
    for c in range(tm // SGU_CHUNK):
        r = slice(c * SGU_CHUNK, (c + 1) * SGU_CHUNK)
        for gi in range(SGU_GROUPS):
            l = slice(gi * LANES, (gi + 1) * LANES)
            mixed = _dot(ws_ref[gi], vn[r, l]) + sb_ref[gi]
            o_ref[r, l] = (jax.nn.gelu(u_ref[r, l].astype(F32)) * mixed).astype(BF16)


def _sgu(z, ln_g, ln_b, ws, sb_b, tm=512):
    t = z.shape[0]
    w = SGU_GROUPS * LANES
    tm = np.gcd(tm, t)
    return pl.pallas_call(
        _sgu_kernel,
        out_shape=jax.ShapeDtypeStruct((t, w), BF16),
        grid=(t // tm,),
        in_specs=[pl.BlockSpec((tm, w), lambda i: (i, 0)),
                  pl.BlockSpec((tm, w), lambda i: (i, 1)),
                  pl.BlockSpec((1, w), lambda i: (0, 0)),
                  pl.BlockSpec((1, w), lambda i: (0, 0)),
                  pl.BlockSpec((SGU_GROUPS, SGU_CHUNK, SGU_CHUNK), lambda i: (0, 0, 0)),
                  pl.BlockSpec((SGU_GROUPS, SGU_CHUNK, LANES), lambda i: (0, 0, 0))],
        out_specs=pl.BlockSpec((tm, w), lambda i: (i, 0)),
        compiler_params=_cparams("arbitrary"),
        name="sgu",
    )(z, z, ln_g, ln_b, ws, sb_b)


def _gate_kernel(ab_ref, alog_ref, dt_ref, gcol_ref, grow_ref):
    ab = ab_ref[...]
    tr = ab.shape[0]
    lane = lax.broadcasted_iota(I32, ab.shape, 1)
    z = ab + dt_ref[...]
    softplus = jnp.maximum(z, 0.0) + jnp.log1p(jnp.exp(-jnp.abs(z)))
    g = -jnp.exp(alog_ref[...]) * softplus
    beta = jax.nn.sigmoid(ab)
    nh = DN_HEADS
    g = jnp.where(lane < 2 * nh, g, 0.0)
    i = lax.broadcasted_iota(I32, (tr, tr), 0)
    j = lax.broadcasted_iota(I32, (tr, tr), 1)
    same = (i // DN_CHUNK) == (j // DN_CHUNK)
    lower = jnp.where(same & (j <= i), 1.0, 0.0).astype(BF16)
    upper = jnp.where(same & (j >= i), 1.0, 0.0).astype(BF16)
    g_hi, g_mid, g_lo = _split3(g)
    pre = _dot(lower, g_hi) + _dot(lower, g_mid) + _dot(lower, g_lo)
    suf = _dot(upper, g_hi) + _dot(upper, g_mid) + _dot(upper, g_lo)
    out = jnp.where(lane < nh, pre, jnp.where(lane < 2 * nh, suf, jnp.where(lane < 4 * nh, beta, 0.0)))
    gcol_ref[...] = out
    grow_ref[0] = out.T[:2 * nh, :]


def _gates(ab, alog_row, dt_row, nb, seq, tr=256):
    t = ab.shape[0]
    per = seq // tr
    return pl.pallas_call(
        _gate_kernel,
        out_shape=(jax.ShapeDtypeStruct((t, LANES), F32),
                   jax.ShapeDtypeStruct((nb, 2 * DN_HEADS, seq), F32)),
        grid=(t // tr,),
        in_specs=[pl.BlockSpec((tr, LANES), lambda i: (i, 0)),
                  pl.BlockSpec((1, LANES), lambda i: (0, 0)),
                  pl.BlockSpec((1, LANES), lambda i: (0, 0))],
        out_specs=(pl.BlockSpec((tr, LANES), lambda i: (i, 0)),
                   pl.BlockSpec((1, 2 * DN_HEADS, tr), lambda i: (i // per, 0, i % per))),
        compiler_params=_cparams("arbitrary"),
        name="gates",
    )(ab, alog_row, dt_row)


def _conv_silu(x_ref, w_ref):
    x = x_ref[...].astype(F32)
    s = x.shape[0]
    row = lax.broadcasted_iota(I32, x.shape, 0)
    pad = (CONV_WIDTH - 1) // 2
    acc = x * w_ref[pad:pad + 1, :]
    for j in range(CONV_WIDTH):
        d = j - pad
        if d == 0:
            continue
        xs = pltpu.roll(x, (-d) % s, 0)
        ok = (row + d >= 0) & (row + d < s)
        acc = acc + jnp.where(ok, xs, 0.0) * w_ref[j:j + 1, :]
    return acc * jax.nn.sigmoid(acc)


def _l2n(x):
    return x * lax.rsqrt(jnp.sum(x * x, axis=-1, keepdims=True) + L2_EPS)


def _unit_tri_inverses(mats, i, j):
    c = mats[0].shape[0]
    x = i ^ j
    eye = jnp.where(i == j, 1.0, 0.0)
    tinv = [eye - jnp.where(x < 2, a, 0.0) for a in mats]
    s = 2
    while s < c:
        level = (x >= s) & (x < 2 * s)
        tb = [t.astype(BF16) for t in tinv]
        m = [_dot(t, jnp.where(level, a, 0.0).astype(BF16)).astype(BF16) for t, a in zip(tb, mats)]
        tinv = [t - _dot(mm, b) for t, mm, b in zip(tinv, m, tb)]
        s *= 2
    return tinv


def _dn_prep(items):
    c, hd = items[0][1].shape
    i = lax.broadcasted_iota(I32, (c, c), 0)
    j = lax.broadcasted_iota(I32, (c, c), 1)
    decs, kbs, kbfs = [], [], []
    for forward, q, k, v, gam, grow, beta in items:
        incl = (i >= j) if forward else (i <= j)
        gi = jnp.concatenate([gam] * (c // LANES), axis=1)
        decs.append(jnp.where(incl, jnp.exp(gi - grow), 0.0))
        kbs.append(k * beta)
        kbfs.append(k.astype(BF16))
    kk = [_dot_nt(kb.astype(BF16), kbf) for kb, kbf in zip(kbs, kbfs)]
    mats = [jnp.where((i > j) if it[0] else (i < j), m * dec, 0.0) for it, m, dec in zip(items, kk, decs)]
    tinvs = _unit_tri_inverses(mats, i, j)
    egs = [jnp.exp(it[4]) for it in items]
    uws = [_dot(t.astype(BF16), jnp.concatenate([it[3] * it[6], kb * eg], axis=1).astype(BF16))
           for t, it, kb, eg in zip(tinvs, items, kbs, egs)]
    attns = [_dot_nt(it[1].astype(BF16), kbf) * dec for it, kbf, dec in zip(items, kbfs, decs)]
    outs = []
    for (forward, q, k, v, gam, grow, beta), uw, eg, attn in zip(items, uws, egs, attns):
        glast = gam[c - 1:c, :] if forward else gam[0:1, :]
        kd = k * jnp.exp(glast - gam)
        outs.append((uw[:, :hd], uw[:, hd:].astype(BF16), (q * eg).astype(BF16), kd.T.astype(BF16),
                     attn.astype(BF16)))
    return outs


def _dn_kernel(q_ref, k_ref, v_ref, og_ref, wq_ref, wk_ref, wv_ref, gcol_ref, grow_ref, ng_ref, o_ref,
               q_s, k_s, v_s, gam_s, beta_s, u_s, w_s, qd_s, kdt_s, attn_s, o_s):
    h = pl.program_id(1)
    nh = DN_HEADS
    hd = DN_HEAD_DIM
    q_s[...] = _l2n(_conv_silu(q_ref, wq_ref)) * (hd ** -0.5)
    k_s[...] = _l2n(_conv_silu(k_ref, wk_ref))
    v_s[...] = _conv_silu(v_ref, wv_ref)

    g_hi, g_mid, g_lo = _split3(gcol_ref[...])
    rr = lax.broadcasted_iota(I32, (LANES, 2 * LANES), 0)
    cc = lax.broadcasted_iota(I32, (LANES, 2 * LANES), 1)
    sel = lambda off: jnp.where(rr == h + off + jnp.where(cc < LANES, 0, nh), 1.0, 0.0).astype(BF16)
    sel_g = sel(0)
    gam2 = (_dot(jnp.concatenate([g_hi, g_mid], axis=1), jnp.concatenate([sel_g, sel_g], axis=0))
            + _dot(g_lo, sel_g))
    beta2 = _dot(g_hi, sel(2 * nh))
    for d in range(2):
        gam_s[d] = gam2[:, d * LANES:(d + 1) * LANES]
        beta_s[d] = beta2[:, d * LANES:(d + 1) * LANES]

    s = q_s.shape[0]
    c = DN_CHUNK
    n = s // c

    def prep(it, carry):
        where, items = [], []
        for sub in range(DN_PREP_UNROLL):
            rows = pl.ds(pl.multiple_of((it * DN_PREP_UNROLL + sub) * c, c), c)
            for d in range(2):
                where.append((d, rows))
                items.append((d == 0, q_s[rows, :], k_s[rows, :], v_s[rows, :], gam_s[d, rows, :],
                              grow_ref[0, pl.ds(d * nh + h, 1), rows], beta_s[d, rows, :]))
        for (d, rows), (u, w, qd, kdt, attn) in zip(where, _dn_prep(items)):
            u_s[d, rows, :] = u
            w_s[d, rows, :] = w
            qd_s[d, rows, :] = qd
            kdt_s[d, :, rows] = kdt
            attn_s[d, rows, :] = attn
        return carry

    lax.fori_loop(0, n // DN_PREP_UNROLL, prep, 0)

    def scan(it, states):
        r0s = [pl.multiple_of((it if d == 0 else n - 1 - it) * c, c) for d in range(2)]
        rows = [pl.ds(r0, c) for r0 in r0s]
        rs = [_dot(jnp.concatenate([w_s[d, rows[d], :], qd_s[d, rows[d], :]], axis=0), states[d].astype(BF16))
              for d in range(2)]
        vbs = [(u_s[d, rows[d], :] - rs[d][:c]).astype(BF16) for d in range(2)]
        for d in range(2):
            o_s[d, rows[d], :] = rs[d][c:] + _dot(attn_s[d, rows[d], :], vbs[d])
        new = []
        for d in range(2):
            glast = gam_s[d, pl.ds(r0s[d] + (c - 1 if d == 0 else 0), 1), :]
            new.append(states[d] * jnp.exp(glast) + _dot(kdt_s[d, :, rows[d]], vbs[d]))
        return tuple(new)

    zero = jnp.zeros((hd, hd), F32)
    lax.fori_loop(0, n, scan, (zero, zero))

    o = o_s[0] + o_s[1]
    og = og_ref[...].astype(F32)
    o_ref[...] = (_rms(o, ng_ref[...]) * (og * jax.nn.sigmoid(og))).astype(BF16)


def _deltanet(z, conv_w, gcol, grow, norm_g, nb, seq, qkv_blk, og_blk):
    t = z.shape[0]
    nh, hd, c = DN_HEADS, DN_HEAD_DIM, DN_CHUNK
    assert seq % (c * DN_PREP_UNROLL) == 0
    zspec = lambda off: pl.BlockSpec((seq, hd), lambda b, h: (b, off + h))
    wspec = lambda off: pl.BlockSpec((CONV_WIDTH, hd), lambda b, h: (0, off + h))
    return pl.pallas_call(
        _dn_kernel,
        out_shape=jax.ShapeDtypeStruct((t, nh * hd), BF16),
        grid=(nb, nh),
        in_specs=[zspec(qkv_blk), zspec(qkv_blk + nh), zspec(qkv_blk + 2 * nh), zspec(og_blk),
                  wspec(0), wspec(nh), wspec(2 * nh),
                  pl.BlockSpec((seq, LANES), lambda b, h: (b, 0)),
                  pl.BlockSpec((1, 2 * nh, seq), lambda b, h: (b, 0, 0)),
                  pl.BlockSpec((1, hd), lambda b, h: (0, 0))],
        out_specs=pl.BlockSpec((seq, hd), lambda b, h: (b, h)),
        scratch_shapes=[pltpu.VMEM((seq, hd), F32)] * 3
                       + [pltpu.VMEM((2, seq, hd), F32)] * 3
                       + [pltpu.VMEM((2, seq, hd), BF16)] * 2
                       + [pltpu.VMEM((2, hd, seq), BF16), pltpu.VMEM((2, seq, c), BF16),
                          pltpu.VMEM((2, seq, hd), F32)],
        compiler_params=_cparams("arbitrary", "arbitrary"),
        name="deltanet",
    )(z, z, z, z, conv_w, conv_w, conv_w, gcol, grow, norm_g)


def _merge_kernel(n_first, a_ref, b_ref, ga_ref, gb_ref, xa_ref, xb_ref, gt_ref, wa_ref, wb_ref, wo_ref, g2_ref,
                  sc_ref, sh_ref, x1_ref, h2_ref):
    i = pl.program_id(0)
    ya = _dot(a_ref[...], wa_ref[...])
    yb = _dot(b_ref[...], wb_ref[...])
    m = jax.nn.sigmoid(ga_ref[...].astype(F32)) * ya + jax.nn.sigmoid(gb_ref[...].astype(F32)) * yb
    y = _dot(m.astype(BF16), wo_ref[...])

    def epilogue(x_ref):
        x1 = x_ref[...] + gt_ref[0] * y
        x1_ref[...] = x1
        h2 = _rms(x1, g2_ref[...]) * (1.0 + sc_ref[0]) + sh_ref[0]
        _store_token_tiles(h2_ref, _pack_halves(h2))

    pl.when(i < n_first)(lambda: epilogue(xa_ref))
    pl.when(i >= n_first)(lambda: epilogue(xb_ref))


def _merge(a_out, b_out, z, xa, xb, gt1, wa, wb, wo, g2, sc2, sh2, seq, gate_blk, tm=256):
    d = xa.shape[1]
    t = xa.shape[0] + xb.shape[0]
    wdt = a_out.shape[1]
    tm = min(tm, seq)
    per = seq // tm
    n_first = xa.shape[0] // tm
    spec_a, spec_b = _two_trunk_specs((tm, d), n_first)
    bvec = pl.BlockSpec((1, 1, d), lambda i: (i // per, 0, 0))
    const = lambda shp: pl.BlockSpec(shp, lambda i: (0, 0), pipeline_mode=pl.Buffered(1))
    return pl.pallas_call(
        functools.partial(_merge_kernel, n_first),
        out_shape=(jax.ShapeDtypeStruct((t, d), F32), jax.ShapeDtypeStruct((t * SUBLANES, LANES), U32)),
        grid=(t // tm,),
        in_specs=[pl.BlockSpec((tm, wdt), lambda i: (i, 0)),
                  pl.BlockSpec((tm, wdt), lambda i: (i, 0)),
                  pl.BlockSpec((tm, d), lambda i: (i, gate_blk)),
                  pl.BlockSpec((tm, d), lambda i: (i, gate_blk + 1)),
                  spec_a, spec_b,
                  bvec, const((wdt, d)), const((wdt, d)), const((d, d)),
                  pl.BlockSpec((1, d), lambda i: (0, 0)), bvec, bvec],
        out_specs=(pl.BlockSpec((tm, d), lambda i: (i, 0)),
                   pl.BlockSpec((tm * SUBLANES, LANES), lambda i: (i, 0))),
        compiler_params=_cparams("arbitrary"),
        name="merge",
    )(a_out, b_out, z, z, xa, xb, gt1, wa, wb, wo, g2, sc2, sh2)


def _prep_mixer(w, nb, seq):
    wdt = SGU_GROUPS * LANES
    nh, hd = DN_HEADS, DN_HEAD_DIM
    w_in = w["w_in"][0]
    d = w_in.shape[0]
    ab0 = 2 * wdt + 4 * nh * hd
    w_main = jnp.concatenate([w_in[:, :ab0], w_in[:, ab0 + 4 * nh:]], axis=1).astype(BF16)
    w_ab = jnp.pad(w_in[:, ab0:ab0 + 4 * nh], ((0, 0), (0, LANES - 4 * nh))).astype(BF16)
    pad_row = lambda v: jnp.pad(v.reshape(1, 2 * nh), ((0, 0), (0, LANES - 2 * nh)))
    sb = w["sgu_b"][0]
    wbr = w["w_branch"][0]
    return dict(
        nb=nb, seq=seq,
        norm_mix_g=w["norm_mix_g"][0].reshape(1, d), norm_ffn_g=w["norm_ffn_g"][0].reshape(1, d),
        w_main=w_main, w_ab=w_ab,
        sgu_ln_g=w["sgu_ln_g"][0].reshape(1, wdt), sgu_ln_b=w["sgu_ln_b"][0].reshape(1, wdt),
        sgu_ws=w["sgu_ws"][0].astype(BF16),
        sgu_sb=jnp.broadcast_to(sb[:, :, None], sb.shape + (LANES,)),
        alog_row=pad_row(w["dn_a_log"][0]), dt_row=pad_row(w["dn_dt_bias"][0]),
        dn_conv_w=w["dn_conv_w"][0], dn_norm_g=w["dn_norm_g"][0].reshape(1, hd),
        wa=wbr[:wdt].astype(BF16), wb=wbr[wdt:].astype(BF16), wo=w["w_out"][0].astype(BF16),
    )


def _token_mixer_stage(xa, xb, mods, p):
    nb, seq = p["nb"], p["seq"]
    d = xa.shape[1]
    sh1, sc1, gt1, sh2, sc2, _ = mods
    wdt = SGU_GROUPS * LANES
    nh = DN_HEADS
    z, ab = _inproj(xa, xb, p["norm_mix_g"], sc1, sh1, p["w_main"], p["w_ab"], seq)
    a_out = _sgu(z, p["sgu_ln_g"], p["sgu_ln_b"], p["sgu_ws"], p["sgu_sb"])
    gcol, grow = _gates(ab, p["alog_row"], p["dt_row"], nb, seq)
    qkv_blk = 2 * wdt // LANES
    b_out = _deltanet(z, p["dn_conv_w"], gcol, grow, p["dn_norm_g"], nb, seq,
                      qkv_blk=qkv_blk, og_blk=qkv_blk + 3 * nh)
    gate_blk = (2 * wdt + 4 * nh * DN_HEAD_DIM) // d
    return _merge(a_out, b_out, z, xa, xb, gt1, p["wa"], p["wb"], p["wo"], p["norm_ffn_g"], sc2, sh2, seq,
                  gate_blk)


def _first_argmax(vals, idx, n, axis):
    mx = jnp.max(vals, axis=axis, keepdims=True)
    ix = jnp.min(jnp.where(vals == mx, idx, n), axis=axis, keepdims=True)
    return mx, ix


def _router_kernel(x_ref, g_ref, sc_ref, sh_ref, whi_ref, wlo_ref, bias_ref,
                   eidx_ref, wts_ref, rank_ref, cnt_ref, base_s):
    i = pl.program_id(0)
    ne, tm = bias_ref.shape
    neg = -jnp.inf

    @pl.when(i == 0)
    def _():
        base_s[...] = jnp.zeros_like(base_s)

    h = _rms(x_ref[...], g_ref[...]) * (1.0 + sc_ref[0]) + sh_ref[0]
    h_hi, h_mid, _ = _split3(h)
    whi = whi_ref[...]
    logits = _dot_nt(whi, h_hi) + _dot_nt(wlo_ref[...], h_hi) + _dot_nt(whi, h_mid)
    scores = jax.nn.sigmoid(logits)
    sel = scores + bias_ref[...]

    per = ne // N_GROUPS
    sel3 = sel.reshape(N_GROUPS, per, tm)
    ri = lax.broadcasted_iota(I32, sel3.shape, 1)
    m1, i1 = _first_argmax(sel3, ri, per, 1)
    m2 = jnp.max(jnp.where(ri == i1, neg, sel3), axis=1, keepdims=True)
    grp = (m1 + m2).reshape(N_GROUPS, tm)

    gi = lax.broadcasted_iota(I32, grp.shape, 0)
    chosen = jnp.zeros(grp.shape, F32)
    for _ in range(TOPK_GROUPS):
        _, ix = _first_argmax(grp, gi, N_GROUPS, 0)
        hit = gi == ix
        chosen = jnp.where(hit, 1.0, chosen)
        grp = jnp.where(hit, neg, grp)
    masked = jnp.where(chosen.reshape(N_GROUPS, 1, tm) > 0.0, sel3, neg).reshape(ne, tm)

    ei = lax.broadcasted_iota(I32, (ne, tm), 0)
    msel = jnp.zeros((ne, tm), F32)
    idx_rows, w_rows = [], []
    for _ in range(TOP_K):
        _, ix = _first_argmax(masked, ei, ne, 0)
        hit = ei == ix
        w_rows.append(jnp.sum(jnp.where(hit, scores, 0.0), axis=0, keepdims=True))
        idx_rows.append(ix)
        msel = jnp.where(hit, 1.0, msel)
        masked = jnp.where(hit, neg, masked)
    w = jnp.concatenate(w_rows, axis=0)
    eidx_ref[...] = jnp.concatenate(idx_rows, axis=0)
    wts_ref[...] = w / (jnp.sum(w, axis=0, keepdims=True) + 1e-20) * ROUTED_SCALE

    mb = msel.astype(BF16)
    a = lax.broadcasted_iota(I32, (tm, tm), 0)
    b = lax.broadcasted_iota(I32, (tm, tm), 1)
    before = _dot(mb, jnp.where(a < b, 1.0, 0.0).astype(BF16))
    pos = base_s[...] + before
    rank_ref[...] = jnp.concatenate(
        [jnp.sum(jnp.where(ei == ix, pos, 0.0), axis=0, keepdims=True) for ix in idx_rows], axis=0).astype(I32)
    base_s[...] = base_s[...] + _dot(mb, jnp.ones((tm, tm), BF16))
    cnt_ref[...] = base_s[:, :LANES].astype(I32)


def _router(x1, g2, sc2, sh2, wt_hi, wt_lo, bias, seq, tm=512):
    t, d = x1.shape
    ne = wt_hi.shape[0]
    tm = min(tm, seq)
    per = seq // tm
    bvec = pl.BlockSpec((1, 1, d), lambda i: (i // per, 0, 0))
    kt = lambda dt: jax.ShapeDtypeStruct((TOP_K, t), dt)
    kspec = pl.BlockSpec((TOP_K, tm), lambda i: (0, i))
    return pl.pallas_call(
        _router_kernel,
        out_shape=(kt(I32), kt(F32), kt(I32), jax.ShapeDtypeStruct((ne, LANES), I32)),
        grid=(t // tm,),
        in_specs=[pl.BlockSpec((tm, d), lambda i: (i, 0)),
                  pl.BlockSpec((1, d), lambda i: (0, 0)), bvec, bvec,
                  pl.BlockSpec((ne, d), lambda i: (0, 0)),
                  pl.BlockSpec((ne, d), lambda i: (0, 0)),
                  pl.BlockSpec((ne, tm), lambda i: (0, 0))],
        out_specs=(kspec, kspec, kspec, pl.BlockSpec((ne, LANES), lambda i: (0, 0))),
        scratch_shapes=[pltpu.VMEM((ne, tm), F32)],
        compiler_params=_cparams("arbitrary"),
        name="router",
    )(x1, g2, sc2, sh2, wt_hi, wt_lo, jnp.broadcast_to(bias.reshape(ne, 1), (ne, tm)))


def _rows_kernel(eidx_ref, rank_ref, pstart_ref, rows_ref):
    ne, tm = pstart_ref.shape
    ei = lax.broadcasted_iota(I32, (ne, tm), 0)
    ps = pstart_ref[...]
    eidx = eidx_ref[...]
    rows_ref[...] = rank_ref[...] + jnp.concatenate(
        [jnp.sum(jnp.where(ei == eidx[k:k + 1, :], ps, 0), axis=0, keepdims=True) for k in range(TOP_K)], axis=0)


def _pair_rows(eidx, rank, pstart, tm=512):
    t = eidx.shape[1]
    ne = pstart.shape[0]
    tm = min(tm, t)
    kspec = pl.BlockSpec((TOP_K, tm), lambda i: (0, i))
    return pl.pallas_call(
        _rows_kernel,
        out_shape=jax.ShapeDtypeStruct((TOP_K, t), I32),
        grid=(t // tm,),
        in_specs=[kspec, kspec, pl.BlockSpec((ne, tm), lambda i: (0, 0))],
        out_specs=kspec,
        compiler_params=_cparams("arbitrary"),
        name="pair_rows",
    )(eidx, rank, jnp.broadcast_to(pstart.reshape(ne, 1), (ne, tm)))


def _tile(ref, r, n=1):
    return ref.at[pl.ds(pl.multiple_of(r * SUBLANES, SUBLANES), n * SUBLANES), :]


def _token_copy_all(src_of, dst_of, sem, rows_ref, tm):
    def issue(t, carry):
        for k in range(TOP_K):
            row = rows_ref[k, t]
            pltpu.make_async_copy(src_of(k, t, row), dst_of(k, t, row), sem).start(priority=k % 2)
        return carry
    lax.fori_loop(0, tm, issue, 0)


def _swiglu_packed(h_ref, wg_ref, wu_ref, wd_ref):
    lo, hi = _unpack_halves(_load_token_tiles(h_ref))
    lo, hi = lo.astype(BF16), hi.astype(BF16)
    n = lo.shape[1]
    g = _dot(lo, wg_ref[:n, :]) + _dot(hi, wg_ref[n:, :])
    u = _dot(lo, wu_ref[:n, :]) + _dot(hi, wu_ref[n:, :])
    return _dot((g * jax.nn.sigmoid(g) * u).astype(BF16), wd_ref[...])


def _dispatch_kernel(pstart_ref, cnt_ref, nused_ref, h_ref, rows_ref, wg_ref, wu_ref, wd_ref,
                     xs_ref, shared_ref, zbuf, sem, zsem):
    i = pl.program_id(0)
    tm = h_ref.shape[0] // SUBLANES
    bm = zbuf.shape[0] // SUBLANES
    ne = cnt_ref.shape[0]
    nblk = xs_ref.shape[0] // (bm * SUBLANES)

    def zero_fill(act):
        def per_expert(e, carry):
            c = cnt_ref[e]
            pad = lax.rem(bm - lax.rem(c, bm), bm)
            off = pstart_ref[e] + c
            s = bm // 2
            while s >= 1:
                @pl.when((pad & s) != 0)
                def _(s=s, off=off):
                    act(pltpu.make_async_copy(_tile(zbuf, 0, s), _tile(xs_ref, off, s), zsem))
                off = off + (pad & s)
                s //= 2
            return carry
        lax.fori_loop(0, ne, per_expert, 0)

        def per_block(j, carry):
            act(pltpu.make_async_copy(zbuf, _tile(xs_ref, j * bm, bm), zsem))
            return carry
        lax.fori_loop(nused_ref[0], nblk, per_block, 0)

    @pl.when(i == 0)
    def _():
        zbuf[...] = jnp.zeros_like(zbuf)
        zero_fill(lambda cp: cp.start())

    _token_copy_all(lambda k, t, row: _tile(h_ref, t), lambda k, t, row: _tile(xs_ref, row), sem, rows_ref, tm)
    shared_ref[...] = _swiglu_packed(h_ref, wg_ref, wu_ref, wd_ref).astype(BF16)
    for _ in range(TOP_K):
        pltpu.make_async_copy(h_ref, _tile(xs_ref, 0, tm), sem).wait()

    @pl.when(i == 0)
    def _():
        zero_fill(lambda cp: cp.wait())


def _dispatch(h2p, rows, pstart, counts, n_used, sh_wg, sh_wu, sh_wd, n_rows, bm, tm=256):
    t = h2p.shape[0] // SUBLANES
    d, f = sh_wg.shape
    tm = min(tm, t)
    const = lambda shp: pl.BlockSpec(shp, lambda i, *_: (0, 0), pipeline_mode=pl.Buffered(1))
    return pl.pallas_call(
        _dispatch_kernel,
        out_shape=(jax.ShapeDtypeStruct((n_rows * SUBLANES, LANES), U32), jax.ShapeDtypeStruct((t, d), BF16)),
        grid_spec=pltpu.PrefetchScalarGridSpec(
            num_scalar_prefetch=3,
            grid=(t // tm,),
            in_specs=[pl.BlockSpec((tm * SUBLANES, LANES), lambda i, *_: (i, 0)),
                      pl.BlockSpec((TOP_K, tm), lambda i, *_: (0, i), memory_space=pltpu.SMEM),
                      const((d, f)), const((d, f)), const((f, d))],
            out_specs=(pl.BlockSpec(memory_space=pl.ANY), pl.BlockSpec((tm, d), lambda i, *_: (i, 0))),
            scratch_shapes=[pltpu.VMEM((bm * SUBLANES, LANES), U32), pltpu.SemaphoreType.DMA(()),
                            pltpu.SemaphoreType.DMA(())]),
        compiler_params=_cparams("arbitrary"),
        name="dispatch",
    )(pstart, counts, n_used, h2p, rows, sh_wg, sh_wu, sh_wd)


def _expert_kernel(be_ref, first_ref, nxt_ref, slot_ref, nu_ref, xs_ref, wg_hbm, wu_hbm, wd_hbm, y_ref,
                   wg_buf, wu_buf, wd_buf, sems):
    j = pl.program_id(0)
    nu = nu_ref[0]

    def weight_copies(e, s):
        return [pltpu.make_async_copy(hbm.at[e], buf.at[s], sems.at[s, i])
                for i, (hbm, buf) in enumerate(((wg_hbm, wg_buf), (wu_hbm, wu_buf), (wd_hbm, wd_buf)))]

    @pl.when((j == 0) & (nu > 0))
    def _():
        for cp in weight_copies(be_ref[0], slot_ref[0]):
            cp.start()

    @pl.when((j < nu) & (first_ref[j] == 1))
    def _():
        for cp in weight_copies(be_ref[j], slot_ref[j]):
            cp.wait()

        @pl.when(nxt_ref[j] >= 0)
        def _():
            for cp in weight_copies(nxt_ref[j], 1 - slot_ref[j]):
                cp.start()

    @pl.when(j < nu)
    def _():
        s = slot_ref[j]
        lo, hi = _unpack_halves(_load_token_tiles(xs_ref))
        lo, hi = lo.astype(BF16), hi.astype(BF16)
        n = lo.shape[1]
        wg = wg_buf[s].astype(BF16)
        wu = wu_buf[s].astype(BF16)
        g = _dot(lo, wg[:n]) + _dot(hi, wg[n:])
        u = _dot(lo, wu[:n]) + _dot(hi, wu[n:])
        a = (g * jax.nn.sigmoid(g) * u).astype(BF16)
        _store_token_tiles(y_ref, _pack_halves(_dot(a, wd_buf[s].astype(BF16))))

    @pl.when(j >= nu)
    def _():
        y_ref[...] = jnp.zeros_like(y_ref)


def _experts(xs, block_e, block_first, block_next, block_slot, n_used, w_gate, w_up, w_down, bm):
    n_rows = xs.shape[0] // SUBLANES
    ne, d, f = w_gate.shape
    nblk = n_rows // bm
    blk = (bm * SUBLANES, LANES)
    return pl.pallas_call(
        _expert_kernel,
        out_shape=jax.ShapeDtypeStruct(xs.shape, U32),
        grid_spec=pltpu.PrefetchScalarGridSpec(
            num_scalar_prefetch=5,
            grid=(nblk,),
            in_specs=[pl.BlockSpec(blk, lambda j, be, fi, nx, sl, nu: (jnp.minimum(j, nu[0] - 1), 0)),
                      pl.BlockSpec(memory_space=pl.ANY),
                      pl.BlockSpec(memory_space=pl.ANY),
                      pl.BlockSpec(memory_space=pl.ANY)],
            out_specs=pl.BlockSpec(blk, lambda j, *_: (j, 0)),
            scratch_shapes=[pltpu.VMEM((2, d, f), F32), pltpu.VMEM((2, d, f), F32), pltpu.VMEM((2, f, d), F32),
                            pltpu.SemaphoreType.DMA((2, 3))]),
        compiler_params=_cparams("arbitrary"),
        name="experts",
    )(block_e, block_first, block_next, block_slot, n_used, xs, w_gate, w_up, w_down)


def _final_kernel(n_first, x1_ref, sh_ref, gt_ref, wts_ref, fg_ref, rows_ref, rows_nx_ref, y_ref,
                  oa_ref, ob_ref, ybuf, sems):
    i = pl.program_id(0)
    tm = x1_ref.shape[0]
    n = x1_ref.shape[1] // 2
    slot = lax.rem(i, 2)

    def gather(r_ref, s):
        _token_copy_all(lambda k, t, row: _tile(y_ref, row), lambda k, t, row: _tile(ybuf.at[s, k], t),
                        sems.at[s], r_ref, tm)

    pl.when(i == 0)(lambda: gather(rows_ref, 0))
    pl.when(i + 1 < pl.num_programs(0))(lambda: gather(rows_nx_ref, 1 - slot))

    for k in range(TOP_K):
        pltpu.make_async_copy(_tile(y_ref, 0, tm), ybuf.at[slot, k], sems.at[slot]).wait()
    shared = sh_ref[...].astype(F32)
    acc_lo, acc_hi = shared[:, :n], shared[:, n:]
    wts = wts_ref[...].T
    for k in range(TOP_K):
        ylo, yhi = _unpack_halves(_load_token_tiles(ybuf.at[slot, k]))
        wk = wts[:, k:k + 1]
        acc_lo = acc_lo + wk * ylo
        acc_hi = acc_hi + wk * yhi
    gt = gt_ref[0]
    x_lo = x1_ref[:, :n] + gt[:, :n] * acc_lo
    x_hi = x1_ref[:, n:] + gt[:, n:] * acc_hi
    ms = (jnp.sum(x_lo * x_lo, axis=-1, keepdims=True) + jnp.sum(x_hi * x_hi, axis=-1, keepdims=True)) / (2 * n)
    r = lax.rsqrt(ms + RMS_EPS)

    def write(o_ref):
        o_ref[:, :n] = x_lo * r * fg_ref[:, :n]
        o_ref[:, n:] = x_hi * r * fg_ref[:, n:]

    pl.when(i < n_first)(lambda: write(oa_ref))
    pl.when(i >= n_first)(lambda: write(ob_ref))


def _final(x1, shared, gt2, wts, final_g, rows, y, rows_first, seq, tm=256):
    t, d = x1.shape
    tm = min(tm, seq)
    per = seq // tm
    nt = t // tm
    n_first = rows_first // tm
    out_a, out_b = _two_trunk_specs((tm, d), n_first)
    return pl.pallas_call(
        functools.partial(_final_kernel, n_first),
        out_shape=(jax.ShapeDtypeStruct((rows_first, d), F32), jax.ShapeDtypeStruct((t - rows_first, d), F32)),
        grid=(nt,),
        in_specs=[pl.BlockSpec((tm, d), lambda i: (i, 0)),
                  pl.BlockSpec((tm, d), lambda i: (i, 0)),
                  pl.BlockSpec((1, 1, d), lambda i: (i // per, 0, 0)),
                  pl.BlockSpec((TOP_K, tm), lambda i: (0, i)),
                  pl.BlockSpec((1, d), lambda i: (0, 0)),
                  pl.BlockSpec((TOP_K, tm), lambda i: (0, i), memory_space=pltpu.SMEM),
                  pl.BlockSpec((TOP_K, tm), lambda i: (0, jnp.minimum(i + 1, nt - 1)), memory_space=pltpu.SMEM),
                  pl.BlockSpec(memory_space=pl.ANY)],
        out_specs=(out_a, out_b),
        scratch_shapes=[pltpu.VMEM((2, TOP_K, tm * SUBLANES, LANES), U32), pltpu.SemaphoreType.DMA((2,))],
        compiler_params=_cparams("arbitrary"),
        name="final",
    )(x1, shared, gt2, wts, final_g, rows, rows, y)


EXPERT_ROWS = 256


def _prep_moe(w):
    rw = w["router_w"][0]
    d = rw.shape[0]
    wt = rw.T
    wt_hi = wt.astype(BF16)
    return dict(
        wt_hi=wt_hi, wt_lo=(wt - wt_hi.astype(F32)).astype(BF16), bias=w["router_bias"][0],
        w_gate=w["exp_w_gate"][0], w_up=w["exp_w_up"][0], w_down=w["exp_w_down"][0],
        sh_wg=w["sh_w_gate"][0].astype(BF16), sh_wu=w["sh_w_up"][0].astype(BF16),
        sh_wd=w["sh_w_down"][0].astype(BF16), final_g=w["final_g"].reshape(1, d),
    )


def _moe_stage(x1, h2p, mods, pm, pw, seq, rows_first):
    _, _, _, sh2, sc2, gt2 = mods
    t = x1.shape[0]
    ne = pw["wt_hi"].shape[0]
    bm = EXPERT_ROWS
    eidx, wts, rank, cnt = _router(x1, pm["norm_ffn_g"], sc2, sh2, pw["wt_hi"], pw["wt_lo"], pw["bias"], seq)
    counts = cnt[:, 0]
    pcounts = (counts + bm - 1) // bm * bm
    pends = jnp.cumsum(pcounts)
    pstart = (pends - pcounts).astype(I32)
    nblk = (t * TOP_K + ne * (bm - 1) + bm - 1) // bm
    n_used = (pends[-1:] // bm).astype(I32)
    jb = jnp.arange(nblk, dtype=I32)
    be = jnp.minimum(jnp.sum((pends[None, :] <= (jb * bm)[:, None]).astype(I32), axis=1), ne - 1)
    ids = jnp.arange(ne, dtype=I32)
    onehot = be[:, None] == ids[None, :]
    look = lambda table: jnp.sum(jnp.where(onehot, table[None, :], 0), axis=1).astype(I32)
    first = (jb * bm == look(pstart)).astype(I32)
    live = counts > 0
    ordinal = jnp.cumsum(live.astype(I32)) - 1
    after = lax.cummin(jnp.where(live, ids, ne), reverse=True)
    next_live = jnp.concatenate([after[1:], jnp.full((1,), ne, I32)])
    nxt = look(jnp.where(next_live < ne, next_live, -1))
    slot = look(ordinal) % 2
    rows = _pair_rows(eidx, rank, pstart)
    xs, shared = _dispatch(h2p, rows, pstart, counts, n_used, pw["sh_wg"], pw["sh_wu"], pw["sh_wd"],
                           nblk * bm, bm)
    y = _experts(xs, be, first, nxt, slot, n_used, pw["w_gate"], pw["w_up"], pw["w_down"], bm)
    return _final(x1, shared, gt2, wts, pw["final_g"], rows, y, rows_first, seq)


def kernel(x_prompt, x_sample, c_prompt, c_sample, ada_w, ada_b, norm_mix_g, norm_ffn_g, w_in, sgu_ln_g,
           sgu_ln_b, sgu_ws, sgu_b, dn_conv_w, dn_a_log, dn_dt_bias, dn_norm_g, w_branch, w_out, router_w,
           router_bias, exp_w_gate, exp_w_up, exp_w_down, sh_w_gate, sh_w_up, sh_w_down, final_g):
    w = dict(ada_w=ada_w, ada_b=ada_b, norm_mix_g=norm_mix_g, norm_ffn_g=norm_ffn_g, w_in=w_in,
             sgu_ln_g=sgu_ln_g, sgu_ln_b=sgu_ln_b, sgu_ws=sgu_ws, sgu_b=sgu_b, dn_conv_w=dn_conv_w,
             dn_a_log=dn_a_log, dn_dt_bias=dn_dt_bias, dn_norm_g=dn_norm_g, w_branch=w_branch, w_out=w_out,
             router_w=router_w, router_bias=router_bias, exp_w_gate=exp_w_gate, exp_w_up=exp_w_up,
             exp_w_down=exp_w_down, sh_w_gate=sh_w_gate, sh_w_up=sh_w_up, sh_w_down=sh_w_down, final_g=final_g)
    assert w_in.shape[0] == 1, "one layer"
    bp, seq, d = x_prompt.shape
    bs = x_sample.shape[0]
    assert x_sample.shape[1] == seq
    nb = bp + bs
    xa = x_prompt.reshape(bp * seq, d)
    xb = x_sample.reshape(bs * seq, d)
    c = jnp.concatenate([c_prompt, c_sample], axis=0)
    npad = -nb % 8
    mod = _ada(jnp.pad(c, ((0, npad), (0, 0))), ada_w[0], ada_b[0])[:nb]
    mods = [m.reshape(nb, 1, d) for m in jnp.split(mod, 6, axis=-1)]
    pm = _prep_mixer(w, nb, seq)
    pw = _prep_moe(w)
    x1, h2p = _token_mixer_stage(xa, xb, mods, pm)
    ya, yb = _moe_stage(x1, h2p, mods, pm, pw, seq, bp * seq)
    return (ya.reshape(bp, seq, d), yb.reshape(bs, seq, d))
```

```python
import functools

import jax
import jax.numpy as jnp
import numpy as np
from jax import lax
from jax.experimental import pallas as pl
from jax.experimental.pallas import tpu as pltpu

F32 = jnp.float32
BF16 = jnp.bfloat16
I32 = jnp.int32
U32 = jnp.uint32

RMS_EPS = 1e-6
LN_EPS = 1e-5
L2_EPS = 1e-6

SGU_GROUPS = 8
SGU_CHUNK = 128
DN_HEADS = 8
DN_HEAD_DIM = 128
DN_CHUNK = 256
DN_PREP_UNROLL = 2
CONV_WIDTH = 5
TOP_K = 8
N_GROUPS = 8
TOPK_GROUPS = 4
ROUTED_SCALE = 2.5

LANES = 128
SUBLANES = 8
VMEM_LIMIT = 56 * 1024 * 1024


def _cparams(*sem):
    return pltpu.CompilerParams(dimension_semantics=sem, vmem_limit_bytes=VMEM_LIMIT)


def _split3(x):
    hi = x.astype(BF16)
    r = x - hi.astype(F32)
    mid = r.astype(BF16)
    lo = (r - mid.astype(F32)).astype(BF16)
    return hi, mid, lo


def _dot(a, b):
    return jnp.dot(a, b, preferred_element_type=F32)


def _dot_nt(a, b):
    return lax.dot_general(a, b, (((1,), (1,)), ((), ())), preferred_element_type=F32)


def _dot_tn(a, b):
    return lax.dot_general(a, b, (((0,), (0,)), ((), ())), preferred_element_type=F32)


def _rms(x, g):
    return x * lax.rsqrt(jnp.mean(x * x, axis=-1, keepdims=True) + RMS_EPS) * g


def _pack_halves(x):
    n = x.shape[1] // 2
    return pltpu.pack_elementwise([x[:, :n], x[:, n:]], packed_dtype=BF16)


def _unpack_halves(w):
    lo = pltpu.unpack_elementwise(w, index=0, packed_dtype=BF16, unpacked_dtype=F32)
    hi = pltpu.unpack_elementwise(w, index=1, packed_dtype=BF16, unpacked_dtype=F32)
    return lo, hi


def _store_token_tiles(ref, x):
    m = x.shape[0]
    for s in range(SUBLANES):
        ref[pl.ds(s, m, stride=SUBLANES), :] = x[:, s * LANES:(s + 1) * LANES]


def _load_token_tiles(ref):
    m = ref.shape[0] // SUBLANES
    return jnp.concatenate([ref[pl.ds(s, m, stride=SUBLANES), :] for s in range(SUBLANES)], axis=1)


def _ada_kernel(c_ref, w_ref, b_ref, o_ref):
    c = c_ref[...]
    a = c * jax.nn.sigmoid(c)
    a_hi, a_mid, _ = _split3(a)
    w = w_ref[...]
    w_hi = w.astype(BF16)
    w_lo = (w - w_hi.astype(F32)).astype(BF16)
    o_ref[...] = _dot(a_hi, w_hi) + _dot(a_mid, w_hi) + _dot(a_hi, w_lo) + b_ref[...]


def _ada(c, ada_w, ada_b, tn=1024):
    nb, d = c.shape
    n = ada_w.shape[1]
    return pl.pallas_call(
        _ada_kernel,
        out_shape=jax.ShapeDtypeStruct((nb, n), F32),
        grid=(n // tn,),
        in_specs=[pl.BlockSpec((nb, d), lambda j: (0, 0)),
                  pl.BlockSpec((d, tn), lambda j: (0, j)),
                  pl.BlockSpec((1, tn), lambda j: (0, j))],
        out_specs=pl.BlockSpec((nb, tn), lambda j: (0, j)),
        compiler_params=_cparams("arbitrary"),
        name="ada",
    )(c, ada_w, ada_b.reshape(1, n))


def _two_trunk_specs(block, n_first):
    first = pl.BlockSpec(block, lambda i, *_: (jnp.minimum(i, n_first - 1), 0))
    second = pl.BlockSpec(block, lambda i, *_: (jnp.maximum(i - n_first, 0), 0))
    return first, second


def _inproj_kernel(n_first, xa_ref, xb_ref, g_ref, sc_ref, sh_ref, w_ref, wab_ref, z_ref, ab_ref, h_s):
    i = pl.program_id(0)
    j = pl.program_id(1)

    def prologue(x_ref):
        h = _rms(x_ref[...], g_ref[...]) * (1.0 + sc_ref[0]) + sh_ref[0]
        h_s[...] = h.astype(BF16)
        ab_ref[...] = _dot(h_s[...], wab_ref[...])

    pl.when((j == 0) & (i < n_first))(lambda: prologue(xa_ref))
    pl.when((j == 0) & (i >= n_first))(lambda: prologue(xb_ref))
    z_ref[...] = _dot(h_s[...], w_ref[...]).astype(BF16)


def _inproj(xa, xb, g, sc, sh, w_main, w_ab, seq, tm=512, tn=2048):
    d = xa.shape[1]
    t = xa.shape[0] + xb.shape[0]
    n = w_main.shape[1]
    tm = min(tm, seq)
    per = seq // tm
    n_first = xa.shape[0] // tm
    spec_a, spec_b = _two_trunk_specs((tm, d), n_first)
    return pl.pallas_call(
        functools.partial(_inproj_kernel, n_first),
        out_shape=(jax.ShapeDtypeStruct((t, n), BF16), jax.ShapeDtypeStruct((t, LANES), F32)),
        grid=(t // tm, n // tn),
        in_specs=[spec_a, spec_b,
                  pl.BlockSpec((1, d), lambda i, j: (0, 0)),
                  pl.BlockSpec((1, 1, d), lambda i, j: (i // per, 0, 0)),
                  pl.BlockSpec((1, 1, d), lambda i, j: (i // per, 0, 0)),
                  pl.BlockSpec((d, tn), lambda i, j: (0, j)),
                  pl.BlockSpec((d, LANES), lambda i, j: (0, 0))],
        out_specs=(pl.BlockSpec((tm, tn), lambda i, j: (i, j)),
                   pl.BlockSpec((tm, LANES), lambda i, j: (i, 0))),
        scratch_shapes=[pltpu.VMEM((tm, d), BF16)],
        compiler_params=_cparams("arbitrary", "arbitrary"),
        name="inproj",
    )(xa, xb, g, sc, sh, w_main, w_ab)


def _sgu_kernel(u_ref, v_ref, g_ref, b_ref, ws_ref, sb_ref, o_ref):
    v = jax.nn.gelu(v_ref[...].astype(F32))
    mu = jnp.mean(v, axis=-1, keepdims=True)
    vc = v - mu
    var = jnp.mean(vc * vc, axis=-1, keepdims=True)
    vn = (vc * lax.rsqrt(var + LN_EPS) * g_ref[...] + b_ref[...]).astype(BF16)
    tm = vn.shape[0]
    for c in range(tm // SGU_CHUNK):
        r = slice(c * SGU_CHUNK, (c + 1) * SGU_CHUNK)
        for gi in range(SGU_GROUPS):
            l = slice(gi * LANES, (gi + 1) * LANES)
            mixed = _dot(ws_ref[gi], vn[r, l]) + sb_ref[gi]
            o_ref[r, l] = (jax.nn.gelu(u_ref[r, l].astype(F32)) * mixed).astype(BF16)


def _sgu(z, ln_g, ln_b, ws, sb_b, tm=512):
    t = z.shape[0]
    w = SGU_GROUPS * LANES
    tm = np.gcd(tm, t)
    return pl.pallas_call(
        _sgu_kernel,
        out_shape=jax.ShapeDtypeStruct((t, w), BF16),
        grid=(t // tm,),
        in_specs=[pl.BlockSpec((tm, w), lambda i: (i, 0)),
                  pl.BlockSpec((tm, w), lambda i: (i, 1)),
                  pl.BlockSpec((1, w), lambda i: (0, 0)),
                  pl.BlockSpec((1, w), lambda i: (0, 0)),
                  pl.BlockSpec((SGU_GROUPS, SGU_CHUNK, SGU_CHUNK), lambda i: (0, 0, 0)),
                  pl.BlockSpec((SGU_GROUPS, SGU_CHUNK, LANES), lambda i: (0, 0, 0))],
        out_specs=pl.BlockSpec((tm, w), lambda i: (i, 0)),
        compiler_params=_cparams("arbitrary"),
        name="sgu",
    )(z, z, ln_g, ln_b, ws, sb_b)


def _gate_kernel(ab_ref, alog_ref, dt_ref, gcol_ref, grow_ref):
    ab = ab_ref[...]
    tr = ab.shape[0]
    lane = lax.broadcasted_iota(I32, ab.shape, 1)
    z = ab + dt_ref[...]
    softplus = jnp.maximum(z, 0.0) + jnp.log1p(jnp.exp(-jnp.abs(z)))
    g = -jnp.exp(alog_ref[...]) * softplus
    beta = jax.nn.sigmoid(ab)
    nh = DN_HEADS
    g = jnp.where(lane < 2 * nh, g, 0.0)
    i = lax.broadcasted_iota(I32, (tr, tr), 0)
    j = lax.broadcasted_iota(I32, (tr, tr), 1)
    same = (i // DN_CHUNK) == (j // DN_CHUNK)
    lower = jnp.where(same & (j <= i), 1.0, 0.0).astype(BF16)
    upper = jnp.where(same & (j >= i), 1.0, 0.0).astype(BF16)
    g_hi, g_mid, g_lo = _split3(g)
    pre = _dot(lower, g_hi) + _dot(lower, g_mid) + _dot(lower, g_lo)
    suf = _dot(upper, g_hi) + _dot(upper, g_mid) + _dot(upper, g_lo)
    out = jnp.where(lane < nh, pre, jnp.where(lane < 2 * nh, suf, jnp.where(lane < 4 * nh, beta, 0.0)))
    gcol_ref[...] = out
    grow_ref[0] = out.T[:2 * nh, :]


def _gates(ab, alog_row, dt_row, nb, seq, tr=256):
    t = ab.shape[0]
    per = seq // tr
    return pl.pallas_call(
        _gate_kernel,
        out_shape=(jax.ShapeDtypeStruct((t, LANES), F32),
                   jax.ShapeDtypeStruct((nb, 2 * DN_HEADS, seq), F32)),
        grid=(t // tr,),
        in_specs=[pl.BlockSpec((tr, LANES), lambda i: (i, 0)),
                  pl.BlockSpec((1, LANES), lambda i: (0, 0)),
                  pl.BlockSpec((1, LANES), lambda i: (0, 0))],
        out_specs=(pl.BlockSpec((tr, LANES), lambda i: (i, 0)),
                   pl.BlockSpec((1, 2 * DN_HEADS, tr), lambda i: (i // per, 0, i % per))),
        compiler_params=_cparams("arbitrary"),
        name="gates",
    )(ab, alog_row, dt_row)


def _conv_silu(x_ref, w_ref):
    x = x_ref[...].astype(F32)
    s = x.shape[0]
    row = lax.broadcasted_iota(I32, x.shape, 0)
    pad = (CONV_WIDTH - 1) // 2
    acc = x * w_ref[pad:pad + 1, :]
    for j in range(CONV_WIDTH):
        d = j - pad
        if d == 0:
            continue
        xs = pltpu.roll(x, (-d) % s, 0)
        ok = (row + d >= 0) & (row + d < s)
        acc = acc + jnp.where(ok, xs, 0.0) * w_ref[j:j + 1, :]
    return acc * jax.nn.sigmoid(acc)


def _l2n(x):
    return x * lax.rsqrt(jnp.sum(x * x, axis=-1, keepdims=True) + L2_EPS)


def _unit_tri_inverses(mats, i, j):
    c = mats[0].shape[0]
    x = i ^ j
    eye = jnp.where(i == j, 1.0, 0.0)
    tinv = [eye - jnp.where(x < 2, a, 0.0) for a in mats]
    s = 2
    while s < c:
        level = (x >= s) & (x < 2 * s)
        tb = [t.astype(BF16) for t in tinv]
        m = [_dot(t, jnp.where(level, a, 0.0).astype(BF16)).astype(BF16) for t, a in zip(tb, mats)]
        tinv = [t - _dot(mm, b) for t, mm, b in zip(tinv, m, tb)]
        s *= 2
    return tinv


def _dn_prep(items):
    c, hd = items[0][1].shape
    i = lax.broadcasted_iota(I32, (c, c), 0)
    j = lax.broadcasted_iota(I32, (c, c), 1)
    decs, kbs, kbfs = [], [], []
    for forward, q, k, v, gam, grow, beta in items:
        incl = (i >= j) if forward else (i <= j)
        gi = jnp.concatenate([gam] * (c // LANES), axis=1)
        decs.append(jnp.where(incl, jnp.exp(gi - grow), 0.0))
        kbs.append(k * beta)
        kbfs.append(k.astype(BF16))
    kk = [_dot_nt(kb.astype(BF16), kbf) for kb, kbf in zip(kbs, kbfs)]
    mats = [jnp.where((i > j) if it[0] else (i < j), m * dec, 0.0) for it, m, dec in zip(items, kk, decs)]
    tinvs = _unit_tri_inverses(mats, i, j)
    egs = [jnp.exp(it[4]) for it in items]
    uws = [_dot(t.astype(BF16), jnp.concatenate([it[3] * it[6], kb * eg], axis=1).astype(BF16))
           for t, it, kb, eg in zip(tinvs, items, kbs, egs)]
    attns = [_dot_nt(it[1].astype(BF16), kbf) * dec for it, kbf, dec in zip(items, kbfs, decs)]
    outs = []
    for (forward, q, k, v, gam, grow, beta), uw, eg, attn in zip(items, uws, egs, attns):
        glast = gam[c - 1:c, :] if forward else gam[0:1, :]
        kd = k * jnp.exp(glast - gam)
        outs.append((uw[:, :hd], uw[:, hd:].astype(BF16), (q * eg).astype(BF16), kd.T.astype(BF16),
                     attn.astype(BF16)))
    return outs


def _dn_kernel(q_ref, k_ref, v_ref, og_ref, wq_ref, wk_ref, wv_ref, gcol_ref, grow_ref, ng_ref, o_ref,
               q_s, k_s, v_s, gam_s, beta_s, u_s, w_s, qd_s, kdt_s, attn_s, o_s):
    h = pl.program_id(1)
    nh = DN_HEADS
    hd = DN_HEAD_DIM
    q_s[...] = _l2n(_conv_silu(q_ref, wq_ref)) * (hd ** -0.5)
    k_s[...] = _l2n(_conv_silu(k_ref, wk_ref))
    v_s[...] = _conv_silu(v_ref, wv_ref)

    g_hi, g_mid, g_lo = _split3(gcol_ref[...])
    rr = lax.broadcasted_iota(I32, (LANES, 2 * LANES), 0)
    cc = lax.broadcasted_iota(I32, (LANES, 2 * LANES), 1)
    sel = lambda off: jnp.where(rr == h + off + jnp.where(cc < LANES, 0, nh), 1.0, 0.0).astype(BF16)
    sel_g = sel(0)
    gam2 = (_dot(jnp.concatenate([g_hi, g_mid], axis=1), jnp.concatenate([sel_g, sel_g], axis=0))
            + _dot(g_lo, sel_g))
    beta2 = _dot(g_hi, sel(2 * nh))
    for d in range(2):
        gam_s[d] = gam2[:, d * LANES:(d + 1) * LANES]
        beta_s[d] = beta2[:, d * LANES:(d + 1) * LANES]

    s = q_s.shape[0]
    c = DN_CHUNK
    n = s // c

    def prep(it, carry):
        where, items = [], []
        for sub in range(DN_PREP_UNROLL):
            rows = pl.ds(pl.multiple_of((it * DN_PREP_UNROLL + sub) * c, c), c)
            for d in range(2):
                where.append((d, rows))
                items.append((d == 0, q_s[rows, :], k_s[rows, :], v_s[rows, :], gam_s[d, rows, :],
                              grow_ref[0, pl.ds(d * nh + h, 1), rows], beta_s[d, rows, :]))
        for (d, rows), (u, w, qd, kdt, attn) in zip(where, _dn_prep(items)):
            u_s[d, rows, :] = u
            w_s[d, rows, :] = w
            qd_s[d, rows, :] = qd
            kdt_s[d, :, rows] = kdt
            attn_s[d, rows, :] = attn
        return carry

    lax.fori_loop(0, n // DN_PREP_UNROLL, prep, 0)

    def scan(it, states):
        r0s = [pl.multiple_of((it if d == 0 else n - 1 - it) * c, c) for d in range(2)]
        rows = [pl.ds(r0, c) for r0 in r0s]
        rs = [_dot(jnp.concatenate([w_s[d, rows[d], :], qd_s[d, rows[d], :]], axis=0), states[d].astype(BF16))
              for d in range(2)]
        vbs = [(u_s[d, rows[d], :] - rs[d][:c]).astype(BF16) for d in range(2)]
        for d in range(2):
            o_s[d, rows[d], :] = rs[d][c:] + _dot(attn_s[d, rows[d], :], vbs[d])
        new = []
        for d in range(2):
            glast = gam_s[d, pl.ds(r0s[d] + (c - 1 if d == 0 else 0), 1), :]
            new.append(states[d] * jnp.exp(glast) + _dot(kdt_s[d, :, rows[d]], vbs[d]))
        return tuple(new)

    zero = jnp.zeros((hd, hd), F32)
    lax.fori_loop(0, n, scan, (zero, zero))

    o = o_s[0] + o_s[1]
    og = og_ref[...].astype(F32)
    o_ref[...] = (_rms(o, ng_ref[...]) * (og * jax.nn.sigmoid(og))).astype(BF16)


def _deltanet(z, conv_w, gcol, grow, norm_g, nb, seq, qkv_blk, og_blk):
    t = z.shape[0]
    nh, hd, c = DN_HEADS, DN_HEAD_DIM, DN_CHUNK
    assert seq % (c * DN_PREP_UNROLL) == 0
    zspec = lambda off: pl.BlockSpec((seq, hd), lambda b, h: (b, off + h))
    wspec = lambda off: pl.BlockSpec((CONV_WIDTH, hd), lambda b, h: (0, off + h))
    return pl.pallas_call(
        _dn_kernel,
        out_shape=jax.ShapeDtypeStruct((t, nh * hd), BF16),
        grid=(nb, nh),
        in_specs=[zspec(qkv_blk), zspec(qkv_blk + nh), zspec(qkv_blk + 2 * nh), zspec(og_blk),
                  wspec(0), wspec(nh), wspec(2 * nh),
                  pl.BlockSpec((seq, LANES), lambda b, h: (b, 0)),
                  pl.BlockSpec((1, 2 * nh, seq), lambda b, h: (b, 0, 0)),
                  pl.BlockSpec((1, hd), lambda b, h: (0, 0))],
        out_specs=pl.BlockSpec((seq, hd), lambda b, h: (b, h)),
        scratch_shapes=[pltpu.VMEM((seq, hd), F32)] * 3
                       + [pltpu.VMEM((2, seq, hd), F32)] * 3
                       + [pltpu.VMEM((2, seq, hd), BF16)] * 2
                       + [pltpu.VMEM((2, hd, seq), BF16), pltpu.VMEM((2, seq, c), BF16),
                          pltpu.VMEM((2, seq, hd), F32)],
        compiler_params=_cparams("arbitrary", "arbitrary"),
        name="deltanet",
    )(z, z, z, z, conv_w, conv_w, conv_w, gcol, grow, norm_g)


def _merge_kernel(n_first, a_ref, b_ref, ga_ref, gb_ref, xa_ref, xb_ref, gt_ref, wa_ref, wb_ref, wo_ref, g2_ref,
                  sc_ref, sh_ref, x1_ref, h2_ref):
    i = pl.program_id(0)
    ya = _dot(a_ref[...], wa_ref[...])
    yb = _dot(b_ref[...], wb_ref[...])
    m = jax.nn.sigmoid(ga_ref[...].astype(F32)) * ya + jax.nn.sigmoid(gb_ref[...].astype(F32)) * yb
    y = _dot(m.astype(BF16), wo_ref[...])

    def epilogue(x_ref):
        x1 = x_ref[...] + gt_ref[0] * y
        x1_ref[...] = x1
        h2 = _rms(x1, g2_ref[...]) * (1.0 + sc_ref[0]) + sh_ref[0]
        _store_token_tiles(h2_ref, _pack_halves(h2))

    pl.when(i < n_first)(lambda: epilogue(xa_ref))
    pl.when(i >= n_first)(lambda: epilogue(xb_ref))


def _merge(a_out, b_out, z, xa, xb, gt1, wa, wb, wo, g2, sc2, sh2, seq, gate_blk, tm=256):
    d = xa.shape[1]
    t = xa.shape[0] + xb.shape[0]
    wdt = a_out.shape[1]
    tm = min(tm, seq)
    per = seq // tm
    n_first = xa.shape[0] // tm
    spec_a, spec_b = _two_trunk_specs((tm, d), n_first)
    bvec = pl.BlockSpec((1, 1, d), lambda i: (i // per, 0, 0))
    const = lambda shp: pl.BlockSpec(shp, lambda i: (0, 0), pipeline_mode=pl.Buffered(1))
    return pl.pallas_call(
        functools.partial(_merge_kernel, n_first),
        out_shape=(jax.ShapeDtypeStruct((t, d), F32), jax.ShapeDtypeStruct((t * SUBLANES, LANES), U32)),
        grid=(t // tm,),
        in_specs=[pl.BlockSpec((tm, wdt), lambda i: (i, 0)),
                  pl.BlockSpec((tm, wdt), lambda i: (i, 0)),
                  pl.BlockSpec((tm, d), lambda i: (i, gate_blk)),
                  pl.BlockSpec((tm, d), lambda i: (i, gate_blk + 1)),
                  spec_a, spec_b,
                  bvec, const((wdt, d)), const((wdt, d)), const((d, d)),
                  pl.BlockSpec((1, d), lambda i: (0, 0)), bvec, bvec],
        out_specs=(pl.BlockSpec((tm, d), lambda i: (i, 0)),
                   pl.BlockSpec((tm * SUBLANES, LANES), lambda i: (i, 0))),
        compiler_params=_cparams("arbitrary"),
        name="merge",
    )(a_out, b_out, z, z, xa, xb, gt1, wa, wb, wo, g2, sc2, sh2)


def _prep_mixer(w, nb, seq):
    wdt = SGU_GROUPS * LANES
    nh, hd = DN_HEADS, DN_HEAD_DIM
    w_in = w["w_in"][0]
    d = w_in.shape[0]
    ab0 = 2 * wdt + 4 * nh * hd
    w_main = jnp.concatenate([w_in[:, :ab0], w_in[:, ab0 + 4 * nh:]], axis=1).astype(BF16)
    w_ab = jnp.pad(w_in[:, ab0:ab0 + 4 * nh], ((0, 0), (0, LANES - 4 * nh))).astype(BF16)
    pad_row = lambda v: jnp.pad(v.reshape(1, 2 * nh), ((0, 0), (0, LANES - 2 * nh)))
    sb = w["sgu_b"][0]
    wbr = w["w_branch"][0]
    return dict(
        nb=nb, seq=seq,
        norm_mix_g=w["norm_mix_g"][0].reshape(1, d), norm_ffn_g=w["norm_ffn_g"][0].reshape(1, d),
        w_main=w_main, w_ab=w_ab,
        sgu_ln_g=w["sgu_ln_g"][0].reshape(1, wdt), sgu_ln_b=w["sgu_ln_b"][0].reshape(1, wdt),
        sgu_ws=w["sgu_ws"][0].astype(BF16),
        sgu_sb=jnp.broadcast_to(sb[:, :, None], sb.shape + (LANES,)),
        alog_row=pad_row(w["dn_a_log"][0]), dt_row=pad_row(w["dn_dt_bias"][0]),
        dn_conv_w=w["dn_conv_w"][0], dn_norm_g=w["dn_norm_g"][0].reshape(1, hd),
        wa=wbr[:wdt].astype(BF16), wb=wbr[wdt:].astype(BF16), wo=w["w_out"][0].astype(BF16),
    )


def _token_mixer_stage(xa, xb, mods, p):
    nb, seq = p["nb"], p["seq"]
    d = xa.shape[1]
    sh1, sc1, gt1, sh2, sc2, _ = mods
    wdt = SGU_GROUPS * LANES
    nh = DN_HEADS
    z, ab = _inproj(xa, xb, p["norm_mix_g"], sc1, sh1, p["w_main"], p["w_ab"], seq)
    a_out = _sgu(z, p["sgu_ln_g"], p["sgu_ln_b"], p["sgu_ws"], p["sgu_sb"])
    gcol, grow = _gates(ab, p["alog_row"], p["dt_row"], nb, seq)
    qkv_blk = 2 * wdt // LANES
    b_out = _deltanet(z, p["dn_conv_w"], gcol, grow, p["dn_norm_g"], nb, seq,
                      qkv_blk=qkv_blk, og_blk=qkv_blk + 3 * nh)
    gate_blk = (2 * wdt + 4 * nh * DN_HEAD_DIM) // d
    return _merge(a_out, b_out, z, xa, xb, gt1, p["wa"], p["wb"], p["wo"], p["norm_ffn_g"], sc2, sh2, seq,
                  gate_blk)


def _first_argmax(vals, idx, n, axis):
    mx = jnp.max(vals, axis=axis, keepdims=True)
    ix = jnp.min(jnp.where(vals == mx, idx, n), axis=axis, keepdims=True)
    return mx, ix


def _router_kernel(x_ref, g_ref, sc_ref, sh_ref, whi_ref, wlo_ref, bias_ref,
                   eidx_ref, wts_ref, rank_ref, cnt_ref, base_s):
    i = pl.program_id(0)
    ne, tm = bias_ref.shape
    neg = -jnp.inf

    @pl.when(i == 0)
    def _():
        base_s[...] = jnp.zeros_like(base_s)

    h = _rms(x_ref[...], g_ref[...]) * (1.0 + sc_ref[0]) + sh_ref[0]
    h_hi, h_mid, _ = _split3(h)
    whi = whi_ref[...]
    logits = _dot_nt(whi, h_hi) + _dot_nt(wlo_ref[...], h_hi) + _dot_nt(whi, h_mid)
    scores = jax.nn.sigmoid(logits)
    sel = scores + bias_ref[...]

    per = ne // N_GROUPS
    sel3 = sel.reshape(N_GROUPS, per, tm)
    ri = lax.broadcasted_iota(I32, sel3.shape, 1)
    m1, i1 = _first_argmax(sel3, ri, per, 1)
    m2 = jnp.max(jnp.where(ri == i1, neg, sel3), axis=1, keepdims=True)
    grp = (m1 + m2).reshape(N_GROUPS, tm)

    gi = lax.broadcasted_iota(I32, grp.shape, 0)
    chosen = jnp.zeros(grp.shape, F32)
    for _ in range(TOPK_GROUPS):
        _, ix = _first_argmax(grp, gi, N_GROUPS, 0)
        hit = gi == ix
        chosen = jnp.where(hit, 1.0, chosen)
        grp = jnp.where(hit, neg, grp)
    masked = jnp.where(chosen.reshape(N_GROUPS, 1, tm) > 0.0, sel3, neg).reshape(ne, tm)

    ei = lax.broadcasted_iota(I32, (ne, tm), 0)
    msel = jnp.zeros((ne, tm), F32)
    idx_rows, w_rows = [], []
    for _ in range(TOP_K):
        _, ix = _first_argmax(masked, ei, ne, 0)
        hit = ei == ix
        w_rows.append(jnp.sum(jnp.where(hit, scores, 0.0), axis=0, keepdims=True))
        idx_rows.append(ix)
        msel = jnp.where(hit, 1.0, msel)
        masked = jnp.where(hit, neg, masked)
    w = jnp.concatenate(w_rows, axis=0)
    eidx_ref[...] = jnp.concatenate(idx_rows, axis=0)
    wts_ref[...] = w / (jnp.sum(w, axis=0, keepdims=True) + 1e-20) * ROUTED_SCALE

    mb = msel.astype(BF16)
    a = lax.broadcasted_iota(I32, (tm, tm), 0)
    b = lax.broadcasted_iota(I32, (tm, tm), 1)
    before = _dot(mb, jnp.where(a < b, 1.0, 0.0).astype(BF16))
    pos = base_s[...] + before
    rank_ref[...] = jnp.concatenate(
        [jnp.sum(jnp.where(ei == ix, pos, 0.0), axis=0, keepdims=True) for ix in idx_rows], axis=0).astype(I32)
    base_s[...] = base_s[...] + _dot(mb, jnp.ones((tm, tm), BF16))
    cnt_ref[...] = base_s[:, :LANES].astype(I32)


def _router(x1, g2, sc2, sh2, wt_hi, wt_lo, bias, seq, tm=512):
    t, d = x1.shape
    ne = wt_hi.shape[0]
    tm = min(tm, seq)
    per = seq // tm
    bvec = pl.BlockSpec((1, 1, d), lambda i: (i // per, 0, 0))
    kt = lambda dt: jax.ShapeDtypeStruct((TOP_K, t), dt)
    kspec = pl.BlockSpec((TOP_K, tm), lambda i: (0, i))
    return pl.pallas_call(
        _router_kernel,
        out_shape=(kt(I32), kt(F32), kt(I32), jax.ShapeDtypeStruct((ne, LANES), I32)),
        grid=(t // tm,),
        in_specs=[pl.BlockSpec((tm, d), lambda i: (i, 0)),
                  pl.BlockSpec((1, d), lambda i: (0, 0)), bvec, bvec,
                  pl.BlockSpec((ne, d), lambda i: (0, 0)),
                  pl.BlockSpec((ne, d), lambda i: (0, 0)),
                  pl.BlockSpec((ne, tm), lambda i: (0, 0))],
        out_specs=(kspec, kspec, kspec, pl.BlockSpec((ne, LANES), lambda i: (0, 0))),
        scratch_shapes=[pltpu.VMEM((ne, tm), F32)],
        compiler_params=_cparams("arbitrary"),
        name="router",
    )(x1, g2, sc2, sh2, wt_hi, wt_lo, jnp.broadcast_to(bias.reshape(ne, 1), (ne, tm)))


def _rows_kernel(eidx_ref, rank_ref, pstart_ref, rows_ref):
    ne, tm = pstart_ref.shape
    ei = lax.broadcasted_iota(I32, (ne, tm), 0)
    ps = pstart_ref[...]
    eidx = eidx_ref[...]
    rows_ref[...] = rank_ref[...] + jnp.concatenate(
        [jnp.sum(jnp.where(ei == eidx[k:k + 1, :], ps, 0), axis=0, keepdims=True) for k in range(TOP_K)], axis=0)


def _pair_rows(eidx, rank, pstart, tm=512):
    t = eidx.shape[1]
    ne = pstart.shape[0]
    tm = min(tm, t)
    kspec = pl.BlockSpec((TOP_K, tm), lambda i: (0, i))
    return pl.pallas_call(
        _rows_kernel,
        out_shape=jax.ShapeDtypeStruct((TOP_K, t), I32),
        grid=(t // tm,),
        in_specs=[kspec, kspec, pl.BlockSpec((ne, tm), lambda i: (0, 0))],
        out_specs=kspec,
        compiler_params=_cparams("arbitrary"),
        name="pair_rows",
    )(eidx, rank, jnp.broadcast_to(pstart.reshape(ne, 1), (ne, tm)))


def _tile(ref, r, n=1):
    return ref.at[pl.ds(pl.multiple_of(r * SUBLANES, SUBLANES), n * SUBLANES), :]


def _token_copy_all(src_of, dst_of, sem, rows_ref, tm):
    def issue(t, carry):
        for k in range(TOP_K):
            row = rows_ref[k, t]
            pltpu.make_async_copy(src_of(k, t, row), dst_of(k, t, row), sem).start(priority=k % 2)
        return carry
    lax.fori_loop(0, tm, issue, 0)


def _swiglu_packed(h_ref, wg_ref, wu_ref, wd_ref):
    lo, hi = _unpack_halves(_load_token_tiles(h_ref))
    lo, hi = lo.astype(BF16), hi.astype(BF16)
    n = lo.shape[1]
    g = _dot(lo, wg_ref[:n, :]) + _dot(hi, wg_ref[n:, :])
    u = _dot(lo, wu_ref[:n, :]) + _dot(hi, wu_ref[n:, :])
    return _dot((g * jax.nn.sigmoid(g) * u).astype(BF16), wd_ref[...])


def _dispatch_kernel(pstart_ref, cnt_ref, nused_ref, h_ref, rows_ref, wg_ref, wu_ref, wd_ref,
                     xs_ref, shared_ref, zbuf, sem, zsem):
    i = pl.program_id(0)
    tm = h_ref.shape[0] // SUBLANES
    bm = zbuf.shape[0] // SUBLANES
    ne = cnt_ref.shape[0]
    nblk = xs_ref.shape[0] // (bm * SUBLANES)

    def zero_fill(act):
        def per_expert(e, carry):
            c = cnt_ref[e]
            pad = lax.rem(bm - lax.rem(c, bm), bm)
            off = pstart_ref[e] + c
            s = bm // 2
            while s >= 1:
                @pl.when((pad & s) != 0)
                def _(s=s, off=off):
                    act(pltpu.make_async_copy(_tile(zbuf, 0, s), _tile(xs_ref, off, s), zsem))
                off = off + (pad & s)
                s //= 2
            return carry
        lax.fori_loop(0, ne, per_expert, 0)

        def per_block(j, carry):
            act(pltpu.make_async_copy(zbuf, _tile(xs_ref, j * bm, bm), zsem))
            return carry
        lax.fori_loop(nused_ref[0], nblk, per_block, 0)

    @pl.when(i == 0)
    def _():
        zbuf[...] = jnp.zeros_like(zbuf)
        zero_fill(lambda cp: cp.start())

    _token_copy_all(lambda k, t, row: _tile(h_ref, t), lambda k, t, row: _tile(xs_ref, row), sem, rows_ref, tm)
    shared_ref[...] = _swiglu_packed(h_ref, wg_ref, wu_ref, wd_ref).astype(BF16)
    for _ in range(TOP_K):
        pltpu.make_async_copy(h_ref, _tile(xs_ref, 0, tm), sem).wait()

    @pl.when(i == 0)
    def _():
        zero_fill(lambda cp: cp.wait())


def _dispatch(h2p, rows, pstart, counts, n_used, sh_wg, sh_wu, sh_wd, n_rows, bm, tm=256):
    t = h2p.shape[0] // SUBLANES
    d, f = sh_wg.shape
    tm = min(tm, t)
    const = lambda shp: pl.BlockSpec(shp, lambda i, *_: (0, 0), pipeline_mode=pl.Buffered(1))
    return pl.pallas_call(
        _dispatch_kernel,
        out_shape=(jax.ShapeDtypeStruct((n_rows * SUBLANES, LANES), U32), jax.ShapeDtypeStruct((t, d), BF16)),
        grid_spec=pltpu.PrefetchScalarGridSpec(
            num_scalar_prefetch=3,
            grid=(t // tm,),
            in_specs=[pl.BlockSpec((tm * SUBLANES, LANES), lambda i, *_: (i, 0)),
                      pl.BlockSpec((TOP_K, tm), lambda i, *_: (0, i), memory_space=pltpu.SMEM),
                      const((d, f)), const((d, f)), const((f, d))],
            out_specs=(pl.BlockSpec(memory_space=pl.ANY), pl.BlockSpec((tm, d), lambda i, *_: (i, 0))),
            scratch_shapes=[pltpu.VMEM((bm * SUBLANES, LANES), U32), pltpu.SemaphoreType.DMA(()),
                            pltpu.SemaphoreType.DMA(())]),
        compiler_params=_cparams("arbitrary"),
        name="dispatch",
    )(pstart, counts, n_used, h2p, rows, sh_wg, sh_wu, sh_wd)


def _expert_kernel(be_ref, first_ref, nxt_ref, slot_ref, nu_ref, xs_hbm, wg_hbm, wu_hbm, wd_hbm, y_hbm,
                   xbuf, ybuf, wg_buf, wu_buf, wd_buf, xsem, ysem, wsem):
    j = pl.program_id(0)
    nblk = pl.num_programs(0)
    nu = nu_ref[0]
    bm = xbuf.shape[1] // SUBLANES
    cur = lax.rem(j, 2)

    def weight_copies(e, s):
        return [pltpu.make_async_copy(hbm.at[e], buf.at[s], wsem.at[s, i])
                for i, (hbm, buf) in enumerate(((wg_hbm, wg_buf), (wu_hbm, wu_buf), (wd_hbm, wd_buf)))]

    def x_copy(blk, s):
        return pltpu.make_async_copy(_tile(xs_hbm, blk * bm, bm), xbuf.at[s], xsem.at[s])

    def y_copy(blk, s):
        return pltpu.make_async_copy(ybuf.at[s], _tile(y_hbm, blk * bm, bm), ysem.at[s])

    @pl.when((j == 0) & (nu > 0))
    def _():
        x_copy(0, 0).start()
        for cp in weight_copies(be_ref[0], slot_ref[0]):
            cp.start()

    @pl.when(j < nu)
    def _():
        x_copy(j, cur).wait()

    @pl.when(j + 1 < nu)
    def _():
        x_copy(j + 1, 1 - cur).start()

    @pl.when((j < nu) & (first_ref[j] == 1))
    def _():
        for cp in weight_copies(be_ref[j], slot_ref[j]):
            cp.wait()

        @pl.when(nxt_ref[j] >= 0)
        def _():
            for cp in weight_copies(nxt_ref[j], 1 - slot_ref[j]):
                cp.start()

    @pl.when(j >= 2)
    def _():
        y_copy(j - 2, cur).wait()

    @pl.when(j < nu)
    def _():
        s = slot_ref[j]
        lo, hi = _unpack_halves(_load_token_tiles(xbuf.at[cur]))
        lo, hi = lo.astype(BF16), hi.astype(BF16)
        n = lo.shape[1]
        wg = wg_buf[s].astype(BF16)
        wu = wu_buf[s].astype(BF16)
        g = _dot(lo, wg[:n]) + _dot(hi, wg[n:])
        u = _dot(lo, wu[:n]) + _dot(hi, wu[n:])
        a = (g * jax.nn.sigmoid(g) * u).astype(BF16)
        _store_token_tiles(ybuf.at[cur], _pack_halves(_dot(a, wd_buf[s].astype(BF16))))

    @pl.when(j >= nu)
    def _():
        ybuf[cur] = jnp.zeros(ybuf.shape[1:], ybuf.dtype)

    y_copy(j, cur).start()

    @pl.when(j == nblk - 1)
    def _():
        @pl.when(j >= 1)
        def _():
            y_copy(j - 1, 1 - cur).wait()
        y_copy(j, cur).wait()


def _experts(xs, block_e, block_first, block_next, block_slot, n_used, w_gate, w_up, w_down, bm):
    n_rows = xs.shape[0] // SUBLANES
    ne, d, f = w_gate.shape
    nblk = n_rows // bm
    blk = (2, bm * SUBLANES, LANES)
    return pl.pallas_call(
        _expert_kernel,
        out_shape=jax.ShapeDtypeStruct(xs.shape, U32),
        grid_spec=pltpu.PrefetchScalarGridSpec(
            num_scalar_prefetch=5,
            grid=(nblk,),
            in_specs=[pl.BlockSpec(memory_space=pl.ANY)] * 4,
            out_specs=pl.BlockSpec(memory_space=pl.ANY),
            scratch_shapes=[pltpu.VMEM(blk, U32), pltpu.VMEM(blk, U32),
                            pltpu.VMEM((2, d, f), F32), pltpu.VMEM((2, d, f), F32), pltpu.VMEM((2, f, d), F32),
                            pltpu.SemaphoreType.DMA((2,)), pltpu.SemaphoreType.DMA((2,)),
                            pltpu.SemaphoreType.DMA((2, 3))]),
        compiler_params=_cparams("arbitrary"),
        name="experts",
    )(block_e, block_first, block_next, block_slot, n_used, xs, w_gate, w_up, w_down)


def _final_kernel(n_first, x1_ref, sh_ref, gt_ref, wts_ref, fg_ref, rows_ref, rows_nx_ref, y_ref,
                  oa_ref, ob_ref, ybuf, sems):
    i = pl.program_id(0)
    tm = x1_ref.shape[0]
    n = x1_ref.shape[1] // 2
    slot = lax.rem(i, 2)

    def gather(r_ref, s):
        _token_copy_all(lambda k, t, row: _tile(y_ref, row), lambda k, t, row: _tile(ybuf.at[s, k], t),
                        sems.at[s], r_ref, tm)

    pl.when(i == 0)(lambda: gather(rows_ref, 0))
    pl.when(i + 1 < pl.num_programs(0))(lambda: gather(rows_nx_ref, 1 - slot))

    for k in range(TOP_K):
        pltpu.make_async_copy(_tile(y_ref, 0, tm), ybuf.at[slot, k], sems.at[slot]).wait()
    shared = sh_ref[...].astype(F32)
    acc_lo, acc_hi = shared[:, :n], shared[:, n:]
    wts = wts_ref[...].T
    for k in range(TOP_K):
        ylo, yhi = _unpack_halves(_load_token_tiles(ybuf.at[slot, k]))
        wk = wts[:, k:k + 1]
        acc_lo = acc_lo + wk * ylo
        acc_hi = acc_hi + wk * yhi
    gt = gt_ref[0]
    x_lo = x1_ref[:, :n] + gt[:, :n] * acc_lo
    x_hi = x1_ref[:, n:] + gt[:, n:] * acc_hi
    ms = (jnp.sum(x_lo * x_lo, axis=-1, keepdims=True) + jnp.sum(x_hi * x_hi, axis=-1, keepdims=True)) / (2 * n)
    r = lax.rsqrt(ms + RMS_EPS)

    def write(o_ref):
        o_ref[:, :n] = x_lo * r * fg_ref[:, :n]
        o_ref[:, n:] = x_hi * r * fg_ref[:, n:]

    pl.when(i < n_first)(lambda: write(oa_ref))
    pl.when(i >= n_first)(lambda: write(ob_ref))


def _final(x1, shared, gt2, wts, final_g, rows, y, rows_first, seq, tm=256):
    t, d = x1.shape
    tm = min(tm, seq)
    per = seq // tm
    nt = t // tm
    n_first = rows_first // tm
    out_a, out_b = _two_trunk_specs((tm, d), n_first)
    return pl.pallas_call(
        functools.partial(_final_kernel, n_first),
        out_shape=(jax.ShapeDtypeStruct((rows_first, d), F32), jax.ShapeDtypeStruct((t - rows_first, d), F32)),
        grid=(nt,),
        in_specs=[pl.BlockSpec((tm, d), lambda i: (i, 0)),
                  pl.BlockSpec((tm, d), lambda i: (i, 0)),
                  pl.BlockSpec((1, 1, d), lambda i: (i // per, 0, 0)),
                  pl.BlockSpec((TOP_K, tm), lambda i: (0, i)),
                  pl.BlockSpec((1, d), lambda i: (0, 0)),
                  pl.BlockSpec((TOP_K, tm), lambda i: (0, i), memory_space=pltpu.SMEM),
                  pl.BlockSpec((TOP_K, tm), lambda i: (0, jnp.minimum(i + 1, nt - 1)), memory_space=pltpu.SMEM),
                  pl.BlockSpec(memory_space=pl.ANY)],
        out_specs=(out_a, out_b),
        scratch_shapes=[pltpu.VMEM((2, TOP_K, tm * SUBLANES, LANES), U32), pltpu.SemaphoreType.DMA((2,))],
        compiler_params=_cparams("arbitrary"),
        name="final",
    )(x1, shared, gt2, wts, final_g, rows, rows, y)


EXPERT_ROWS = 256


def _prep_moe(w):
    rw = w["router_w"][0]
    d = rw.shape[0]
    wt = rw.T
    wt_hi = wt.astype(BF16)
    return dict(
        wt_hi=wt_hi, wt_lo=(wt - wt_hi.astype(F32)).astype(BF16), bias=w["router_bias"][0],
        w_gate=w["exp_w_gate"][0], w_up=w["exp_w_up"][0], w_down=w["exp_w_down"][0],
        sh_wg=w["sh_w_gate"][0].astype(BF16), sh_wu=w["sh_w_up"][0].astype(BF16),
        sh_wd=w["sh_w_down"][0].astype(BF16), final_g=w["final_g"].reshape(1, d),
    )


def _moe_stage(x1, h2p, mods, pm, pw, seq, rows_first):
    _, _, _, sh2, sc2, gt2 = mods
    t = x1.shape[0]
    ne = pw["wt_hi"].shape[0]
    bm = EXPERT_ROWS
    eidx, wts, rank, cnt = _router(x1, pm["norm_ffn_g"], sc2, sh2, pw["wt_hi"], pw["wt_lo"], pw["bias"], seq)
    counts = cnt[:, 0]
    pcounts = (counts + bm - 1) // bm * bm
    pends = jnp.cumsum(pcounts)
    pstart = (pends - pcounts).astype(I32)
    nblk = (t * TOP_K + ne * (bm - 1) + bm - 1) // bm
    n_used = (pends[-1:] // bm).astype(I32)
    jb = jnp.arange(nblk, dtype=I32)
    be = jnp.minimum(jnp.sum((pends[None, :] <= (jb * bm)[:, None]).astype(I32), axis=1), ne - 1)
    ids = jnp.arange(ne, dtype=I32)
    onehot = be[:, None] == ids[None, :]
    look = lambda table: jnp.sum(jnp.where(onehot, table[None, :], 0), axis=1).astype(I32)
    first = (jb * bm == look(pstart)).astype(I32)
    live = counts > 0
    ordinal = jnp.cumsum(live.astype(I32)) - 1
    after = lax.cummin(jnp.where(live, ids, ne), reverse=True)
    next_live = jnp.concatenate([after[1:], jnp.full((1,), ne, I32)])
    nxt = look(jnp.where(next_live < ne, next_live, -1))
    slot = look(ordinal) % 2
    rows = _pair_rows(eidx, rank, pstart)
    xs, shared = _dispatch(h2p, rows, pstart, counts, n_used, pw["sh_wg"], pw["sh_wu"], pw["sh_wd"],
                           nblk * bm, bm)
    y = _experts(xs, be, first, nxt, slot, n_used, pw["w_gate"], pw["w_up"], pw["w_down"], bm)
    return _final(x1, shared, gt2, wts, pw["final_g"], rows, y, rows_first, seq)


def kernel(x_prompt, x_sample, c_prompt, c_sample, ada_w, ada_b, norm_mix_g, norm_ffn_g, w_in, sgu_ln_g,
           sgu_ln_b, sgu_ws, sgu_b, dn_conv_w, dn_a_log, dn_dt_bias, dn_norm_g, w_branch, w_out, router_w,
           router_bias, exp_w_gate, exp_w_up, exp_w_down, sh_w_gate, sh_w_up, sh_w_down, final_g):
    w = dict(ada_w=ada_w, ada_b=ada_b, norm_mix_g=norm_mix_g, norm_ffn_g=norm_ffn_g, w_in=w_in,
             sgu_ln_g=sgu_ln_g, sgu_ln_b=sgu_ln_b, sgu_ws=sgu_ws, sgu_b=sgu_b, dn_conv_w=dn_conv_w,
             dn_a_log=dn_a_log, dn_dt_bias=dn_dt_bias, dn_norm_g=dn_norm_g, w_branch=w_branch, w_out=w_out,
             router_w=router_w, router_bias=router_bias, exp_w_gate=exp_w_gate, exp_w_up=exp_w_up,
             exp_w_down=exp_w_down, sh_w_gate=sh_w_gate, sh_w_up=sh_w_up, sh_w_down=sh_w_down, final_g=final_g)
    assert w_in.shape[0] == 1, "one layer"
    bp, seq, d = x_prompt.shape
    bs = x_sample.shape[0]
    assert x_sample.shape[1] == seq
    nb = bp + bs
    xa = x_prompt.reshape(bp * seq, d)
    xb = x_sample.reshape(bs * seq, d)
    c = jnp.concatenate([c_prompt, c_sample], axis=0)
    npad = -nb % 8
    mod = _ada(jnp.pad(c, ((0, npad), (0, 0))), ada_w[0], ada_b[0])[:nb]
    mods = [m.reshape(nb, 1, d) for m in jnp.split(mod, 6, axis=-1)]
    pm = _prep_mixer(w, nb, seq)
    pw = _prep_moe(w)
    x1, h2p = _token_mixer_stage(xa, xb, mods, pm)
    ya, yb = _moe_stage(x1, h2p, mods, pm, pw, seq, bp * seq)
    return (ya.reshape(bp, seq, d), yb.reshape(bs, seq, d))
```

```python
import functools

import jax
import jax.numpy as jnp
import numpy as np
from jax import lax
from jax.experimental import pallas as pl
from jax.experimental.pallas import tpu as pltpu

F32 = jnp.float32
BF16 = jnp.bfloat16
I32 = jnp.int32
U32 = jnp.uint32

RMS_EPS = 1e-6
LN_EPS = 1e-5
L2_EPS = 1e-6

SGU_GROUPS = 8
SGU_CHUNK = 128
DN_HEADS = 8
DN_HEAD_DIM = 128
DN_CHUNK = 256
DN_PREP_UNROLL = 2
CONV_WIDTH = 5
TOP_K = 8
N_GROUPS = 8
TOPK_GROUPS = 4
ROUTED_SCALE = 2.5

LANES = 128
SUBLANES = 8
VMEM_LIMIT = 56 * 1024 * 1024


def _cparams(*sem):
    return pltpu.CompilerParams(dimension_semantics=sem, vmem_limit_bytes=VMEM_LIMIT)


def _split3(x):
    hi = x.astype(BF16)
    r = x - hi.astype(F32)
    mid = r.astype(BF16)
    lo = (r - mid.astype(F32)).astype(BF16)
    return hi, mid, lo


def _dot(a, b):
    return jnp.dot(a, b, preferred_element_type=F32)


def _dot_nt(a, b):
    return lax.dot_general(a, b, (((1,), (1,)), ((), ())), preferred_element_type=F32)


def _dot_tn(a, b):
    return lax.dot_general(a, b, (((0,), (0,)), ((), ())), preferred_element_type=F32)


def _rms(x, g):
    return x * lax.rsqrt(jnp.mean(x * x, axis=-1, keepdims=True) + RMS_EPS) * g


def _pack_halves(x):
    n = x.shape[1] // 2
    return pltpu.pack_elementwise([x[:, :n], x[:, n:]], packed_dtype=BF16)


def _unpack_halves(w):
    lo = pltpu.unpack_elementwise(w, index=0, packed_dtype=BF16, unpacked_dtype=F32)
    hi = pltpu.unpack_elementwise(w, index=1, packed_dtype=BF16, unpacked_dtype=F32)
    return lo, hi


def _store_token_tiles(ref, x):
    m = x.shape[0]
    for s in range(SUBLANES):
        ref[pl.ds(s, m, stride=SUBLANES), :] = x[:, s * LANES:(s + 1) * LANES]


def _load_token_tiles(ref):
    m = ref.shape[0] // SUBLANES
    return jnp.concatenate([ref[pl.ds(s, m, stride=SUBLANES), :] for s in range(SUBLANES)], axis=1)


def _ada_kernel(c_ref, w_ref, b_ref, o_ref):
    c = c_ref[...]
    a = c * jax.nn.sigmoid(c)
    a_hi, a_mid, _ = _split3(a)
    w = w_ref[...]
    w_hi = w.astype(BF16)
    w_lo = (w - w_hi.astype(F32)).astype(BF16)
    o_ref[...] = _dot(a_hi, w_hi) + _dot(a_mid, w_hi) + _dot(a_hi, w_lo) + b_ref[...]


def _ada(c, ada_w, ada_b, tn=1024):
    nb, d = c.shape
    n = ada_w.shape[1]
    return pl.pallas_call(
        _ada_kernel,
        out_shape=jax.ShapeDtypeStruct((nb, n), F32),
        grid=(n // tn,),
        in_specs=[pl.BlockSpec((nb, d), lambda j: (0, 0)),
                  pl.BlockSpec((d, tn), lambda j: (0, j)),
                  pl.BlockSpec((1, tn), lambda j: (0, j))],
        out_specs=pl.BlockSpec((nb, tn), lambda j: (0, j)),
        compiler_params=_cparams("arbitrary"),
        name="ada",
    )(c, ada_w, ada_b.reshape(1, n))


def _two_trunk_specs(block, n_first):
    first = pl.BlockSpec(block, lambda i, *_: (jnp.minimum(i, n_first - 1), 0))
    second = pl.BlockSpec(block, lambda i, *_: (jnp.maximum(i - n_first, 0), 0))
    return first, second


def _inproj_kernel(n_first, xa_ref, xb_ref, g_ref, sc_ref, sh_ref, w_ref, wab_ref, z_ref, ab_ref, h_s):
    i = pl.program_id(0)
    j = pl.program_id(1)

    def prologue(x_ref):
        h = _rms(x_ref[...], g_ref[...]) * (1.0 + sc_ref[0]) + sh_ref[0]
        h_s[...] = h.astype(BF16)
        ab_ref[...] = _dot(h_s[...], wab_ref[...])

    pl.when((j == 0) & (i < n_first))(lambda: prologue(xa_ref))
    pl.when((j == 0) & (i >= n_first))(lambda: prologue(xb_ref))
    z_ref[...] = _dot(h_s[...], w_ref[...]).astype(BF16)


def _inproj(xa, xb, g, sc, sh, w_main, w_ab, seq, tm=512, tn=2048):
    d = xa.shape[1]
    t = xa.shape[0] + xb.shape[0]
    n = w_main.shape[1]
    tm = min(tm, seq)
    per = seq // tm
    n_first = xa.shape[0] // tm
    spec_a, spec_b = _two_trunk_specs((tm, d), n_first)
    return pl.pallas_call(
        functools.partial(_inproj_kernel, n_first),
        out_shape=(jax.ShapeDtypeStruct((t, n), BF16), jax.ShapeDtypeStruct((t, LANES), F32)),
        grid=(t // tm, n // tn),
        in_specs=[spec_a, spec_b,
                  pl.BlockSpec((1, d), lambda i, j: (0, 0)),
                  pl.BlockSpec((1, 1, d), lambda i, j: (i // per, 0, 0)),
                  pl.BlockSpec((1, 1, d), lambda i, j: (i // per, 0, 0)),
                  pl.BlockSpec((d, tn), lambda i, j: (0, j)),
                  pl.BlockSpec((d, LANES), lambda i, j: (0, 0))],
        out_specs=(pl.BlockSpec((tm, tn), lambda i, j: (i, j)),
                   pl.BlockSpec((tm, LANES), lambda i, j: (i, 0))),
        scratch_shapes=[pltpu.VMEM((tm, d), BF16)],
        compiler_params=_cparams("arbitrary", "arbitrary"),
        name="inproj",
    )(xa, xb, g, sc, sh, w_main, w_ab)


def _sgu_kernel(u_ref, v_ref, g_ref, b_ref, ws_ref, sb_ref, o_ref):
    v = jax.nn.gelu(v_ref[...].astype(F32))
    mu = jnp.mean(v, axis=-1, keepdims=True)
    vc = v - mu
    var = jnp.mean(vc * vc, axis=-1, keepdims=True)
    vn = (vc * lax.rsqrt(var + LN_EPS) * g_ref[...] + b_ref[...]).astype(BF16)
    tm = vn.shape[0]
    for c in range(tm // SGU_CHUNK):
        r = slice(c * SGU_CHUNK, (c + 1) * SGU_CHUNK)
        for gi in range(SGU_GROUPS):
            l = slice(gi * LANES, (gi + 1) * LANES)
            mixed = _dot(ws_ref[gi], vn[r, l]) + sb_ref[gi]
            o_ref[r, l] = (jax.nn.gelu(u_ref[r, l].astype(F32)) * mixed).astype(BF16)


def _sgu(z, ln_g, ln_b, ws, sb_b, tm=512):
    t = z.shape[0]
    w = SGU_GROUPS * LANES
    tm = np.gcd(tm, t)
    return pl.pallas_call(
        _sgu_kernel,
        out_shape=jax.ShapeDtypeStruct((t, w), BF16),
        grid=(t // tm,),
        in_specs=[pl.BlockSpec((tm, w), lambda i: (i, 0)),
                  pl.BlockSpec((tm, w), lambda i: (i, 1)),
                  pl.BlockSpec((1, w), lambda i: (0, 0)),
                  pl.BlockSpec((1, w), lambda i: (0, 0)),
                  pl.BlockSpec((SGU_GROUPS, SGU_CHUNK, SGU_CHUNK), lambda i: (0, 0, 0)),
                  pl.BlockSpec((SGU_GROUPS, SGU_CHUNK, LANES), lambda i: (0, 0, 0))],
        out_specs=pl.BlockSpec((tm, w), lambda i: (i, 0)),
        compiler_params=_cparams("arbitrary"),
        name="sgu",
    )(z, z, ln_g, ln_b, ws, sb_b)


def _gate_kernel(ab_ref, alog_ref, dt_ref, gcol_ref, grow_ref):
    ab = ab_ref[...]
    tr = ab.shape[0]
    lane = lax.broadcasted_iota(I32, ab.shape, 1)
    z = ab + dt_ref[...]
    softplus = jnp.maximum(z, 0.0) + jnp.log1p(jnp.exp(-jnp.abs(z)))
    g = -jnp.exp(alog_ref[...]) * softplus
    beta = jax.nn.sigmoid(ab)
    nh = DN_HEADS
    g = jnp.where(lane < 2 * nh, g, 0.0)
    i = lax.broadcasted_iota(I32, (tr, tr), 0)
    j = lax.broadcasted_iota(I32, (tr, tr), 1)
    same = (i // DN_CHUNK) == (j // DN_CHUNK)
    lower = jnp.where(same & (j <= i), 1.0, 0.0).astype(BF16)
    upper = jnp.where(same & (j >= i), 1.0, 0.0).astype(BF16)
    g_hi, g_mid, g_lo = _split3(g)
    pre = _dot(lower, g_hi) + _dot(lower, g_mid) + _dot(lower, g_lo)
    suf = _dot(upper, g_hi) + _dot(upper, g_mid) + _dot(upper, g_lo)
    out = jnp.where(lane < nh, pre, jnp.where(lane < 2 * nh, suf, jnp.where(lane < 4 * nh, beta, 0.0)))
    gcol_ref[...] = out
    grow_ref[0] = out.T[:2 * nh, :]


def _gates(ab, alog_row, dt_row, nb, seq, tr=256):
    t = ab.shape[0]
    per = seq // tr
    return pl.pallas_call(
        _gate_kernel,
        out_shape=(jax.ShapeDtypeStruct((t, LANES), F32),
                   jax.ShapeDtypeStruct((nb, 2 * DN_HEADS, seq), F32)),
        grid=(t // tr,),
        in_specs=[pl.BlockSpec((tr, LANES), lambda i: (i, 0)),
                  pl.BlockSpec((1, LANES), lambda i: (0, 0)),
                  pl.BlockSpec((1, LANES), lambda i: (0, 0))],
        out_specs=(pl.BlockSpec((tr, LANES), lambda i: (i, 0)),
                   pl.BlockSpec((1, 2 * DN_HEADS, tr), lambda i: (i // per, 0, i % per))),
        compiler_params=_cparams("arbitrary"),
        name="gates",
    )(ab, alog_row, dt_row)


def _conv_silu(x_ref, w_ref):
    x = x_ref[...].astype(F32)
    s = x.shape[0]
    row = lax.broadcasted_iota(I32, x.shape, 0)
    pad = (CONV_WIDTH - 1) // 2
    acc = x * w_ref[pad:pad + 1, :]
    for j in range(CONV_WIDTH):
        d = j - pad
        if d == 0:
            continue
        xs = pltpu.roll(x, (-d) % s, 0)
        ok = (row + d >= 0) & (row + d < s)
        acc = acc + jnp.where(ok, xs, 0.0) * w_ref[j:j + 1, :]
    return acc * jax.nn.sigmoid(acc)


def _l2n(x):
    return x * lax.rsqrt(jnp.sum(x * x, axis=-1, keepdims=True) + L2_EPS)


def _unit_tri_inverses(mats, i, j):
    c = mats[0].shape[0]
    x = i ^ j
    eye = jnp.where(i == j, 1.0, 0.0)
    tinv = [eye - jnp.where(x < 2, a, 0.0) for a in mats]
    s = 2
    while s < c:
        level = (x >= s) & (x < 2 * s)
        tb = [t.astype(BF16) for t in tinv]
        m = [_dot(t, jnp.where(level, a, 0.0).astype(BF16)).astype(BF16) for t, a in zip(tb, mats)]
        tinv = [t - _dot(mm, b) for t, mm, b in zip(tinv, m, tb)]
        s *= 2
    return tinv


def _dn_prep(items):
    c, hd = items[0][1].shape
    i = lax.broadcasted_iota(I32, (c, c), 0)
    j = lax.broadcasted_iota(I32, (c, c), 1)
    decs, kbs, kbfs = [], [], []
    for forward, q, k, v, gam, grow, beta in items:
        incl = (i >= j) if forward else (i <= j)
        gi = jnp.concatenate([gam] * (c // LANES), axis=1)
        decs.append(jnp.where(incl, jnp.exp(gi - grow), 0.0))
        kbs.append(k * beta)
        kbfs.append(k.astype(BF16))
    kk = [_dot_nt(kb.astype(BF16), kbf) for kb, kbf in zip(kbs, kbfs)]
    mats = [jnp.where((i > j) if it[0] else (i < j), m * dec, 0.0) for it, m, dec in zip(items, kk, decs)]
    tinvs = _unit_tri_inverses(mats, i, j)
    egs = [jnp.exp(it[4]) for it in items]
    uws = [_dot(t.astype(BF16), jnp.concatenate([it[3] * it[6], kb * eg], axis=1).astype(BF16))
           for t, it, kb, eg in zip(tinvs, items, kbs, egs)]
    attns = [_dot_nt(it[1].astype(BF16), kbf) * dec for it, kbf, dec in zip(items, kbfs, decs)]
    outs = []
    for (forward, q, k, v, gam, grow, beta), uw, eg, attn in zip(items, uws, egs, attns):
        glast = gam[c - 1:c, :] if forward else gam[0:1, :]
        kd = k * jnp.exp(glast - gam)
        outs.append((uw[:, :hd], uw[:, hd:].astype(BF16), (q * eg).astype(BF16), kd.T.astype(BF16),
                     attn.astype(BF16)))
    return outs


def _dn_kernel(q_ref, k_ref, v_ref, og_ref, wq_ref, wk_ref, wv_ref, gcol_ref, grow_ref, ng_ref, o_ref,
               q_s, k_s, v_s, gam_s, beta_s, u_s, w_s, qd_s, kdt_s, attn_s, o_s):
    h = pl.program_id(1)
    nh = DN_HEADS
    hd = DN_HEAD_DIM
    q_s[...] = _l2n(_conv_silu(q_ref, wq_ref)) * (hd ** -0.5)
    k_s[...] = _l2n(_conv_silu(k_ref, wk_ref))
    v_s[...] = _conv_silu(v_ref, wv_ref)

    g_hi, g_mid, g_lo = _split3(gcol_ref[...])
    rr = lax.broadcasted_iota(I32, (LANES, 2 * LANES), 0)
    cc = lax.broadcasted_iota(I32, (LANES, 2 * LANES), 1)
    sel = lambda off: jnp.where(rr == h + off + jnp.where(cc < LANES, 0, nh), 1.0, 0.0).astype(BF16)
    sel_g = sel(0)
    gam2 = (_dot(jnp.concatenate([g_hi, g_mid], axis=1), jnp.concatenate([sel_g, sel_g], axis=0))
            + _dot(g_lo, sel_g))
    beta2 = _dot(g_hi, sel(2 * nh))
    for d in range(2):
        gam_s[d] = gam2[:, d * LANES:(d + 1) * LANES]
        beta_s[d] = beta2[:, d * LANES:(d + 1) * LANES]

    s = q_s.shape[0]
    c = DN_CHUNK
    n = s // c

    def prep(it, carry):
        where, items = [], []
        for sub in range(DN_PREP_UNROLL):
            rows = pl.ds(pl.multiple_of((it * DN_PREP_UNROLL + sub) * c, c), c)
            for d in range(2):
                where.append((d, rows))
                items.append((d == 0, q_s[rows, :], k_s[rows, :], v_s[rows, :], gam_s[d, rows, :],
                              grow_ref[0, pl.ds(d * nh + h, 1), rows], beta_s[d, rows, :]))
        for (d, rows), (u, w, qd, kdt, attn) in zip(where, _dn_prep(items)):
            u_s[d, rows, :] = u
            w_s[d, rows, :] = w
            qd_s[d, rows, :] = qd
            kdt_s[d, :, rows] = kdt
            attn_s[d, rows, :] = attn
        return carry

    lax.fori_loop(0, n // DN_PREP_UNROLL, prep, 0)

    def scan(it, states):
        r0s = [pl.multiple_of((it if d == 0 else n - 1 - it) * c, c) for d in range(2)]
        rows = [pl.ds(r0, c) for r0 in r0s]
        rs = [_dot(jnp.concatenate([w_s[d, rows[d], :], qd_s[d, rows[d], :]], axis=0), states[d].astype(BF16))
              for d in range(2)]
        vbs = [(u_s[d, rows[d], :] - rs[d][:c]).astype(BF16) for d in range(2)]
        for d in range(2):
            o_s[d, rows[d], :] = rs[d][c:] + _dot(attn_s[d, rows[d], :], vbs[d])
        new = []
        for d in range(2):
            glast = gam_s[d, pl.ds(r0s[d] + (c - 1 if d == 0 else 0), 1), :]
            new.append(states[d] * jnp.exp(glast) + _dot(kdt_s[d, :, rows[d]], vbs[d]))
        return tuple(new)

    zero = jnp.zeros((hd, hd), F32)
    lax.fori_loop(0, n, scan, (zero, zero))

    o = o_s[0] + o_s[1]
    og = og_ref[...].astype(F32)
    o_ref[...] = (_rms(o, ng_ref[...]) * (og * jax.nn.sigmoid(og))).astype(BF16)


def _deltanet(z, conv_w, gcol, grow, norm_g, nb, seq, qkv_blk, og_blk):
    t = z.shape[0]
    nh, hd, c = DN_HEADS, DN_HEAD_DIM, DN_CHUNK
    assert seq % (c * DN_PREP_UNROLL) == 0
    zspec = lambda off: pl.BlockSpec((seq, hd), lambda b, h: (b, off + h))
    wspec = lambda off: pl.BlockSpec((CONV_WIDTH, hd), lambda b, h: (0, off + h))
    return pl.pallas_call(
        _dn_kernel,
        out_shape=jax.ShapeDtypeStruct((t, nh * hd), BF16),
        grid=(nb, nh),
        in_specs=[zspec(qkv_blk), zspec(qkv_blk + nh), zspec(qkv_blk + 2 * nh), zspec(og_blk),
                  wspec(0), wspec(nh), wspec(2 * nh),
                  pl.BlockSpec((seq, LANES), lambda b, h: (b, 0)),
                  pl.BlockSpec((1, 2 * nh, seq), lambda b, h: (b, 0, 0)),
                  pl.BlockSpec((1, hd), lambda b, h: (0, 0))],
        out_specs=pl.BlockSpec((seq, hd), lambda b, h: (b, h)),
        scratch_shapes=[pltpu.VMEM((seq, hd), F32)] * 3
                       + [pltpu.VMEM((2, seq, hd), F32)] * 3
                       + [pltpu.VMEM((2, seq, hd), BF16)] * 2
                       + [pltpu.VMEM((2, hd, seq), BF16), pltpu.VMEM((2, seq, c), BF16),
                          pltpu.VMEM((2, seq, hd), F32)],
        compiler_params=_cparams("arbitrary", "arbitrary"),
        name="deltanet",
    )(z, z, z, z, conv_w, conv_w, conv_w, gcol, grow, norm_g)


def _merge_kernel(n_first, a_ref, b_ref, ga_ref, gb_ref, xa_ref, xb_ref, gt_ref, wa_ref, wb_ref, wo_ref, g2_ref,
                  sc_ref, sh_ref, x1_ref, h2_ref):
    i = pl.program_id(0)
    ya = _dot(a_ref[...], wa_ref[...])
    yb = _dot(b_ref[...], wb_ref[...])
    m = jax.nn.sigmoid(ga_ref[...].astype(F32)) * ya + jax.nn.sigmoid(gb_ref[...].astype(F32)) * yb
    y = _dot(m.astype(BF16), wo_ref[...])

    def epilogue(x_ref):
        x1 = x_ref[...] + gt_ref[0] * y
        x1_ref[...] = x1
        h2 = _rms(x1, g2_ref[...]) * (1.0 + sc_ref[0]) + sh_ref[0]
        _store_token_tiles(h2_ref, _pack_halves(h2))

    pl.when(i < n_first)(lambda: epilogue(xa_ref))
    pl.when(i >= n_first)(lambda: epilogue(xb_ref))


def _merge(a_out, b_out, z, xa, xb, gt1, wa, wb, wo, g2, sc2, sh2, seq, gate_blk, tm=256):
    d = xa.shape[1]
    t = xa.shape[0] + xb.shape[0]
    wdt = a_out.shape[1]
    tm = min(tm, seq)
    per = seq // tm
    n_first = xa.shape[0] // tm
    spec_a, spec_b = _two_trunk_specs((tm, d), n_first)
    bvec = pl.BlockSpec((1, 1, d), lambda i: (i // per, 0, 0))
    const = lambda shp: pl.BlockSpec(shp, lambda i: (0, 0), pipeline_mode=pl.Buffered(1))
    return pl.pallas_call(
        functools.partial(_merge_kernel, n_first),
        out_shape=(jax.ShapeDtypeStruct((t, d), F32), jax.ShapeDtypeStruct((t * SUBLANES, LANES), U32)),
        grid=(t // tm,),
        in_specs=[pl.BlockSpec((tm, wdt), lambda i: (i, 0)),
                  pl.BlockSpec((tm, wdt), lambda i: (i, 0)),
                  pl.BlockSpec((tm, d), lambda i: (i, gate_blk)),
                  pl.BlockSpec((tm, d), lambda i: (i, gate_blk + 1)),
                  spec_a, spec_b,
                  bvec, const((wdt, d)), const((wdt, d)), const((d, d)),
                  pl.BlockSpec((1, d), lambda i: (0, 0)), bvec, bvec],
        out_specs=(pl.BlockSpec((tm, d), lambda i: (i, 0)),
                   pl.BlockSpec((tm * SUBLANES, LANES), lambda i: (i, 0))),
        compiler_params=_cparams("arbitrary"),
        name="merge",
    )(a_out, b_out, z, z, xa, xb, gt1, wa, wb, wo, g2, sc2, sh2)


def _prep_mixer(w, nb, seq):
    wdt = SGU_GROUPS * LANES
    nh, hd = DN_HEADS, DN_HEAD_DIM
    w_in = w["w_in"][0]
    d = w_in.shape[0]
    ab0 = 2 * wdt + 4 * nh * hd
    w_main = jnp.concatenate([w_in[:, :ab0], w_in[:, ab0 + 4 * nh:]], axis=1).astype(BF16)
    w_ab = jnp.pad(w_in[:, ab0:ab0 + 4 * nh], ((0, 0), (0, LANES - 4 * nh))).astype(BF16)
    pad_row = lambda v: jnp.pad(v.reshape(1, 2 * nh), ((0, 0), (0, LANES - 2 * nh)))
    sb = w["sgu_b"][0]
    wbr = w["w_branch"][0]
    return dict(
        nb=nb, seq=seq,
        norm_mix_g=w["norm_mix_g"][0].reshape(1, d), norm_ffn_g=w["norm_ffn_g"][0].reshape(1, d),
        w_main=w_main, w_ab=w_ab,
        sgu_ln_g=w["sgu_ln_g"][0].reshape(1, wdt), sgu_ln_b=w["sgu_ln_b"][0].reshape(1, wdt),
        sgu_ws=w["sgu_ws"][0].astype(BF16),
        sgu_sb=jnp.broadcast_to(sb[:, :, None], sb.shape + (LANES,)),
        alog_row=pad_row(w["dn_a_log"][0]), dt_row=pad_row(w["dn_dt_bias"][0]),
        dn_conv_w=w["dn_conv_w"][0], dn_norm_g=w["dn_norm_g"][0].reshape(1, hd),
        wa=wbr[:wdt].astype(BF16), wb=wbr[wdt:].astype(BF16), wo=w["w_out"][0].astype(BF16),
    )


def _token_mixer_stage(xa, xb, mods, p):
    nb, seq = p["nb"], p["seq"]
    d = xa.shape[1]
    sh1, sc1, gt1, sh2, sc2, _ = mods
    wdt = SGU_GROUPS * LANES
    nh = DN_HEADS
    z, ab = _inproj(xa, xb, p["norm_mix_g"], sc1, sh1, p["w_main"], p["w_ab"], seq)
    a_out = _sgu(z, p["sgu_ln_g"], p["sgu_ln_b"], p["sgu_ws"], p["sgu_sb"])
    gcol, grow = _gates(ab, p["alog_row"], p["dt_row"], nb, seq)
    qkv_blk = 2 * wdt // LANES
    b_out = _deltanet(z, p["dn_conv_w"], gcol, grow, p["dn_norm_g"], nb, seq,
                      qkv_blk=qkv_blk, og_blk=qkv_blk + 3 * nh)
    gate_blk = (2 * wdt + 4 * nh * DN_HEAD_DIM) // d
    return _merge(a_out, b_out, z, xa, xb, gt1, p["wa"], p["wb"], p["wo"], p["norm_ffn_g"], sc2, sh2, seq,
                  gate_blk)


def _first_argmax(vals, idx, n, axis):
    mx = jnp.max(vals, axis=axis, keepdims=True)
    ix = jnp.min(jnp.where(vals == mx, idx, n), axis=axis, keepdims=True)
    return mx, ix


def _router_kernel(x_ref, g_ref, sc_ref, sh_ref, whi_ref, wlo_ref, bias_ref,
                   eidx_ref, wts_ref, rank_ref, cnt_ref, base_s):
    i = pl.program_id(0)
    ne, tm = bias_ref.shape
    neg = -jnp.inf

    @pl.when(i == 0)
    def _():
        base_s[...] = jnp.zeros_like(base_s)

    h = _rms(x_ref[...], g_ref[...]) * (1.0 + sc_ref[0]) + sh_ref[0]
    h_hi, h_mid, _ = _split3(h)
    whi = whi_ref[...]
    logits = _dot_nt(whi, h_hi) + _dot_nt(wlo_ref[...], h_hi) + _dot_nt(whi, h_mid)
    scores = jax.nn.sigmoid(logits)
    sel = scores + bias_ref[...]

    per = ne // N_GROUPS
    sel3 = sel.reshape(N_GROUPS, per, tm)
    ri = lax.broadcasted_iota(I32, sel3.shape, 1)
    m1, i1 = _first_argmax(sel3, ri, per, 1)
    m2 = jnp.max(jnp.where(ri == i1, neg, sel3), axis=1, keepdims=True)
    grp = (m1 + m2).reshape(N_GROUPS, tm)

    gi = lax.broadcasted_iota(I32, grp.shape, 0)
    chosen = jnp.zeros(grp.shape, F32)
    for _ in range(TOPK_GROUPS):
        _, ix = _first_argmax(grp, gi, N_GROUPS, 0)
        hit = gi == ix
        chosen = jnp.where(hit, 1.0, chosen)
        grp = jnp.where(hit, neg, grp)
    masked = jnp.where(chosen.reshape(N_GROUPS, 1, tm) > 0.0, sel3, neg).reshape(ne, tm)

    ei = lax.broadcasted_iota(I32, (ne, tm), 0)
    msel = jnp.zeros((ne, tm), F32)
    idx_rows, w_rows = [], []
    for _ in range(TOP_K):
        _, ix = _first_argmax(masked, ei, ne, 0)
        hit = ei == ix
        w_rows.append(jnp.sum(jnp.where(hit, scores, 0.0), axis=0, keepdims=True))
        idx_rows.append(ix)
        msel = jnp.where(hit, 1.0, msel)
        masked = jnp.where(hit, neg, masked)
    w = jnp.concatenate(w_rows, axis=0)
    eidx_ref[...] = jnp.concatenate(idx_rows, axis=0)
    wts_ref[...] = w / (jnp.sum(w, axis=0, keepdims=True) + 1e-20) * ROUTED_SCALE

    mb = msel.astype(BF16)
    a = lax.broadcasted_iota(I32, (tm, tm), 0)
    b = lax.broadcasted_iota(I32, (tm, tm), 1)
    before = _dot(mb, jnp.where(a < b, 1.0, 0.0).astype(BF16))
    pos = base_s[...] + before
    rank_ref[...] = jnp.concatenate(
        [jnp.sum(jnp.where(ei == ix, pos, 0.0), axis=0, keepdims=True) for ix in idx_rows], axis=0).astype(I32)
    base_s[...] = base_s[...] + _dot(mb, jnp.ones((tm, tm), BF16))
    cnt_ref[...] = base_s[:, :LANES].astype(I32)


def _router(x1, g2, sc2, sh2, wt_hi, wt_lo, bias, seq, tm=512):
    t, d = x1.shape
    ne = wt_hi.shape[0]
    tm = min(tm, seq)
    per = seq // tm
    bvec = pl.BlockSpec((1, 1, d), lambda i: (i // per, 0, 0))
    kt = lambda dt: jax.ShapeDtypeStruct((TOP_K, t), dt)
    kspec = pl.BlockSpec((TOP_K, tm), lambda i: (0, i))
    return pl.pallas_call(
        _router_kernel,
        out_shape=(kt(I32), kt(F32), kt(I32), jax.ShapeDtypeStruct((ne, LANES), I32)),
        grid=(t // tm,),
        in_specs=[pl.BlockSpec((tm, d), lambda i: (i, 0)),
                  pl.BlockSpec((1, d), lambda i: (0, 0)), bvec, bvec,
                  pl.BlockSpec((ne, d), lambda i: (0, 0)),
                  pl.BlockSpec((ne, d), lambda i: (0, 0)),
                  pl.BlockSpec((ne, tm), lambda i: (0, 0))],
        out_specs=(kspec, kspec, kspec, pl.BlockSpec((ne, LANES), lambda i: (0, 0))),
        scratch_shapes=[pltpu.VMEM((ne, tm), F32)],
        compiler_params=_cparams("arbitrary"),
        name="router",
    )(x1, g2, sc2, sh2, wt_hi, wt_lo, jnp.broadcast_to(bias.reshape(ne, 1), (ne, tm)))


def _rows_kernel(eidx_ref, rank_ref, pstart_ref, rows_ref):
    ne, tm = pstart_ref.shape
    ei = lax.broadcasted_iota(I32, (ne, tm), 0)
    ps = pstart_ref[...]
    eidx = eidx_ref[...]
    rows_ref[...] = rank_ref[...] + jnp.concatenate(
        [jnp.sum(jnp.where(ei == eidx[k:k + 1, :], ps, 0), axis=0, keepdims=True) for k in range(TOP_K)], axis=0)


def _pair_rows(eidx, rank, pstart, tm=512):
    t = eidx.shape[1]
    ne = pstart.shape[0]
    tm = min(tm, t)
    kspec = pl.BlockSpec((TOP_K, tm), lambda i: (0, i))
    return pl.pallas_call(
        _rows_kernel,
        out_shape=jax.ShapeDtypeStruct((TOP_K, t), I32),
        grid=(t // tm,),
        in_specs=[kspec, kspec, pl.BlockSpec((ne, tm), lambda i: (0, 0))],
        out_specs=kspec,
        compiler_params=_cparams("arbitrary"),
        name="pair_rows",
    )(eidx, rank, jnp.broadcast_to(pstart.reshape(ne, 1), (ne, tm)))


def _tile(ref, r, n=1):
    return ref.at[pl.ds(pl.multiple_of(r * SUBLANES, SUBLANES), n * SUBLANES), :]


def _token_copy_all(src_of, dst_of, sem, rows_ref, tm):
    def issue(t, carry):
        for k in range(TOP_K):
            row = rows_ref[k, t]
            pltpu.make_async_copy(src_of(k, t, row), dst_of(k, t, row), sem).start(priority=k % 2)
        return carry
    lax.fori_loop(0, tm, issue, 0)


def _swiglu_packed(h_ref, wg_ref, wu_ref, wd_ref):
    lo, hi = _unpack_halves(_load_token_tiles(h_ref))
    lo, hi = lo.astype(BF16), hi.astype(BF16)
    n = lo.shape[1]
    g = _dot(lo, wg_ref[:n, :]) + _dot(hi, wg_ref[n:, :])
    u = _dot(lo, wu_ref[:n, :]) + _dot(hi, wu_ref[n:, :])
    return _dot((g * jax.nn.sigmoid(g) * u).astype(BF16), wd_ref[...])


def _dispatch_kernel(pstart_ref, cnt_ref, nused_ref, h_ref, h_hbm, rows_ref, wg_ref, wu_ref, wd_ref,
                     xs_ref, shared_ref, zbuf, sem, zsem):
    i = pl.program_id(0)
    tm = h_ref.shape[0] // SUBLANES
    bm = zbuf.shape[0] // SUBLANES
    ne = cnt_ref.shape[0]
    nblk = xs_ref.shape[0] // (bm * SUBLANES)

    def zero_fill(act):
        def per_expert(e, carry):
            c = cnt_ref[e]
            pad = lax.rem(bm - lax.rem(c, bm), bm)
            off = pstart_ref[e] + c
            s = bm // 2
            while s >= 1:
                @pl.when((pad & s) != 0)
                def _(s=s, off=off):
                    act(pltpu.make_async_copy(_tile(zbuf, 0, s), _tile(xs_ref, off, s), zsem))
                off = off + (pad & s)
                s //= 2
            return carry
        lax.fori_loop(0, ne, per_expert, 0)

        def per_block(j, carry):
            act(pltpu.make_async_copy(zbuf, _tile(xs_ref, j * bm, bm), zsem))
            return carry
        lax.fori_loop(nused_ref[0], nblk, per_block, 0)

    @pl.when(i == 0)
    def _():
        zbuf[...] = jnp.zeros_like(zbuf)
        zero_fill(lambda cp: cp.start())

    _token_copy_all(lambda k, t, row: _tile(h_ref, t) if k % 2 == 0 else _tile(h_hbm, i * tm + t),
                    lambda k, t, row: _tile(xs_ref, row), sem, rows_ref, tm)
    shared_ref[...] = _swiglu_packed(h_ref, wg_ref, wu_ref, wd_ref).astype(BF16)
    for _ in range(TOP_K):
        pltpu.make_async_copy(h_ref, _tile(xs_ref, 0, tm), sem).wait()

    @pl.when(i == 0)
    def _():
        zero_fill(lambda cp: cp.wait())


def _dispatch(h2p, rows, pstart, counts, n_used, sh_wg, sh_wu, sh_wd, n_rows, bm, tm=256):
    t = h2p.shape[0] // SUBLANES
    d, f = sh_wg.shape
    tm = min(tm, t)
    const = lambda shp: pl.BlockSpec(shp, lambda i, *_: (0, 0), pipeline_mode=pl.Buffered(1))
    return pl.pallas_call(
        _dispatch_kernel,
        out_shape=(jax.ShapeDtypeStruct((n_rows * SUBLANES, LANES), U32), jax.ShapeDtypeStruct((t, d), BF16)),
        grid_spec=pltpu.PrefetchScalarGridSpec(
            num_scalar_prefetch=3,
            grid=(t // tm,),
            in_specs=[pl.BlockSpec((tm * SUBLANES, LANES), lambda i, *_: (i, 0)),
                      pl.BlockSpec(memory_space=pl.ANY),
                      pl.BlockSpec((TOP_K, tm), lambda i, *_: (0, i), memory_space=pltpu.SMEM),
                      const((d, f)), const((d, f)), const((f, d))],
            out_specs=(pl.BlockSpec(memory_space=pl.ANY), pl.BlockSpec((tm, d), lambda i, *_: (i, 0))),
            scratch_shapes=[pltpu.VMEM((bm * SUBLANES, LANES), U32), pltpu.SemaphoreType.DMA(()),
                            pltpu.SemaphoreType.DMA(())]),
        compiler_params=_cparams("arbitrary"),
        name="dispatch",
    )(pstart, counts, n_used, h2p, h2p, rows, sh_wg, sh_wu, sh_wd)


def _expert_kernel(be_ref, first_ref, nxt_ref, nxt2_ref, slot_ref, nu_ref, xs_hbm, wg_hbm, wu_hbm, wd_hbm, y_hbm,
                   xbuf, ybuf, wg_buf, wu_buf, wd_buf, xsem, ysem, wsem):
    j = pl.program_id(0)
    nblk = pl.num_programs(0)
    nu = nu_ref[0]
    bm = xbuf.shape[1] // SUBLANES
    cur = lax.rem(j, 2)
    nslot = EXPERT_WEIGHT_SLOTS

    def weight_copies(e, s):
        return [pltpu.make_async_copy(hbm.at[e], buf.at[s], wsem.at[s, i])
                for i, (hbm, buf) in enumerate(((wg_hbm, wg_buf), (wu_hbm, wu_buf), (wd_hbm, wd_buf)))]

    def x_copy(blk, s):
        return pltpu.make_async_copy(_tile(xs_hbm, blk * bm, bm), xbuf.at[s], xsem.at[s])

    def y_copy(blk, s):
        return pltpu.make_async_copy(ybuf.at[s], _tile(y_hbm, blk * bm, bm), ysem.at[s])

    @pl.when((j == 0) & (nu > 0))
    def _():
        x_copy(0, 0).start()
        for cp in weight_copies(be_ref[0], slot_ref[0]):
            cp.start()

        @pl.when(nxt_ref[0] >= 0)
        def _():
            for cp in weight_copies(nxt_ref[0], lax.rem(slot_ref[0] + 1, nslot)):
                cp.start()

    @pl.when(j < nu)
    def _():
        x_copy(j, cur).wait()

    @pl.when(j + 1 < nu)
    def _():
        x_copy(j + 1, 1 - cur).start()

    @pl.when((j < nu) & (first_ref[j] == 1))
    def _():
        for cp in weight_copies(be_ref[j], slot_ref[j]):
            cp.wait()

        @pl.when(nxt2_ref[j] >= 0)
        def _():
            for cp in weight_copies(nxt2_ref[j], lax.rem(slot_ref[j] + 2, nslot)):
                cp.start()

    @pl.when(j >= 2)
    def _():
        y_copy(j - 2, cur).wait()

    @pl.when(j < nu)
    def _():
        s = slot_ref[j]
        lo, hi = _unpack_halves(_load_token_tiles(xbuf.at[cur]))
        lo, hi = lo.astype(BF16), hi.astype(BF16)
        n = lo.shape[1]
        wg = wg_buf[s].astype(BF16)
        wu = wu_buf[s].astype(BF16)
        g = _dot(lo, wg[:n]) + _dot(hi, wg[n:])
        u = _dot(lo, wu[:n]) + _dot(hi, wu[n:])
        a = (g * jax.nn.sigmoid(g) * u).astype(BF16)
        _store_token_tiles(ybuf.at[cur], _pack_halves(_dot(a, wd_buf[s].astype(BF16))))

    @pl.when(j >= nu)
    def _():
        ybuf[cur] = jnp.zeros(ybuf.shape[1:], ybuf.dtype)

    y_copy(j, cur).start()

    @pl.when(j == nblk - 1)
    def _():
        @pl.when(j >= 1)
        def _():
            y_copy(j - 1, 1 - cur).wait()
        y_copy(j, cur).wait()


EXPERT_WEIGHT_SLOTS = 3


def _experts(xs, block_e, block_first, block_next, block_next2, block_slot, n_used, w_gate, w_up, w_down, bm):
    n_rows = xs.shape[0] // SUBLANES
    ne, d, f = w_gate.shape
    nblk = n_rows // bm
    blk = (2, bm * SUBLANES, LANES)
    ns = EXPERT_WEIGHT_SLOTS
    return pl.pallas_call(
        _expert_kernel,
        out_shape=jax.ShapeDtypeStruct(xs.shape, U32),
        grid_spec=pltpu.PrefetchScalarGridSpec(
            num_scalar_prefetch=6,
            grid=(nblk,),
            in_specs=[pl.BlockSpec(memory_space=pl.ANY)] * 4,
            out_specs=pl.BlockSpec(memory_space=pl.ANY),
            scratch_shapes=[pltpu.VMEM(blk, U32), pltpu.VMEM(blk, U32),
                            pltpu.VMEM((ns, d, f), F32), pltpu.VMEM((ns, d, f), F32), pltpu.VMEM((ns, f, d), F32),
                            pltpu.SemaphoreType.DMA((2,)), pltpu.SemaphoreType.DMA((2,)),
                            pltpu.SemaphoreType.DMA((ns, 3))]),
        compiler_params=_cparams("arbitrary"),
        name="experts",
    )(block_e, block_first, block_next, block_next2, block_slot, n_used, xs, w_gate, w_up, w_down)


def _final_kernel(n_first, x1_ref, sh_ref, gt_ref, wts_ref, fg_ref, rows_ref, rows_nx_ref, y_ref,
                  oa_ref, ob_ref, ybuf, sems):
    i = pl.program_id(0)
    tm = x1_ref.shape[0]
    n = x1_ref.shape[1] // 2
    slot = lax.rem(i, 2)

    def gather(r_ref, s):
        _token_copy_all(lambda k, t, row: _tile(y_ref, row), lambda k, t, row: _tile(ybuf.at[s, k], t),
                        sems.at[s], r_ref, tm)

    pl.when(i == 0)(lambda: gather(rows_ref, 0))
    pl.when(i + 1 < pl.num_programs(0))(lambda: gather(rows_nx_ref, 1 - slot))

    for k in range(TOP_K):
        pltpu.make_async_copy(_tile(y_ref, 0, tm), ybuf.at[slot, k], sems.at[slot]).wait()
    shared = sh_ref[...].astype(F32)
    acc_lo, acc_hi = shared[:, :n], shared[:, n:]
    wts = wts_ref[...].T
    for k in range(TOP_K):
        ylo, yhi = _unpack_halves(_load_token_tiles(ybuf.at[slot, k]))
        wk = wts[:, k:k + 1]
        acc_lo = acc_lo + wk * ylo
        acc_hi = acc_hi + wk * yhi
    gt = gt_ref[0]
    x_lo = x1_ref[:, :n] + gt[:, :n] * acc_lo
    x_hi = x1_ref[:, n:] + gt[:, n:] * acc_hi
    ms = (jnp.sum(x_lo * x_lo, axis=-1, keepdims=True) + jnp.sum(x_hi * x_hi, axis=-1, keepdims=True)) / (2 * n)
    r = lax.rsqrt(ms + RMS_EPS)

    def write(o_ref):
        o_ref[:, :n] = x_lo * r * fg_ref[:, :n]
        o_ref[:, n:] = x_hi * r * fg_ref[:, n:]

    pl.when(i < n_first)(lambda: write(oa_ref))
    pl.when(i >= n_first)(lambda: write(ob_ref))


def _final(x1, shared, gt2, wts, final_g, rows, y, rows_first, seq, tm=256):
    t, d = x1.shape
    tm = min(tm, seq)
    per = seq // tm
    nt = t // tm
    n_first = rows_first // tm
    out_a, out_b = _two_trunk_specs((tm, d), n_first)
    return pl.pallas_call(
        functools.partial(_final_kernel, n_first),
        out_shape=(jax.ShapeDtypeStruct((rows_first, d), F32), jax.ShapeDtypeStruct((t - rows_first, d), F32)),
        grid=(nt,),
        in_specs=[pl.BlockSpec((tm, d), lambda i: (i, 0)),
                  pl.BlockSpec((tm, d), lambda i: (i, 0)),
                  pl.BlockSpec((1, 1, d), lambda i: (i // per, 0, 0)),
                  pl.BlockSpec((TOP_K, tm), lambda i: (0, i)),
                  pl.BlockSpec((1, d), lambda i: (0, 0)),
                  pl.BlockSpec((TOP_K, tm), lambda i: (0, i), memory_space=pltpu.SMEM),
                  pl.BlockSpec((TOP_K, tm), lambda i: (0, jnp.minimum(i + 1, nt - 1)), memory_space=pltpu.SMEM),
                  pl.BlockSpec(memory_space=pl.ANY)],
        out_specs=(out_a, out_b),
        scratch_shapes=[pltpu.VMEM((2, TOP_K, tm * SUBLANES, LANES), U32), pltpu.SemaphoreType.DMA((2,))],
        compiler_params=_cparams("arbitrary"),
        name="final",
    )(x1, shared, gt2, wts, final_g, rows, rows, y)


EXPERT_ROWS = 256


def _prep_moe(w):
    rw = w["router_w"][0]
    d = rw.shape[0]
    wt = rw.T
    wt_hi = wt.astype(BF16)
    return dict(
        wt_hi=wt_hi, wt_lo=(wt - wt_hi.astype(F32)).astype(BF16), bias=w["router_bias"][0],
        w_gate=w["exp_w_gate"][0], w_up=w["exp_w_up"][0], w_down=w["exp_w_down"][0],
        sh_wg=w["sh_w_gate"][0].astype(BF16), sh_wu=w["sh_w_up"][0].astype(BF16),
        sh_wd=w["sh_w_down"][0].astype(BF16), final_g=w["final_g"].reshape(1, d),
    )


def _moe_stage(x1, h2p, mods, pm, pw, seq, rows_first):
    _, _, _, sh2, sc2, gt2 = mods
    t = x1.shape[0]
    ne = pw["wt_hi"].shape[0]
    bm = EXPERT_ROWS
    eidx, wts, rank, cnt = _router(x1, pm["norm_ffn_g"], sc2, sh2, pw["wt_hi"], pw["wt_lo"], pw["bias"], seq)
    counts = cnt[:, 0]
    pcounts = (counts + bm - 1) // bm * bm
    pends = jnp.cumsum(pcounts)
    pstart = (pends - pcounts).astype(I32)
    nblk = (t * TOP_K + ne * (bm - 1) + bm - 1) // bm
    n_used = (pends[-1:] // bm).astype(I32)
    jb = jnp.arange(nblk, dtype=I32)
    be = jnp.minimum(jnp.sum((pends[None, :] <= (jb * bm)[:, None]).astype(I32), axis=1), ne - 1)
    ids = jnp.arange(ne, dtype=I32)
    onehot = be[:, None] == ids[None, :]
    look = lambda table: jnp.sum(jnp.where(onehot, table[None, :], 0), axis=1).astype(I32)
    first = (jb * bm == look(pstart)).astype(I32)
    live = counts > 0
    ordinal = jnp.cumsum(live.astype(I32)) - 1
    after = lax.cummin(jnp.where(live, ids, ne), reverse=True)
    next_live = jnp.concatenate([after[1:], jnp.full((1,), ne, I32)])
    nxt = look(jnp.where(next_live < ne, next_live, -1))
    next2 = jnp.concatenate([next_live, jnp.full((1,), ne, I32)])[next_live]
    nxt2 = look(jnp.where(next2 < ne, next2, -1))
    slot = look(ordinal) % EXPERT_WEIGHT_SLOTS
    rows = _pair_rows(eidx, rank, pstart)
    xs, shared = _dispatch(h2p, rows, pstart, counts, n_used, pw["sh_wg"], pw["sh_wu"], pw["sh_wd"],
                           nblk * bm, bm)
    y = _experts(xs, be, first, nxt, nxt2, slot, n_used, pw["w_gate"], pw["w_up"], pw["w_down"], bm)
    return _final(x1, shared, gt2, wts, pw["final_g"], rows, y, rows_first, seq)


def kernel(x_prompt, x_sample, c_prompt, c_sample, ada_w, ada_b, norm_mix_g, norm_ffn_g, w_in, sgu_ln_g,
           sgu_ln_b, sgu_ws, sgu_b, dn_conv_w, dn_a_log, dn_dt_bias, dn_norm_g, w_branch, w_out, router_w,
           router_bias, exp_w_gate, exp_w_up, exp_w_down, sh_w_gate, sh_w_up, sh_w_down, final_g):
    w = dict(ada_w=ada_w, ada_b=ada_b, norm_mix_g=norm_mix_g, norm_ffn_g=norm_ffn_g, w_in=w_in,
             sgu_ln_g=sgu_ln_g, sgu_ln_b=sgu_ln_b, sgu_ws=sgu_ws, sgu_b=sgu_b, dn_conv_w=dn_conv_w,
             dn_a_log=dn_a_log, dn_dt_bias=dn_dt_bias, dn_norm_g=dn_norm_g, w_branch=w_branch, w_out=w_out,
             router_w=router_w, router_bias=router_bias, exp_w_gate=exp_w_gate, exp_w_up=exp_w_up,
             exp_w_down=exp_w_down, sh_w_gate=sh_w_gate, sh_w_up=sh_w_up, sh_w_down=sh_w_down, final_g=final_g)
    assert w_in.shape[0] == 1, "one layer"
    bp, seq, d = x_prompt.shape
    bs = x_sample.shape[0]
    assert x_sample.shape[1] == seq
    nb = bp + bs
    xa = x_prompt.reshape(bp * seq, d)
    xb = x_sample.reshape(bs * seq, d)
    c = jnp.concatenate([c_prompt, c_sample], axis=0)
    npad = -nb % 8
    mod = _ada(jnp.pad(c, ((0, npad), (0, 0))), ada_w[0], ada_b[0])[:nb]
    mods = [m.reshape(nb, 1, d) for m in jnp.split(mod, 6, axis=-1)]
    pm = _prep_mixer(w, nb, seq)
    pw = _prep_moe(w)
    x1, h2p = _token_mixer_stage(xa, xb, mods, pm)
    ya, yb = _moe_stage(x1, h2p, mods, pm, pw, seq, bp * seq)
    return (ya.reshape(bp, seq, d), yb.reshape(bs, seq, d))
```

```python
import functools

import jax
import jax.numpy as jnp
import numpy as np
from jax import lax
from jax.experimental import pallas as pl
from jax.experimental.pallas import tpu as pltpu

F32 = jnp.float32
BF16 = jnp.bfloat16
I32 = jnp.int32
U32 = jnp.uint32

RMS_EPS = 1e-6
LN_EPS = 1e-5
L2_EPS = 1e-6

SGU_GROUPS = 8
SGU_CHUNK = 128
DN_HEADS = 8
DN_HEAD_DIM = 128
DN_CHUNK = 256
DN_PREP_UNROLL = 2
CONV_WIDTH = 5
TOP_K = 8
N_GROUPS = 8
TOPK_GROUPS = 4
ROUTED_SCALE = 2.5

LANES = 128
SUBLANES = 8
VMEM_LIMIT = 56 * 1024 * 1024


def _cparams(*sem):
    return pltpu.CompilerParams(dimension_semantics=sem, vmem_limit_bytes=VMEM_LIMIT)


def _split3(x):
    hi = x.astype(BF16)
    r = x - hi.astype(F32)
    mid = r.astype(BF16)
    lo = (r - mid.astype(F32)).astype(BF16)
    return hi, mid, lo


def _dot(a, b):
    return jnp.dot(a, b, preferred_element_type=F32)


def _dot_nt(a, b):
    return lax.dot_general(a, b, (((1,), (1,)), ((), ())), preferred_element_type=F32)


def _dot_tn(a, b):
    return lax.dot_general(a, b, (((0,), (0,)), ((), ())), preferred_element_type=F32)


def _rms(x, g):
    return x * lax.rsqrt(jnp.mean(x * x, axis=-1, keepdims=True) + RMS_EPS) * g


def _pack_halves(x):
    n = x.shape[1] // 2
    return pltpu.pack_elementwise([x[:, :n], x[:, n:]], packed_dtype=BF16)


def _unpack_halves(w):
    lo = pltpu.unpack_elementwise(w, index=0, packed_dtype=BF16, unpacked_dtype=F32)
    hi = pltpu.unpack_elementwise(w, index=1, packed_dtype=BF16, unpacked_dtype=F32)
    return lo, hi


def _store_token_tiles(ref, x):
    m = x.shape[0]
    for s in range(SUBLANES):
        ref[pl.ds(s, m, stride=SUBLANES), :] = x[:, s * LANES:(s + 1) * LANES]


def _load_token_tiles(ref):
    m = ref.shape[0] // SUBLANES
    return jnp.concatenate([ref[pl.ds(s, m, stride=SUBLANES), :] for s in range(SUBLANES)], axis=1)


def _ada_kernel(c_ref, w_ref, b_ref, o_ref):
    c = c_ref[...]
    a = c * jax.nn.sigmoid(c)
    a_hi, a_mid, _ = _split3(a)
    w = w_ref[...]
    w_hi = w.astype(BF16)
    w_lo = (w - w_hi.astype(F32)).astype(BF16)
    o_ref[...] = _dot(a_hi, w_hi) + _dot(a_mid, w_hi) + _dot(a_hi, w_lo) + b_ref[...]


def _ada(c, ada_w, ada_b, tn=1024):
    nb, d = c.shape
    n = ada_w.shape[1]
    return pl.pallas_call(
        _ada_kernel,
        out_shape=jax.ShapeDtypeStruct((nb, n), F32),
        grid=(n // tn,),
        in_specs=[pl.BlockSpec((nb, d), lambda j: (0, 0)),
                  pl.BlockSpec((d, tn), lambda j: (0, j)),
                  pl.BlockSpec((1, tn), lambda j: (0, j))],
        out_specs=pl.BlockSpec((nb, tn), lambda j: (0, j)),
        compiler_params=_cparams("arbitrary"),
        name="ada",
    )(c, ada_w, ada_b.reshape(1, n))


def _two_trunk_specs(block, n_first):
    first = pl.BlockSpec(block, lambda i, *_: (jnp.minimum(i, n_first - 1), 0))
    second = pl.BlockSpec(block, lambda i, *_: (jnp.maximum(i - n_first, 0), 0))
    return first, second


def _inproj_kernel(n_first, xa_ref, xb_ref, g_ref, sc_ref, sh_ref, w_ref, wab_ref, z_ref, ab_ref, h_s):
    i = pl.program_id(0)
    j = pl.program_id(1)

    def prologue(x_ref):
        h = _rms(x_ref[...], g_ref[...]) * (1.0 + sc_ref[0]) + sh_ref[0]
        h_s[...] = h.astype(BF16)
        ab_ref[...] = _dot(h_s[...], wab_ref[...])

    pl.when((j == 0) & (i < n_first))(lambda: prologue(xa_ref))
    pl.when((j == 0) & (i >= n_first))(lambda: prologue(xb_ref))
    z_ref[...] = _dot(h_s[...], w_ref[...]).astype(BF16)


def _inproj(xa, xb, g, sc, sh, w_main, w_ab, seq, tm=512, tn=2048):
    d = xa.shape[1]
    t = xa.shape[0] + xb.shape[0]
    n = w_main.shape[1]
    tm = min(tm, seq)
    per = seq // tm
    n_first = xa.shape[0] // tm
    spec_a, spec_b = _two_trunk_specs((tm, d), n_first)
    return pl.pallas_call(
        functools.partial(_inproj_kernel, n_first),
        out_shape=(jax.ShapeDtypeStruct((t, n), BF16), jax.ShapeDtypeStruct((t, LANES), F32)),
        grid=(t // tm, n // tn),
        in_specs=[spec_a, spec_b,
                  pl.BlockSpec((1, d), lambda i, j: (0, 0)),
                  pl.BlockSpec((1, 1, d), lambda i, j: (i // per, 0, 0)),
                  pl.BlockSpec((1, 1, d), lambda i, j: (i // per, 0, 0)),
                  pl.BlockSpec((d, tn), lambda i, j: (0, j)),
                  pl.BlockSpec((d, LANES), lambda i, j: (0, 0))],
        out_specs=(pl.BlockSpec((tm, tn), lambda i, j: (i, j)),
                   pl.BlockSpec((tm, LANES), lambda i, j: (i, 0))),
        scratch_shapes=[pltpu.VMEM((tm, d), BF16)],
        compiler_params=_cparams("arbitrary", "arbitrary"),
        name="inproj",
    )(xa, xb, g, sc, sh, w_main, w_ab)


def _sgu_kernel(u_ref, v_ref, g_ref, b_ref, ws_ref, sb_ref, o_ref):
    v = jax.nn.gelu(v_ref[...].astype(F32))
    mu = jnp.mean(v, axis=-1, keepdims=True)
    vc = v - mu
    var = jnp.mean(vc * vc, axis=-1, keepdims=True)
    vn = (vc * lax.rsqrt(var + LN_EPS) * g_ref[...] + b_ref[...]).astype(BF16)
    tm = vn.shape[0]
    for c in range(tm // SGU_CHUNK):
        r = slice(c * SGU_CHUNK, (c + 1) * SGU_CHUNK)
        for gi in range(SGU_GROUPS):
            l = slice(gi * LANES, (gi + 1) * LANES)
            mixed = _dot(ws_ref[gi], vn[r, l]) + sb_ref[gi]
            o_ref[r, l] = (jax.nn.gelu(u_ref[r, l].astype(F32)) * mixed).astype(BF16)


def _sgu(z, ln_g, ln_b, ws, sb_b, tm=512):
    t = z.shape[0]
    w = SGU_GROUPS * LANES
    tm = np.gcd(tm, t)
    return pl.pallas_call(
        _sgu_kernel,
        out_shape=jax.ShapeDtypeStruct((t, w), BF16),
        grid=(t // tm,),
        in_specs=[pl.BlockSpec((tm, w), lambda i: (i, 0)),
                  pl.BlockSpec((tm, w), lambda i: (i, 1)),
                  pl.BlockSpec((1, w), lambda i: (0, 0)),
                  pl.BlockSpec((1, w), lambda i: (0, 0)),
                  pl.BlockSpec((SGU_GROUPS, SGU_CHUNK, SGU_CHUNK), lambda i: (0, 0, 0)),
                  pl.BlockSpec((SGU_GROUPS, SGU_CHUNK, LANES), lambda i: (0, 0, 0))],
        out_specs=pl.BlockSpec((tm, w), lambda i: (i, 0)),
        compiler_params=_cparams("arbitrary"),
        name="sgu",
    )(z, z, ln_g, ln_b, ws, sb_b)


def _gate_kernel(ab_ref, alog_ref, dt_ref, gcol_ref, grow_ref):
    ab = ab_ref[...]
    tr = ab.shape[0]
    lane = lax.broadcasted_iota(I32, ab.shape, 1)
    z = ab + dt_ref[...]
    softplus = jnp.maximum(z, 0.0) + jnp.log1p(jnp.exp(-jnp.abs(z)))
    g = -jnp.exp(alog_ref[...]) * softplus
    beta = jax.nn.sigmoid(ab)
    nh = DN_HEADS
    g = jnp.where(lane < 2 * nh, g, 0.0)
    i = lax.broadcasted_iota(I32, (tr, tr), 0)
    j = lax.broadcasted_iota(I32, (tr, tr), 1)
    same = (i // DN_CHUNK) == (j // DN_CHUNK)
    lower = jnp.where(same & (j <= i), 1.0, 0.0).astype(BF16)
    upper = jnp.where(same & (j >= i), 1.0, 0.0).astype(BF16)
    g_hi, g_mid, g_lo = _split3(g)
    pre = _dot(lower, g_hi) + _dot(lower, g_mid) + _dot(lower, g_lo)
    suf = _dot(upper, g_hi) + _dot(upper, g_mid) + _dot(upper, g_lo)
    out = jnp.where(lane < nh, pre, jnp.where(lane < 2 * nh, suf, jnp.where(lane < 4 * nh, beta, 0.0)))
    gcol_ref[...] = out
    grow_ref[0] = out.T[:2 * nh, :]


def _gates(ab, alog_row, dt_row, nb, seq, tr=512):
    t = ab.shape[0]
    per = seq // tr
    return pl.pallas_call(
        _gate_kernel,
        out_shape=(jax.ShapeDtypeStruct((t, LANES), F32),
                   jax.ShapeDtypeStruct((nb, 2 * DN_HEADS, seq), F32)),
        grid=(t // tr,),
        in_specs=[pl.BlockSpec((tr, LANES), lambda i: (i, 0)),
                  pl.BlockSpec((1, LANES), lambda i: (0, 0)),
                  pl.BlockSpec((1, LANES), lambda i: (0, 0))],
        out_specs=(pl.BlockSpec((tr, LANES), lambda i: (i, 0)),
                   pl.BlockSpec((1, 2 * DN_HEADS, tr), lambda i: (i // per, 0, i % per))),
        compiler_params=_cparams("arbitrary"),
        name="gates",
    )(ab, alog_row, dt_row)


def _conv_silu(x_ref, w_ref):
    x = x_ref[...].astype(F32)
    s = x.shape[0]
    row = lax.broadcasted_iota(I32, x.shape, 0)
    pad = (CONV_WIDTH - 1) // 2
    acc = x * w_ref[pad:pad + 1, :]
    for j in range(CONV_WIDTH):
        d = j - pad
        if d == 0:
            continue
        xs = pltpu.roll(x, (-d) % s, 0)
        ok = (row + d >= 0) & (row + d < s)
        acc = acc + jnp.where(ok, xs, 0.0) * w_ref[j:j + 1, :]
    return acc * jax.nn.sigmoid(acc)


def _l2n(x):
    return x * lax.rsqrt(jnp.sum(x * x, axis=-1, keepdims=True) + L2_EPS)


def _unit_tri_inverses(mats, i, j):
    c = mats[0].shape[0]
    x = i ^ j
    eye = jnp.where(i == j, 1.0, 0.0)
    tinv = [eye - jnp.where(x < 2, a, 0.0) for a in mats]
    s = 2
    while s < c:
        level = (x >= s) & (x < 2 * s)
        tb = [t.astype(BF16) for t in tinv]
        m = [_dot(t, jnp.where(level, a, 0.0).astype(BF16)).astype(BF16) for t, a in zip(tb, mats)]
        tinv = [t - _dot(mm, b) for t, mm, b in zip(tinv, m, tb)]
        s *= 2
    return tinv


def _dn_prep(items):
    c, hd = items[0][1].shape
    i = lax.broadcasted_iota(I32, (c, c), 0)
    j = lax.broadcasted_iota(I32, (c, c), 1)
    decs, kbs, kbfs = [], [], []
    for forward, q, k, v, gam, grow, beta in items:
        incl = (i >= j) if forward else (i <= j)
        gi = jnp.concatenate([gam] * (c // LANES), axis=1)
        decs.append(jnp.where(incl, jnp.exp(gi - grow), 0.0))
        kbs.append(k * beta)
        kbfs.append(k.astype(BF16))
    kk = [_dot_nt(kb.astype(BF16), kbf) for kb, kbf in zip(kbs, kbfs)]
    mats = [jnp.where((i > j) if it[0] else (i < j), m * dec, 0.0) for it, m, dec in zip(items, kk, decs)]
    tinvs = _unit_tri_inverses(mats, i, j)
    egs = [jnp.exp(it[4]) for it in items]
    uws = [_dot(t.astype(BF16), jnp.concatenate([it[3] * it[6], kb * eg], axis=1).astype(BF16))
           for t, it, kb, eg in zip(tinvs, items, kbs, egs)]
    attns = [_dot_nt(it[1].astype(BF16), kbf) * dec for it, kbf, dec in zip(items, kbfs, decs)]
    outs = []
    for (forward, q, k, v, gam, grow, beta), uw, eg, attn in zip(items, uws, egs, attns):
        glast = gam[c - 1:c, :] if forward else gam[0:1, :]
        kd = k * jnp.exp(glast - gam)
        outs.append((uw[:, :hd], uw[:, hd:].astype(BF16), (q * eg).astype(BF16), kd.T.astype(BF16),
                     attn.astype(BF16)))
    return outs


def _dn_kernel(q_ref, k_ref, v_ref, og_ref, wq_ref, wk_ref, wv_ref, gcol_ref, grow_ref, ng_ref, o_ref,
               q_s, k_s, v_s, gam_s, beta_s, u_s, w_s, qd_s, kdt_s, attn_s, o_s):
    h = pl.program_id(1)
    nh = DN_HEADS
    hd = DN_HEAD_DIM
    q_s[...] = _l2n(_conv_silu(q_ref, wq_ref)) * (hd ** -0.5)
    k_s[...] = _l2n(_conv_silu(k_ref, wk_ref))
    v_s[...] = _conv_silu(v_ref, wv_ref)

    g_hi, g_mid, g_lo = _split3(gcol_ref[...])
    rr = lax.broadcasted_iota(I32, (LANES, 2 * LANES), 0)
    cc = lax.broadcasted_iota(I32, (LANES, 2 * LANES), 1)
    sel = lambda off: jnp.where(rr == h + off + jnp.where(cc < LANES, 0, nh), 1.0, 0.0).astype(BF16)
    sel_g = sel(0)
    gam2 = (_dot(jnp.concatenate([g_hi, g_mid], axis=1), jnp.concatenate([sel_g, sel_g], axis=0))
            + _dot(g_lo, sel_g))
    beta2 = _dot(g_hi, sel(2 * nh))
    for d in range(2):
        gam_s[d] = gam2[:, d * LANES:(d + 1) * LANES]
        beta_s[d] = beta2[:, d * LANES:(d + 1) * LANES]

    s = q_s.shape[0]
    c = DN_CHUNK
    n = s // c

    def prep(it, carry):
        where, items = [], []
        for sub in range(DN_PREP_UNROLL):
            rows = pl.ds(pl.multiple_of((it * DN_PREP_UNROLL + sub) * c, c), c)
            for d in range(2):
                where.append((d, rows))
                items.append((d == 0, q_s[rows, :], k_s[rows, :], v_s[rows, :], gam_s[d, rows, :],
                              grow_ref[0, pl.ds(d * nh + h, 1), rows], beta_s[d, rows, :]))
        for (d, rows), (u, w, qd, kdt, attn) in zip(where, _dn_prep(items)):
            u_s[d, rows, :] = u
            w_s[d, rows, :] = w
            qd_s[d, rows, :] = qd
            kdt_s[d, :, rows] = kdt
            attn_s[d, rows, :] = attn
        return carry

    lax.fori_loop(0, n // DN_PREP_UNROLL, prep, 0)

    def scan(it, states):
        r0s = [pl.multiple_of((it if d == 0 else n - 1 - it) * c, c) for d in range(2)]
        rows = [pl.ds(r0, c) for r0 in r0s]
        rs = [_dot(jnp.concatenate([w_s[d, rows[d], :], qd_s[d, rows[d], :]], axis=0), states[d].astype(BF16))
              for d in range(2)]
        vbs = [(u_s[d, rows[d], :] - rs[d][:c]).astype(BF16) for d in range(2)]
        for d in range(2):
            o_s[d, rows[d], :] = rs[d][c:] + _dot(attn_s[d, rows[d], :], vbs[d])
        new = []
        for d in range(2):
            glast = gam_s[d, pl.ds(r0s[d] + (c - 1 if d == 0 else 0), 1), :]
            new.append(states[d] * jnp.exp(glast) + _dot(kdt_s[d, :, rows[d]], vbs[d]))
        return tuple(new)

    zero = jnp.zeros((hd, hd), F32)
    lax.fori_loop(0, n, scan, (zero, zero))

    o = o_s[0] + o_s[1]
    og = og_ref[...].astype(F32)
    o_ref[...] = (_rms(o, ng_ref[...]) * (og * jax.nn.sigmoid(og))).astype(BF16)


def _deltanet(z, conv_w, gcol, grow, norm_g, nb, seq, qkv_blk, og_blk):
    t = z.shape[0]
    nh, hd, c = DN_HEADS, DN_HEAD_DIM, DN_CHUNK
    assert seq % (c * DN_PREP_UNROLL) == 0
    zspec = lambda off: pl.BlockSpec((seq, hd), lambda b, h: (b, off + h))
    wspec = lambda off: pl.BlockSpec((CONV_WIDTH, hd), lambda b, h: (0, off + h))
    return pl.pallas_call(
        _dn_kernel,
        out_shape=jax.ShapeDtypeStruct((t, nh * hd), BF16),
        grid=(nb, nh),
        in_specs=[zspec(qkv_blk), zspec(qkv_blk + nh), zspec(qkv_blk + 2 * nh), zspec(og_blk),
                  wspec(0), wspec(nh), wspec(2 * nh),
                  pl.BlockSpec((seq, LANES), lambda b, h: (b, 0)),
                  pl.BlockSpec((1, 2 * nh, seq), lambda b, h: (b, 0, 0)),
                  pl.BlockSpec((1, hd), lambda b, h: (0, 0))],
        out_specs=pl.BlockSpec((seq, hd), lambda b, h: (b, h)),
        scratch_shapes=[pltpu.VMEM((seq, hd), F32)] * 3
                       + [pltpu.VMEM((2, seq, hd), F32)] * 3
                       + [pltpu.VMEM((2, seq, hd), BF16)] * 2
                       + [pltpu.VMEM((2, hd, seq), BF16), pltpu.VMEM((2, seq, c), BF16),
                          pltpu.VMEM((2, seq, hd), F32)],
        compiler_params=_cparams("arbitrary", "arbitrary"),
        name="deltanet",
    )(z, z, z, z, conv_w, conv_w, conv_w, gcol, grow, norm_g)


def _merge_kernel(n_first, a_ref, b_ref, ga_ref, gb_ref, xa_ref, xb_ref, gt_ref, wa_ref, wb_ref, wo_ref, g2_ref,
                  sc_ref, sh_ref, x1_ref, h2_ref):
    i = pl.program_id(0)
    ya = _dot(a_ref[...], wa_ref[...])
    yb = _dot(b_ref[...], wb_ref[...])
    m = jax.nn.sigmoid(ga_ref[...].astype(F32)) * ya + jax.nn.sigmoid(gb_ref[...].astype(F32)) * yb
    y = _dot(m.astype(BF16), wo_ref[...])

    def epilogue(x_ref):
        x1 = x_ref[...] + gt_ref[0] * y
        x1_ref[...] = x1
        h2 = _rms(x1, g2_ref[...]) * (1.0 + sc_ref[0]) + sh_ref[0]
        _store_token_tiles(h2_ref, _pack_halves(h2))

    pl.when(i < n_first)(lambda: epilogue(xa_ref))
    pl.when(i >= n_first)(lambda: epilogue(xb_ref))


def _merge(a_out, b_out, z, xa, xb, gt1, wa, wb, wo, g2, sc2, sh2, seq, gate_blk, tm=256):
    d = xa.shape[1]
    t = xa.shape[0] + xb.shape[0]
    wdt = a_out.shape[1]
    tm = min(tm, seq)
    per = seq // tm
    n_first = xa.shape[0] // tm
    spec_a, spec_b = _two_trunk_specs((tm, d), n_first)
    bvec = pl.BlockSpec((1, 1, d), lambda i: (i // per, 0, 0))
    const = lambda shp: pl.BlockSpec(shp, lambda i: (0, 0), pipeline_mode=pl.Buffered(1))
    return pl.pallas_call(
        functools.partial(_merge_kernel, n_first),
        out_shape=(jax.ShapeDtypeStruct((t, d), F32), jax.ShapeDtypeStruct((t * SUBLANES, LANES), U32)),
        grid=(t // tm,),
        in_specs=[pl.BlockSpec((tm, wdt), lambda i: (i, 0)),
                  pl.BlockSpec((tm, wdt), lambda i: (i, 0)),
                  pl.BlockSpec((tm, d), lambda i: (i, gate_blk)),
                  pl.BlockSpec((tm, d), lambda i: (i, gate_blk + 1)),
                  spec_a, spec_b,
                  bvec, const((wdt, d)), const((wdt, d)), const((d, d)),
                  pl.BlockSpec((1, d), lambda i: (0, 0)), bvec, bvec],
        out_specs=(pl.BlockSpec((tm, d), lambda i: (i, 0)),
                   pl.BlockSpec((tm * SUBLANES, LANES), lambda i: (i, 0))),
        compiler_params=_cparams("arbitrary"),
        name="merge",
    )(a_out, b_out, z, z, xa, xb, gt1, wa, wb, wo, g2, sc2, sh2)


def _prep_mixer(w, nb, seq):
    wdt = SGU_GROUPS * LANES
    nh, hd = DN_HEADS, DN_HEAD_DIM
    w_in = w["w_in"][0]
    d = w_in.shape[0]
    ab0 = 2 * wdt + 4 * nh * hd
    w_main = jnp.concatenate([w_in[:, :ab0], w_in[:, ab0 + 4 * nh:]], axis=1).astype(BF16)
    w_ab = jnp.pad(w_in[:, ab0:ab0 + 4 * nh], ((0, 0), (0, LANES - 4 * nh))).astype(BF16)
    pad_row = lambda v: jnp.pad(v.reshape(1, 2 * nh), ((0, 0), (0, LANES - 2 * nh)))
    sb = w["sgu_b"][0]
    wbr = w["w_branch"][0]
    return dict(
        nb=nb, seq=seq,
        norm_mix_g=w["norm_mix_g"][0].reshape(1, d), norm_ffn_g=w["norm_ffn_g"][0].reshape(1, d),
        w_main=w_main, w_ab=w_ab,
        sgu_ln_g=w["sgu_ln_g"][0].reshape(1, wdt), sgu_ln_b=w["sgu_ln_b"][0].reshape(1, wdt),
        sgu_ws=w["sgu_ws"][0].astype(BF16),
        sgu_sb=jnp.broadcast_to(sb[:, :, None], sb.shape + (LANES,)),
        alog_row=pad_row(w["dn_a_log"][0]), dt_row=pad_row(w["dn_dt_bias"][0]),
        dn_conv_w=w["dn_conv_w"][0], dn_norm_g=w["dn_norm_g"][0].reshape(1, hd),
        wa=wbr[:wdt].astype(BF16), wb=wbr[wdt:].astype(BF16), wo=w["w_out"][0].astype(BF16),
    )


def _token_mixer_stage(xa, xb, mods, p):
    nb, seq = p["nb"], p["seq"]
    d = xa.shape[1]
    sh1, sc1, gt1, sh2, sc2, _ = mods
    wdt = SGU_GROUPS * LANES
    nh = DN_HEADS
    z, ab = _inproj(xa, xb, p["norm_mix_g"], sc1, sh1, p["w_main"], p["w_ab"], seq)
    a_out = _sgu(z, p["sgu_ln_g"], p["sgu_ln_b"], p["sgu_ws"], p["sgu_sb"])
    gcol, grow = _gates(ab, p["alog_row"], p["dt_row"], nb, seq)
    qkv_blk = 2 * wdt // LANES
    b_out = _deltanet(z, p["dn_conv_w"], gcol, grow, p["dn_norm_g"], nb, seq,
                      qkv_blk=qkv_blk, og_blk=qkv_blk + 3 * nh)
    gate_blk = (2 * wdt + 4 * nh * DN_HEAD_DIM) // d
    return _merge(a_out, b_out, z, xa, xb, gt1, p["wa"], p["wb"], p["wo"], p["norm_ffn_g"], sc2, sh2, seq,
                  gate_blk)


def _first_argmax(vals, idx, n, axis):
    mx = jnp.max(vals, axis=axis, keepdims=True)
    ix = jnp.min(jnp.where(vals == mx, idx, n), axis=axis, keepdims=True)
    return mx, ix


def _router_kernel(x_ref, g_ref, sc_ref, sh_ref, whi_ref, wlo_ref, bias_ref,
                   eidx_ref, wts_ref, rank_ref, cnt_ref, base_s):
    i = pl.program_id(0)
    ne, tm = bias_ref.shape
    neg = -jnp.inf

    @pl.when(i == 0)
    def _():
        base_s[...] = jnp.zeros_like(base_s)

    h = _rms(x_ref[...], g_ref[...]) * (1.0 + sc_ref[0]) + sh_ref[0]
    h_hi, h_mid, _ = _split3(h)
    whi = whi_ref[...]
    logits = _dot_nt(whi, h_hi) + _dot_nt(wlo_ref[...], h_hi) + _dot_nt(whi, h_mid)
    scores = jax.nn.sigmoid(logits)
    sel = scores + bias_ref[...]

    per = ne // N_GROUPS
    sel3 = sel.reshape(N_GROUPS, per, tm)
    ri = lax.broadcasted_iota(I32, sel3.shape, 1)
    m1, i1 = _first_argmax(sel3, ri, per, 1)
    m2 = jnp.max(jnp.where(ri == i1, neg, sel3), axis=1, keepdims=True)
    grp = (m1 + m2).reshape(N_GROUPS, tm)

    gi = lax.broadcasted_iota(I32, grp.shape, 0)
    chosen = jnp.zeros(grp.shape, F32)
    for _ in range(TOPK_GROUPS):
        _, ix = _first_argmax(grp, gi, N_GROUPS, 0)
        hit = gi == ix
        chosen = jnp.where(hit, 1.0, chosen)
        grp = jnp.where(hit, neg, grp)
    masked = jnp.where(chosen.reshape(N_GROUPS, 1, tm) > 0.0, sel3, neg).reshape(ne, tm)

    ei = lax.broadcasted_iota(I32, (ne, tm), 0)
    msel = jnp.zeros((ne, tm), F32)
    idx_rows, w_rows = [], []
    for _ in range(TOP_K):
        _, ix = _first_argmax(masked, ei, ne, 0)
        hit = ei == ix
        w_rows.append(jnp.sum(jnp.where(hit, scores, 0.0), axis=0, keepdims=True))
        idx_rows.append(ix)
        msel = jnp.where(hit, 1.0, msel)
        masked = jnp.where(hit, neg, masked)
    w = jnp.concatenate(w_rows, axis=0)
    eidx_ref[...] = jnp.concatenate(idx_rows, axis=0)
    wts_ref[...] = w / (jnp.sum(w, axis=0, keepdims=True) + 1e-20) * ROUTED_SCALE

    mb = msel.astype(BF16)
    a = lax.broadcasted_iota(I32, (tm, tm), 0)
    b = lax.broadcasted_iota(I32, (tm, tm), 1)
    before = _dot(mb, jnp.where(a < b, 1.0, 0.0).astype(BF16))
    pos = base_s[...] + before
    rank_ref[...] = jnp.concatenate(
        [jnp.sum(jnp.where(ei == ix, pos, 0.0), axis=0, keepdims=True) for ix in idx_rows], axis=0).astype(I32)
    base_s[...] = base_s[...] + _dot(mb, jnp.ones((tm, tm), BF16))
    cnt_ref[...] = base_s[:, :LANES].astype(I32)


def _router(x1, g2, sc2, sh2, wt_hi, wt_lo, bias, seq, tm=512):
    t, d = x1.shape
    ne = wt_hi.shape[0]
    tm = min(tm, seq)
    per = seq // tm
    bvec = pl.BlockSpec((1, 1, d), lambda i: (i // per, 0, 0))
    kt = lambda dt: jax.ShapeDtypeStruct((TOP_K, t), dt)
    kspec = pl.BlockSpec((TOP_K, tm), lambda i: (0, i))
    return pl.pallas_call(
        _router_kernel,
        out_shape=(kt(I32), kt(F32), kt(I32), jax.ShapeDtypeStruct((ne, LANES), I32)),
        grid=(t // tm,),
        in_specs=[pl.BlockSpec((tm, d), lambda i: (i, 0)),
                  pl.BlockSpec((1, d), lambda i: (0, 0)), bvec, bvec,
                  pl.BlockSpec((ne, d), lambda i: (0, 0)),
                  pl.BlockSpec((ne, d), lambda i: (0, 0)),
                  pl.BlockSpec((ne, tm), lambda i: (0, 0))],
        out_specs=(kspec, kspec, kspec, pl.BlockSpec((ne, LANES), lambda i: (0, 0))),
        scratch_shapes=[pltpu.VMEM((ne, tm), F32)],
        compiler_params=_cparams("arbitrary"),
        name="router",
    )(x1, g2, sc2, sh2, wt_hi, wt_lo, jnp.broadcast_to(bias.reshape(ne, 1), (ne, tm)))


def _rows_kernel(eidx_ref, rank_ref, pstart_ref, rows_ref):
    ne, tm = pstart_ref.shape
    ei = lax.broadcasted_iota(I32, (ne, tm), 0)
    ps = pstart_ref[...]
    eidx = eidx_ref[...]
    rows_ref[...] = rank_ref[...] + jnp.concatenate(
        [jnp.sum(jnp.where(ei == eidx[k:k + 1, :], ps, 0), axis=0, keepdims=True) for k in range(TOP_K)], axis=0)


def _pair_rows(eidx, rank, pstart, tm=512):
    t = eidx.shape[1]
    ne = pstart.shape[0]
    tm = min(tm, t)
    kspec = pl.BlockSpec((TOP_K, tm), lambda i: (0, i))
    return pl.pallas_call(
        _rows_kernel,
        out_shape=jax.ShapeDtypeStruct((TOP_K, t), I32),
        grid=(t // tm,),
        in_specs=[kspec, kspec, pl.BlockSpec((ne, tm), lambda i: (0, 0))],
        out_specs=kspec,
        compiler_params=_cparams("arbitrary"),
        name="pair_rows",
    )(eidx, rank, jnp.broadcast_to(pstart.reshape(ne, 1), (ne, tm)))


def _tile(ref, r, n=1):
    return ref.at[pl.ds(pl.multiple_of(r * SUBLANES, SUBLANES), n * SUBLANES), :]


def _token_copy_all(src_of, dst_of, sem, rows_ref, tm):
    def issue(t, carry):
        for k in range(TOP_K):
            row = rows_ref[k, t]
            pltpu.make_async_copy(src_of(k, t, row), dst_of(k, t, row), sem).start(priority=k % 2)
        return carry
    lax.fori_loop(0, tm, issue, 0, unroll=4)


def _swiglu_packed(h_ref, wg_ref, wu_ref, wd_ref):
    lo, hi = _unpack_halves(_load_token_tiles(h_ref))
    lo, hi = lo.astype(BF16), hi.astype(BF16)
    n = lo.shape[1]
    g = _dot(lo, wg_ref[:n, :]) + _dot(hi, wg_ref[n:, :])
    u = _dot(lo, wu_ref[:n, :]) + _dot(hi, wu_ref[n:, :])
    return _dot((g * jax.nn.sigmoid(g) * u).astype(BF16), wd_ref[...])


def _dispatch_kernel(pstart_ref, cnt_ref, nused_ref, h_ref, rows_ref, wg_ref, wu_ref, wd_ref,
                     xs_ref, shared_ref, zbuf, sem, zsem):
    i = pl.program_id(0)
    tm = h_ref.shape[0] // SUBLANES
    bm = zbuf.shape[0] // SUBLANES
    ne = cnt_ref.shape[0]
    nblk = xs_ref.shape[0] // (bm * SUBLANES)

    def zero_fill(act):
        def per_expert(e, carry):
            c = cnt_ref[e]
            pad = lax.rem(bm - lax.rem(c, bm), bm)
            off = pstart_ref[e] + c
            s = bm // 2
            while s >= 1:
                @pl.when((pad & s) != 0)
                def _(s=s, off=off):
                    act(pltpu.make_async_copy(_tile(zbuf, 0, s), _tile(xs_ref, off, s), zsem))
                off = off + (pad & s)
                s //= 2
            return carry
        lax.fori_loop(0, ne, per_expert, 0)

        def per_block(j, carry):
            act(pltpu.make_async_copy(zbuf, _tile(xs_ref, j * bm, bm), zsem))
            return carry
        lax.fori_loop(nused_ref[0], nblk, per_block, 0)

    @pl.when(i == 0)
    def _():
        zbuf[...] = jnp.zeros_like(zbuf)
        zero_fill(lambda cp: cp.start())

    _token_copy_all(lambda k, t, row: _tile(h_ref, t), lambda k, t, row: _tile(xs_ref, row), sem, rows_ref, tm)
    shared_ref[...] = _swiglu_packed(h_ref, wg_ref, wu_ref, wd_ref).astype(BF16)
    for _ in range(TOP_K):
        pltpu.make_async_copy(h_ref, _tile(xs_ref, 0, tm), sem).wait()

    @pl.when(i == 0)
    def _():
        zero_fill(lambda cp: cp.wait())


def _dispatch(h2p, rows, pstart, counts, n_used, sh_wg, sh_wu, sh_wd, n_rows, bm, tm=256):
    t = h2p.shape[0] // SUBLANES
    d, f = sh_wg.shape
    tm = min(tm, t)
    const = lambda shp: pl.BlockSpec(shp, lambda i, *_: (0, 0), pipeline_mode=pl.Buffered(1))
    return pl.pallas_call(
        _dispatch_kernel,
        out_shape=(jax.ShapeDtypeStruct((n_rows * SUBLANES, LANES), U32), jax.ShapeDtypeStruct((t, d), BF16)),
        grid_spec=pltpu.PrefetchScalarGridSpec(
            num_scalar_prefetch=3,
            grid=(t // tm,),
            in_specs=[pl.BlockSpec((tm * SUBLANES, LANES), lambda i, *_: (i, 0)),
                      pl.BlockSpec((TOP_K, tm), lambda i, *_: (0, i), memory_space=pltpu.SMEM),
                      const((d, f)), const((d, f)), const((f, d))],
            out_specs=(pl.BlockSpec(memory_space=pl.ANY), pl.BlockSpec((tm, d), lambda i, *_: (i, 0))),
            scratch_shapes=[pltpu.VMEM((bm * SUBLANES, LANES), U32), pltpu.SemaphoreType.DMA(()),
                            pltpu.SemaphoreType.DMA(())]),
        compiler_params=_cparams("arbitrary"),
        name="dispatch",
    )(pstart, counts, n_used, h2p, rows, sh_wg, sh_wu, sh_wd)


def _expert_kernel(be_ref, first_ref, nxt_ref, slot_ref, nu_ref, xs_ref, wg_hbm, wu_hbm, wd_hbm, y_ref,
                   wg_buf, wu_buf, wd_buf, sems):
    j = pl.program_id(0)
    nu = nu_ref[0]

    def weight_copies(e, s):
        return [pltpu.make_async_copy(hbm.at[e], buf.at[s], sems.at[s, i])
                for i, (hbm, buf) in enumerate(((wg_hbm, wg_buf), (wu_hbm, wu_buf), (wd_hbm, wd_buf)))]

    @pl.when((j == 0) & (nu > 0))
    def _():
        for cp in weight_copies(be_ref[0], slot_ref[0]):
            cp.start()

    @pl.when((j < nu) & (first_ref[j] == 1))
    def _():
        for cp in weight_copies(be_ref[j], slot_ref[j]):
            cp.wait()

        @pl.when(nxt_ref[j] >= 0)
        def _():
            for cp in weight_copies(nxt_ref[j], 1 - slot_ref[j]):
                cp.start()

    @pl.when(j < nu)
    def _():
        s = slot_ref[j]
        lo, hi = _unpack_halves(_load_token_tiles(xs_ref))
        lo, hi = lo.astype(BF16), hi.astype(BF16)
        n = lo.shape[1]
        wg = wg_buf[s].astype(BF16)
        wu = wu_buf[s].astype(BF16)
        g = _dot(lo, wg[:n]) + _dot(hi, wg[n:])
        u = _dot(lo, wu[:n]) + _dot(hi, wu[n:])
        a = (g * jax.nn.sigmoid(g) * u).astype(BF16)
        _store_token_tiles(y_ref, _pack_halves(_dot(a, wd_buf[s].astype(BF16))))

    @pl.when(j >= nu)
    def _():
        y_ref[...] = jnp.zeros_like(y_ref)


def _experts(xs, block_e, block_first, block_next, block_slot, n_used, w_gate, w_up, w_down, bm):
    n_rows = xs.shape[0] // SUBLANES
    ne, d, f = w_gate.shape
    nblk = n_rows // bm
    blk = (bm * SUBLANES, LANES)
    return pl.pallas_call(
        _expert_kernel,
        out_shape=jax.ShapeDtypeStruct(xs.shape, U32),
        grid_spec=pltpu.PrefetchScalarGridSpec(
            num_scalar_prefetch=5,
            grid=(nblk,),
            in_specs=[pl.BlockSpec(blk, lambda j, be, fi, nx, sl, nu: (jnp.minimum(j, nu[0] - 1), 0)),
                      pl.BlockSpec(memory_space=pl.ANY),
                      pl.BlockSpec(memory_space=pl.ANY),
                      pl.BlockSpec(memory_space=pl.ANY)],
            out_specs=pl.BlockSpec(blk, lambda j, *_: (j, 0)),
            scratch_shapes=[pltpu.VMEM((2, d, f), F32), pltpu.VMEM((2, d, f), F32), pltpu.VMEM((2, f, d), F32),
                            pltpu.SemaphoreType.DMA((2, 3))]),
        compiler_params=_cparams("arbitrary"),
        name="experts",
    )(block_e, block_first, block_next, block_slot, n_used, xs, w_gate, w_up, w_down)


def _final_kernel(n_first, x1_ref, sh_ref, gt_ref, wts_ref, fg_ref, rows_ref, rows_nx_ref, y_ref,
                  oa_ref, ob_ref, ybuf, sems):
    i = pl.program_id(0)
    tm = x1_ref.shape[0]
    n = x1_ref.shape[1] // 2
    slot = lax.rem(i, 2)

    def gather(r_ref, s):
        _token_copy_all(lambda k, t, row: _tile(y_ref, row), lambda k, t, row: _tile(ybuf.at[s, k], t),
                        sems.at[s], r_ref, tm)

    pl.when(i == 0)(lambda: gather(rows_ref, 0))
    pl.when(i + 1 < pl.num_programs(0))(lambda: gather(rows_nx_ref, 1 - slot))

    for k in range(TOP_K):
        pltpu.make_async_copy(_tile(y_ref, 0, tm), ybuf.at[slot, k], sems.at[slot]).wait()
    shared = sh_ref[...].astype(F32)
    acc_lo, acc_hi = shared[:, :n], shared[:, n:]
    wts = wts_ref[...].T
    for k in range(TOP_K):
        ylo, yhi = _unpack_halves(_load_token_tiles(ybuf.at[slot, k]))
        wk = wts[:, k:k + 1]
        acc_lo = acc_lo + wk * ylo
        acc_hi = acc_hi + wk * yhi
    gt = gt_ref[0]
    x_lo = x1_ref[:, :n] + gt[:, :n] * acc_lo
    x_hi = x1_ref[:, n:] + gt[:, n:] * acc_hi
    ms = (jnp.sum(x_lo * x_lo, axis=-1, keepdims=True) + jnp.sum(x_hi * x_hi, axis=-1, keepdims=True)) / (2 * n)
    r = lax.rsqrt(ms + RMS_EPS)

    def write(o_ref):
        o_ref[:, :n] = x_lo * r * fg_ref[:, :n]
        o_ref[:, n:] = x_hi * r * fg_ref[:, n:]

    pl.when(i < n_first)(lambda: write(oa_ref))
    pl.when(i >= n_first)(lambda: write(ob_ref))


def _final(x1, shared, gt2, wts, final_g, rows, y, rows_first, seq, tm=256):
    t, d = x1.shape
    tm = min(tm, seq)
    per = seq // tm
    nt = t // tm
    n_first = rows_first // tm
    out_a, out_b = _two_trunk_specs((tm, d), n_first)
    return pl.pallas_call(
        functools.partial(_final_kernel, n_first),
        out_shape=(jax.ShapeDtypeStruct((rows_first, d), F32), jax.ShapeDtypeStruct((t - rows_first, d), F32)),
        grid=(nt,),
        in_specs=[pl.BlockSpec((tm, d), lambda i: (i, 0)),
                  pl.BlockSpec((tm, d), lambda i: (i, 0)),
                  pl.BlockSpec((1, 1, d), lambda i: (i // per, 0, 0)),
                  pl.BlockSpec((TOP_K, tm), lambda i: (0, i)),
                  pl.BlockSpec((1, d), lambda i: (0, 0)),
                  pl.BlockSpec((TOP_K, tm), lambda i: (0, i), memory_space=pltpu.SMEM),
                  pl.BlockSpec((TOP_K, tm), lambda i: (0, jnp.minimum(i + 1, nt - 1)), memory_space=pltpu.SMEM),
                  pl.BlockSpec(memory_space=pl.ANY)],
        out_specs=(out_a, out_b),
        scratch_shapes=[pltpu.VMEM((2, TOP_K, tm * SUBLANES, LANES), U32), pltpu.SemaphoreType.DMA((2,))],
        compiler_params=_cparams("arbitrary"),
        name="final",
    )(x1, shared, gt2, wts, final_g, rows, rows, y)


EXPERT_ROWS = 256


def _prep_moe(w):
    rw = w["router_w"][0]
    d = rw.shape[0]
    wt = rw.T
    wt_hi = wt.astype(BF16)
    return dict(
        wt_hi=wt_hi, wt_lo=(wt - wt_hi.astype(F32)).astype(BF16), bias=w["router_bias"][0],
        w_gate=w["exp_w_gate"][0], w_up=w["exp_w_up"][0], w_down=w["exp_w_down"][0],
        sh_wg=w["sh_w_gate"][0].astype(BF16), sh_wu=w["sh_w_up"][0].astype(BF16),
        sh_wd=w["sh_w_down"][0].astype(BF16), final_g=w["final_g"].reshape(1, d),
    )


def _moe_stage(x1, h2p, mods, pm, pw, seq, rows_first):
    _, _, _, sh2, sc2, gt2 = mods
    t = x1.shape[0]
    ne = pw["wt_hi"].shape[0]
    bm = EXPERT_ROWS
    eidx, wts, rank, cnt = _router(x1, pm["norm_ffn_g"], sc2, sh2, pw["wt_hi"], pw["wt_lo"], pw["bias"], seq)
    counts = cnt[:, 0]
    pcounts = (counts + bm - 1) // bm * bm
    pends = jnp.cumsum(pcounts)
    pstart = (pends - pcounts).astype(I32)
    nblk = (t * TOP_K + ne * (bm - 1) + bm - 1) // bm
    n_used = (pends[-1:] // bm).astype(I32)
    jb = jnp.arange(nblk, dtype=I32)
    be = jnp.minimum(jnp.sum((pends[None, :] <= (jb * bm)[:, None]).astype(I32), axis=1), ne - 1)
    ids = jnp.arange(ne, dtype=I32)
    onehot = be[:, None] == ids[None, :]
    look = lambda table: jnp.sum(jnp.where(onehot, table[None, :], 0), axis=1).astype(I32)
    first = (jb * bm == look(pstart)).astype(I32)
    live = counts > 0
    ordinal = jnp.cumsum(live.astype(I32)) - 1
    after = lax.cummin(jnp.where(live, ids, ne), reverse=True)
    next_live = jnp.concatenate([after[1:], jnp.full((1,), ne, I32)])
    nxt = look(jnp.where(next_live < ne, next_live, -1))
    slot = look(ordinal) % 2
    rows = _pair_rows(eidx, rank, pstart)
    xs, shared = _dispatch(h2p, rows, pstart, counts, n_used, pw["sh_wg"], pw["sh_wu"], pw["sh_wd"],
                           nblk * bm, bm)
    y = _experts(xs, be, first, nxt, slot, n_used, pw["w_gate"], pw["w_up"], pw["w_down"], bm)
    return _final(x1, shared, gt2, wts, pw["final_g"], rows, y, rows_first, seq)


def kernel(x_prompt, x_sample, c_prompt, c_sample, ada_w, ada_b, norm_mix_g, norm_ffn_g, w_in, sgu_ln_g,
           sgu_ln_b, sgu_ws, sgu_b, dn_conv_w, dn_a_log, dn_dt_bias, dn_norm_g, w_branch, w_out, router_w,
           router_bias, exp_w_gate, exp_w_up, exp_w_down, sh_w_gate, sh_w_up, sh_w_down, final_g):
    w = dict(ada_w=ada_w, ada_b=ada_b, norm_mix_g=norm_mix_g, norm_ffn_g=norm_ffn_g, w_in=w_in,
             sgu_ln_g=sgu_ln_g, sgu_ln_b=sgu_ln_b, sgu_ws=sgu_ws, sgu_b=sgu_b, dn_conv_w=dn_conv_w,
             dn_a_log=dn_a_log, dn_dt_bias=dn_dt_bias, dn_norm_g=dn_norm_g, w_branch=w_branch, w_out=w_out,
             router_w=router_w, router_bias=router_bias, exp_w_gate=exp_w_gate, exp_w_up=exp_w_up,
             exp_w_down=exp_w_down, sh_w_gate=sh_w_gate, sh_w_up=sh_w_up, sh_w_down=sh_w_down, final_g=final_g)
    assert w_in.shape[0] == 1, "one layer"
    bp, seq, d = x_prompt.shape
    bs = x_sample.shape[0]
    assert x_sample.shape[1] == seq
    nb = bp + bs
    xa = x_prompt.reshape(bp * seq, d)
    xb = x_sample.reshape(bs * seq, d)
    c = jnp.concatenate([c_prompt, c_sample], axis=0)
    npad = -nb % 8
    mod = _ada(jnp.pad(c, ((0, npad), (0, 0))), ada_w[0], ada_b[0])[:nb]
    mods = [m.reshape(nb, 1, d) for m in jnp.split(mod, 6, axis=-1)]
    pm = _prep_mixer(w, nb, seq)
    pw = _prep_moe(w)
    x1, h2p = _token_mixer_stage(xa, xb, mods, pm)
    ya, yb = _moe_stage(x1, h2p, mods, pm, pw, seq, bp * seq)
    return (ya.reshape(bp, seq, d), yb.reshape(bs, seq, d))
```

```python
import functools

import jax
import jax.numpy as jnp
import numpy as np
from jax import lax
from jax.experimental import pallas as pl
from jax.experimental.pallas import tpu as pltpu

F32 = jnp.float32
BF16 = jnp.bfloat16
I32 = jnp.int32
U32 = jnp.uint32

RMS_EPS = 1e-6
LN_EPS = 1e-5
L2_EPS = 1e-6

SGU_GROUPS = 8
SGU_CHUNK = 128
DN_HEADS = 8
DN_HEAD_DIM = 128
DN_CHUNK = 256
DN_PREP_UNROLL = 2
CONV_WIDTH = 5
TOP_K = 8
N_GROUPS = 8
TOPK_GROUPS = 4
ROUTED_SCALE = 2.5

LANES = 128
SUBLANES = 8
VMEM_LIMIT = 56 * 1024 * 1024


def _cparams(*sem):
    return pltpu.CompilerParams(dimension_semantics=sem, vmem_limit_bytes=VMEM_LIMIT)


def _split3(x):
    hi = x.astype(BF16)
    r = x - hi.astype(F32)
    mid = r.astype(BF16)
    lo = (r - mid.astype(F32)).astype(BF16)
    return hi, mid, lo


def _dot(a, b):
    return jnp.dot(a, b, preferred_element_type=F32)


def _dot_nt(a, b):
    return lax.dot_general(a, b, (((1,), (1,)), ((), ())), preferred_element_type=F32)


def _dot_tn(a, b):
    return lax.dot_general(a, b, (((0,), (0,)), ((), ())), preferred_element_type=F32)


def _rms(x, g):
    return x * lax.rsqrt(jnp.mean(x * x, axis=-1, keepdims=True) + RMS_EPS) * g


def _pack_halves(x):
    n = x.shape[1] // 2
    return pltpu.pack_elementwise([x[:, :n], x[:, n:]], packed_dtype=BF16)


def _unpack_halves(w):
    lo = pltpu.unpack_elementwise(w, index=0, packed_dtype=BF16, unpacked_dtype=F32)
    hi = pltpu.unpack_elementwise(w, index=1, packed_dtype=BF16, unpacked_dtype=F32)
    return lo, hi


def _store_token_tiles(ref, x):
    m = x.shape[0]
    for s in range(SUBLANES):
        ref[pl.ds(s, m, stride=SUBLANES), :] = x[:, s * LANES:(s + 1) * LANES]


def _load_token_tiles(ref):
    m = ref.shape[0] // SUBLANES
    return jnp.concatenate([ref[pl.ds(s, m, stride=SUBLANES), :] for s in range(SUBLANES)], axis=1)


def _ada_kernel(c_ref, w_ref, b_ref, o_ref):
    c = c_ref[...]
    a = c * jax.nn.sigmoid(c)
    a_hi, a_mid, _ = _split3(a)
    w = w_ref[...]
    w_hi = w.astype(BF16)
    w_lo = (w - w_hi.astype(F32)).astype(BF16)
    o_ref[...] = _dot(a_hi, w_hi) + _dot(a_mid, w_hi) + _dot(a_hi, w_lo) + b_ref[...]


def _ada(c, ada_w, ada_b, tn=1024):
    nb, d = c.shape
    n = ada_w.shape[1]
    return pl.pallas_call(
        _ada_kernel,
        out_shape=jax.ShapeDtypeStruct((nb, n), F32),
        grid=(n // tn,),
        in_specs=[pl.BlockSpec((nb, d), lambda j: (0, 0)),
                  pl.BlockSpec((d, tn), lambda j: (0, j)),
                  pl.BlockSpec((1, tn), lambda j: (0, j))],
        out_specs=pl.BlockSpec((nb, tn), lambda j: (0, j)),
        compiler_params=_cparams("arbitrary"),
        name="ada",
    )(c, ada_w, ada_b.reshape(1, n))


def _two_trunk_specs(block, n_first):
    first = pl.BlockSpec(block, lambda i, *_: (jnp.minimum(i, n_first - 1), 0))
    second = pl.BlockSpec(block, lambda i, *_: (jnp.maximum(i - n_first, 0), 0))
    return first, second


def _inproj_kernel(n_first, xa_ref, xb_ref, g_ref, sc_ref, sh_ref, w_ref, wab_ref, z_ref, ab_ref, h_s):
    i = pl.program_id(0)
    j = pl.program_id(1)

    def prologue(x_ref):
        h = _rms(x_ref[...], g_ref[...]) * (1.0 + sc_ref[0]) + sh_ref[0]
        h_s[...] = h.astype(BF16)
        ab_ref[...] = _dot(h_s[...], wab_ref[...])

    pl.when((j == 0) & (i < n_first))(lambda: prologue(xa_ref))
    pl.when((j == 0) & (i >= n_first))(lambda: prologue(xb_ref))
    z_ref[...] = _dot(h_s[...], w_ref[...]).astype(BF16)


def _inproj(xa, xb, g, sc, sh, w_main, w_ab, seq, tm=512, tn=2048):
    d = xa.shape[1]
    t = xa.shape[0] + xb.shape[0]
    n = w_main.shape[1]
    tm = min(tm, seq)
    per = seq // tm
    n_first = xa.shape[0] // tm
    spec_a, spec_b = _two_trunk_specs((tm, d), n_first)
    return pl.pallas_call(
        functools.partial(_inproj_kernel, n_first),
        out_shape=(jax.ShapeDtypeStruct((t, n), BF16), jax.ShapeDtypeStruct((t, LANES), F32)),
        grid=(t // tm, n // tn),
        in_specs=[spec_a, spec_b,
                  pl.BlockSpec((1, d), lambda i, j: (0, 0)),
                  pl.BlockSpec((1, 1, d), lambda i, j: (i // per, 0, 0)),
                  pl.BlockSpec((1, 1, d), lambda i, j: (i // per, 0, 0)),
                  pl.BlockSpec((d, tn), lambda i, j: (0, j)),
                  pl.BlockSpec((d, LANES), lambda i, j: (0, 0))],
        out_specs=(pl.BlockSpec((tm, tn), lambda i, j: (i, j)),
                   pl.BlockSpec((tm, LANES), lambda i, j: (i, 0))),
        scratch_shapes=[pltpu.VMEM((tm, d), BF16)],
        compiler_params=_cparams("arbitrary", "arbitrary"),
        name="inproj",
    )(xa, xb, g, sc, sh, w_main, w_ab)


def _sgu_kernel(u_ref, v_ref, g_ref, b_ref, ws_ref, sb_ref, o_ref):
    v = jax.nn.gelu(v_ref[...].astype(F32))
    mu = jnp.mean(v, axis=-1, keepdims=True)
    vc = v - mu
    var = jnp.mean(vc * vc, axis=-1, keepdims=True)
    vn = (vc * lax.rsqrt(var + LN_EPS) * g_ref[...] + b_ref[...]).astype(BF16)
    tm = vn.shape[0]
    for c in range(tm // SGU_CHUNK):
        r = slice(c * SGU_CHUNK, (c + 1) * SGU_CHUNK)
        for gi in range(SGU_GROUPS):
            l = slice(gi * LANES, (gi + 1) * LANES)
            mixed = _dot(ws_ref[gi], vn[r, l]) + sb_ref[gi]
            o_ref[r, l] = (jax.nn.gelu(u_ref[r, l].astype(F32)) * mixed).astype(BF16)


def _sgu(z, ln_g, ln_b, ws, sb_b, tm=512):
    t = z.shape[0]
    w = SGU_GROUPS * LANES
    tm = np.gcd(tm, t)
    return pl.pallas_call(
        _sgu_kernel,
        out_shape=jax.ShapeDtypeStruct((t, w), BF16),
        grid=(t // tm,),
        in_specs=[pl.BlockSpec((tm, w), lambda i: (i, 0)),
                  pl.BlockSpec((tm, w), lambda i: (i, 1)),
                  pl.BlockSpec((1, w), lambda i: (0, 0)),
                  pl.BlockSpec((1, w), lambda i: (0, 0)),
                  pl.BlockSpec((SGU_GROUPS, SGU_CHUNK, SGU_CHUNK), lambda i: (0, 0, 0)),
                  pl.BlockSpec((SGU_GROUPS, SGU_CHUNK, LANES), lambda i: (0, 0, 0))],
        out_specs=pl.BlockSpec((tm, w), lambda i: (i, 0)),
        compiler_params=_cparams("arbitrary"),
        name="sgu",
    )(z, z, ln_g, ln_b, ws, sb_b)


def _gate_kernel(ab_ref, alog_ref, dt_ref, gcol_ref, grow_ref):
    ab = ab_ref[...]
    tr = ab.shape[0]
    lane = lax.broadcasted_iota(I32, ab.shape, 1)
    z = ab + dt_ref[...]
    softplus = jnp.maximum(z, 0.0) + jnp.log1p(jnp.exp(-jnp.abs(z)))
    g = -jnp.exp(alog_ref[...]) * softplus
    beta = jax.nn.sigmoid(ab)
    nh = DN_HEADS
    g = jnp.where(lane < 2 * nh, g, 0.0)
    i = lax.broadcasted_iota(I32, (tr, tr), 0)
    j = lax.broadcasted_iota(I32, (tr, tr), 1)
    same = (i // DN_CHUNK) == (j // DN_CHUNK)
    lower = jnp.where(same & (j <= i), 1.0, 0.0).astype(BF16)
    upper = jnp.where(same & (j >= i), 1.0, 0.0).astype(BF16)
    g_hi, g_mid, g_lo = _split3(g)
    pre = _dot(lower, g_hi) + _dot(lower, g_mid) + _dot(lower, g_lo)
    suf = _dot(upper, g_hi) + _dot(upper, g_mid) + _dot(upper, g_lo)
    out = jnp.where(lane < nh, pre, jnp.where(lane < 2 * nh, suf, jnp.where(lane < 4 * nh, beta, 0.0)))
    gcol_ref[...] = out
    grow_ref[0] = out.T[:2 * nh, :]


def _gates(ab, alog_row, dt_row, nb, seq, tr=512):
    t = ab.shape[0]
    per = seq // tr
    return pl.pallas_call(
        _gate_kernel,
        out_shape=(jax.ShapeDtypeStruct((t, LANES), F32),
                   jax.ShapeDtypeStruct((nb, 2 * DN_HEADS, seq), F32)),
        grid=(t // tr,),
        in_specs=[pl.BlockSpec((tr, LANES), lambda i: (i, 0)),
                  pl.BlockSpec((1, LANES), lambda i: (0, 0)),
                  pl.BlockSpec((1, LANES), lambda i: (0, 0))],
        out_specs=(pl.BlockSpec((tr, LANES), lambda i: (i, 0)),
                   pl.BlockSpec((1, 2 * DN_HEADS, tr), lambda i: (i // per, 0, i % per))),
        compiler_params=_cparams("arbitrary"),
        name="gates",
    )(ab, alog_row, dt_row)


def _conv_silu(x_ref, w_ref):
    x = x_ref[...].astype(F32)
    s = x.shape[0]
    row = lax.broadcasted_iota(I32, x.shape, 0)
    pad = (CONV_WIDTH - 1) // 2
    acc = x * w_ref[pad:pad + 1, :]
    for j in range(CONV_WIDTH):
        d = j - pad
        if d == 0:
            continue
        xs = pltpu.roll(x, (-d) % s, 0)
        ok = (row + d >= 0) & (row + d < s)
        acc = acc + jnp.where(ok, xs, 0.0) * w_ref[j:j + 1, :]
    return acc * jax.nn.sigmoid(acc)


def _l2n(x):
    return x * lax.rsqrt(jnp.sum(x * x, axis=-1, keepdims=True) + L2_EPS)


def _unit_tri_inverses(mats, i, j):
    c = mats[0].shape[0]
    x = i ^ j
    eye = jnp.where(i == j, 1.0, 0.0)
    tinv = [eye - jnp.where(x < 2, a, 0.0) for a in mats]
    s = 2
    while s < c:
        level = (x >= s) & (x < 2 * s)
        tb = [t.astype(BF16) for t in tinv]
        m = [_dot(t, jnp.where(level, a, 0.0).astype(BF16)).astype(BF16) for t, a in zip(tb, mats)]
        tinv = [t - _dot(mm, b) for t, mm, b in zip(tinv, m, tb)]
        s *= 2
    return tinv


def _dn_prep(items):
    c, hd = items[0][1].shape
    i = lax.broadcasted_iota(I32, (c, c), 0)
    j = lax.broadcasted_iota(I32, (c, c), 1)
    decs, kbs, kbfs = [], [], []
    for forward, q, k, v, gam, grow, beta in items:
        incl = (i >= j) if forward else (i <= j)
        gi = jnp.concatenate([gam] * (c // LANES), axis=1)
        decs.append(jnp.where(incl, jnp.exp(gi - grow), 0.0))
        kbs.append(k * beta)
        kbfs.append(k.astype(BF16))
    kk = [_dot_nt(kb.astype(BF16), kbf) for kb, kbf in zip(kbs, kbfs)]
    mats = [jnp.where((i > j) if it[0] else (i < j), m * dec, 0.0) for it, m, dec in zip(items, kk, decs)]
    tinvs = _unit_tri_inverses(mats, i, j)
    egs = [jnp.exp(it[4]) for it in items]
    uws = [_dot(t.astype(BF16), jnp.concatenate([it[3] * it[6], kb * eg], axis=1).astype(BF16))
           for t, it, kb, eg in zip(tinvs, items, kbs, egs)]
    attns = [_dot_nt(it[1].astype(BF16), kbf) * dec for it, kbf, dec in zip(items, kbfs, decs)]
    outs = []
    for (forward, q, k, v, gam, grow, beta), uw, eg, attn in zip(items, uws, egs, attns):
        glast = gam[c - 1:c, :] if forward else gam[0:1, :]
        kd = k * jnp.exp(glast - gam)
        outs.append((uw[:, :hd], uw[:, hd:].astype(BF16), (q * eg).astype(BF16), kd.T.astype(BF16),
                     attn.astype(BF16)))
    return outs


def _dn_kernel(q_ref, k_ref, v_ref, og_ref, wq_ref, wk_ref, wv_ref, gcol_ref, grow_ref, ng_ref, o_ref,
               q_s, k_s, v_s, gam_s, beta_s, u_s, w_s, qd_s, kdt_s, attn_s, o_s):
    h = pl.program_id(1)
    nh = DN_HEADS
    hd = DN_HEAD_DIM
    q_s[...] = _l2n(_conv_silu(q_ref, wq_ref)) * (hd ** -0.5)
    k_s[...] = _l2n(_conv_silu(k_ref, wk_ref))
    v_s[...] = _conv_silu(v_ref, wv_ref)

    g_hi, g_mid, g_lo = _split3(gcol_ref[...])
    rr = lax.broadcasted_iota(I32, (LANES, 2 * LANES), 0)
    cc = lax.broadcasted_iota(I32, (LANES, 2 * LANES), 1)
    sel = lambda off: jnp.where(rr == h + off + jnp.where(cc < LANES, 0, nh), 1.0, 0.0).astype(BF16)
    sel_g = sel(0)
    gam2 = (_dot(jnp.concatenate([g_hi, g_mid], axis=1), jnp.concatenate([sel_g, sel_g], axis=0))
            + _dot(g_lo, sel_g))
    beta2 = _dot(g_hi, sel(2 * nh))
    for d in range(2):
        gam_s[d] = gam2[:, d * LANES:(d + 1) * LANES]
        beta_s[d] = beta2[:, d * LANES:(d + 1) * LANES]

    s = q_s.shape[0]
    c = DN_CHUNK
    n = s // c

    def prep(it, carry):
        where, items = [], []
        for sub in range(DN_PREP_UNROLL):
            rows = pl.ds(pl.multiple_of((it * DN_PREP_UNROLL + sub) * c, c), c)
            for d in range(2):
                where.append((d, rows))
                items.append((d == 0, q_s[rows, :], k_s[rows, :], v_s[rows, :], gam_s[d, rows, :],
                              grow_ref[0, pl.ds(d * nh + h, 1), rows], beta_s[d, rows, :]))
        for (d, rows), (u, w, qd, kdt, attn) in zip(where, _dn_prep(items)):
            u_s[d, rows, :] = u
            w_s[d, rows, :] = w
            qd_s[d, rows, :] = qd
            kdt_s[d, :, rows] = kdt
            attn_s[d, rows, :] = attn
        return carry

    lax.fori_loop(0, n // DN_PREP_UNROLL, prep, 0)

    def scan(it, states):
        r0s = [pl.multiple_of((it if d == 0 else n - 1 - it) * c, c) for d in range(2)]
        rows = [pl.ds(r0, c) for r0 in r0s]
        rs = [_dot(jnp.concatenate([w_s[d, rows[d], :], qd_s[d, rows[d], :]], axis=0), states[d].astype(BF16))
              for d in range(2)]
        vbs = [(u_s[d, rows[d], :] - rs[d][:c]).astype(BF16) for d in range(2)]
        for d in range(2):
            o_s[d, rows[d], :] = rs[d][c:] + _dot(attn_s[d, rows[d], :], vbs[d])
        new = []
        for d in range(2):
            glast = gam_s[d, pl.ds(r0s[d] + (c - 1 if d == 0 else 0), 1), :]
            new.append(states[d] * jnp.exp(glast) + _dot(kdt_s[d, :, rows[d]], vbs[d]))
        return tuple(new)

    zero = jnp.zeros((hd, hd), F32)
    lax.fori_loop(0, n, scan, (zero, zero))

    o = o_s[0] + o_s[1]
    og = og_ref[...].astype(F32)
    o_ref[...] = (_rms(o, ng_ref[...]) * (og * jax.nn.sigmoid(og))).astype(BF16)


def _deltanet(z, conv_w, gcol, grow, norm_g, nb, seq, qkv_blk, og_blk):
    t = z.shape[0]
    nh, hd, c = DN_HEADS, DN_HEAD_DIM, DN_CHUNK
    assert seq % (c * DN_PREP_UNROLL) == 0
    zspec = lambda off: pl.BlockSpec((seq, hd), lambda b, h: (b, off + h))
    wspec = lambda off: pl.BlockSpec((CONV_WIDTH, hd), lambda b, h: (0, off + h))
    return pl.pallas_call(
        _dn_kernel,
        out_shape=jax.ShapeDtypeStruct((t, nh * hd), BF16),
        grid=(nb, nh),
        in_specs=[zspec(qkv_blk), zspec(qkv_blk + nh), zspec(qkv_blk + 2 * nh), zspec(og_blk),
                  wspec(0), wspec(nh), wspec(2 * nh),
                  pl.BlockSpec((seq, LANES), lambda b, h: (b, 0)),
                  pl.BlockSpec((1, 2 * nh, seq), lambda b, h: (b, 0, 0)),
                  pl.BlockSpec((1, hd), lambda b, h: (0, 0))],
        out_specs=pl.BlockSpec((seq, hd), lambda b, h: (b, h)),
        scratch_shapes=[pltpu.VMEM((seq, hd), F32)] * 3
                       + [pltpu.VMEM((2, seq, hd), F32)] * 3
                       + [pltpu.VMEM((2, seq, hd), BF16)] * 2
                       + [pltpu.VMEM((2, hd, seq), BF16), pltpu.VMEM((2, seq, c), BF16),
                          pltpu.VMEM((2, seq, hd), F32)],
        compiler_params=_cparams("arbitrary", "arbitrary"),
        name="deltanet",
    )(z, z, z, z, conv_w, conv_w, conv_w, gcol, grow, norm_g)


def _merge_kernel(n_first, a_ref, b_ref, ga_ref, gb_ref, xa_ref, xb_ref, gt_ref, wa_ref, wb_ref, wo_ref, g2_ref,
                  sc_ref, sh_ref, x1_ref, h2_ref):
    i = pl.program_id(0)
    tm = a_ref.shape[0]
    first = i < n_first
    nh = 2
    for hf in range(nh):
        r = slice(hf * tm // nh, (hf + 1) * tm // nh)
        ya = _dot(a_ref[r, :], wa_ref[...])
        yb = _dot(b_ref[r, :], wb_ref[...])
        m = (jax.nn.sigmoid(ga_ref[r, :].astype(F32)) * ya + jax.nn.sigmoid(gb_ref[r, :].astype(F32)) * yb)
        y = _dot(m.astype(BF16), wo_ref[...])
        x1 = jnp.where(first, xa_ref[r, :], xb_ref[r, :]) + gt_ref[0] * y
        x1_ref[r, :] = x1
        h2 = _rms(x1, g2_ref[...]) * (1.0 + sc_ref[0]) + sh_ref[0]
        tiles = h2_ref.at[pl.ds(hf * tm // nh * SUBLANES, tm // nh * SUBLANES), :]
        _store_token_tiles(tiles, _pack_halves(h2))


def _merge(a_out, b_out, z, xa, xb, gt1, wa, wb, wo, g2, sc2, sh2, seq, gate_blk, tm=256):
    d = xa.shape[1]
    t = xa.shape[0] + xb.shape[0]
    wdt = a_out.shape[1]
    tm = min(tm, seq)
    per = seq // tm
    n_first = xa.shape[0] // tm
    spec_a, spec_b = _two_trunk_specs((tm, d), n_first)
    bvec = pl.BlockSpec((1, 1, d), lambda i: (i // per, 0, 0))
    const = lambda shp: pl.BlockSpec(shp, lambda i: (0, 0), pipeline_mode=pl.Buffered(1))
    return pl.pallas_call(
        functools.partial(_merge_kernel, n_first),
        out_shape=(jax.ShapeDtypeStruct((t, d), F32), jax.ShapeDtypeStruct((t * SUBLANES, LANES), U32)),
        grid=(t // tm,),
        in_specs=[pl.BlockSpec((tm, wdt), lambda i: (i, 0)),
                  pl.BlockSpec((tm, wdt), lambda i: (i, 0)),
                  pl.BlockSpec((tm, d), lambda i: (i, gate_blk)),
                  pl.BlockSpec((tm, d), lambda i: (i, gate_blk + 1)),
                  spec_a, spec_b,
                  bvec, const((wdt, d)), const((wdt, d)), const((d, d)),
                  pl.BlockSpec((1, d), lambda i: (0, 0)), bvec, bvec],
        out_specs=(pl.BlockSpec((tm, d), lambda i: (i, 0)),
                   pl.BlockSpec((tm * SUBLANES, LANES), lambda i: (i, 0))),
        compiler_params=_cparams("arbitrary"),
        name="merge",
    )(a_out, b_out, z, z, xa, xb, gt1, wa, wb, wo, g2, sc2, sh2)


def _prep_mixer(w, nb, seq):
    wdt = SGU_GROUPS * LANES
    nh, hd = DN_HEADS, DN_HEAD_DIM
    w_in = w["w_in"][0]
    d = w_in.shape[0]
    ab0 = 2 * wdt + 4 * nh * hd
    w_main = jnp.concatenate([w_in[:, :ab0], w_in[:, ab0 + 4 * nh:]], axis=1).astype(BF16)
    w_ab = jnp.pad(w_in[:, ab0:ab0 + 4 * nh], ((0, 0), (0, LANES - 4 * nh))).astype(BF16)
    pad_row = lambda v: jnp.pad(v.reshape(1, 2 * nh), ((0, 0), (0, LANES - 2 * nh)))
    sb = w["sgu_b"][0]
    wbr = w["w_branch"][0]
    return dict(
        nb=nb, seq=seq,
        norm_mix_g=w["norm_mix_g"][0].reshape(1, d), norm_ffn_g=w["norm_ffn_g"][0].reshape(1, d),
        w_main=w_main, w_ab=w_ab,
        sgu_ln_g=w["sgu_ln_g"][0].reshape(1, wdt), sgu_ln_b=w["sgu_ln_b"][0].reshape(1, wdt),
        sgu_ws=w["sgu_ws"][0].astype(BF16),
        sgu_sb=jnp.broadcast_to(sb[:, :, None], sb.shape + (LANES,)),
        alog_row=pad_row(w["dn_a_log"][0]), dt_row=pad_row(w["dn_dt_bias"][0]),
        dn_conv_w=w["dn_conv_w"][0], dn_norm_g=w["dn_norm_g"][0].reshape(1, hd),
        wa=wbr[:wdt].astype(BF16), wb=wbr[wdt:].astype(BF16), wo=w["w_out"][0].astype(BF16),
    )


def _token_mixer_stage(xa, xb, mods, p):
    nb, seq = p["nb"], p["seq"]
    d = xa.shape[1]
    sh1, sc1, gt1, sh2, sc2, _ = mods
    wdt = SGU_GROUPS * LANES
    nh = DN_HEADS
    z, ab = _inproj(xa, xb, p["norm_mix_g"], sc1, sh1, p["w_main"], p["w_ab"], seq)
    a_out = _sgu(z, p["sgu_ln_g"], p["sgu_ln_b"], p["sgu_ws"], p["sgu_sb"])
    gcol, grow = _gates(ab, p["alog_row"], p["dt_row"], nb, seq)
    qkv_blk = 2 * wdt // LANES
    b_out = _deltanet(z, p["dn_conv_w"], gcol, grow, p["dn_norm_g"], nb, seq,
                      qkv_blk=qkv_blk, og_blk=qkv_blk + 3 * nh)
    gate_blk = (2 * wdt + 4 * nh * DN_HEAD_DIM) // d
    return _merge(a_out, b_out, z, xa, xb, gt1, p["wa"], p["wb"], p["wo"], p["norm_ffn_g"], sc2, sh2, seq,
                  gate_blk)


def _first_argmax(vals, idx, n, axis):
    mx = jnp.max(vals, axis=axis, keepdims=True)
    ix = jnp.min(jnp.where(vals == mx, idx, n), axis=axis, keepdims=True)
    return mx, ix


def _router_kernel(x_ref, g_ref, sc_ref, sh_ref, whi_ref, wlo_ref, bias_ref,
                   eidx_ref, wts_ref, rank_ref, cnt_ref, base_s):
    i = pl.program_id(0)
    ne, tm = bias_ref.shape
    neg = -jnp.inf

    @pl.when(i == 0)
    def _():
        base_s[...] = jnp.zeros_like(base_s)

    h = _rms(x_ref[...], g_ref[...]) * (1.0 + sc_ref[0]) + sh_ref[0]
    h_hi, h_mid, _ = _split3(h)
    whi = whi_ref[...]
    logits = _dot_nt(whi, h_hi) + _dot_nt(wlo_ref[...], h_hi) + _dot_nt(whi, h_mid)
    scores = jax.nn.sigmoid(logits)
    sel = scores + bias_ref[...]

    per = ne // N_GROUPS
    sel3 = sel.reshape(N_GROUPS, per, tm)
    ri = lax.broadcasted_iota(I32, sel3.shape, 1)
    m1, i1 = _first_argmax(sel3, ri, per, 1)
    m2 = jnp.max(jnp.where(ri == i1, neg, sel3), axis=1, keepdims=True)
    grp = (m1 + m2).reshape(N_GROUPS, tm)

    gi = lax.broadcasted_iota(I32, grp.shape, 0)
    chosen = jnp.zeros(grp.shape, F32)
    for _ in range(TOPK_GROUPS):
        _, ix = _first_argmax(grp, gi, N_GROUPS, 0)
        hit = gi == ix
        chosen = jnp.where(hit, 1.0, chosen)
        grp = jnp.where(hit, neg, grp)
    masked = jnp.where(chosen.reshape(N_GROUPS, 1, tm) > 0.0, sel3, neg).reshape(ne, tm)

    ei = lax.broadcasted_iota(I32, (ne, tm), 0)
    msel = jnp.zeros((ne, tm), F32)
    idx_rows, w_rows = [], []
    for _ in range(TOP_K):
        _, ix = _first_argmax(masked, ei, ne, 0)
        hit = ei == ix
        w_rows.append(jnp.sum(jnp.where(hit, scores, 0.0), axis=0, keepdims=True))
        idx_rows.append(ix)
        msel = jnp.where(hit, 1.0, msel)
        masked = jnp.where(hit, neg, masked)
    w = jnp.concatenate(w_rows, axis=0)
    eidx_ref[...] = jnp.concatenate(idx_rows, axis=0)
    wts_ref[...] = w / (jnp.sum(w, axis=0, keepdims=True) + 1e-20) * ROUTED_SCALE

    mb = msel.astype(BF16)
    a = lax.broadcasted_iota(I32, (tm, tm), 0)
    b = lax.broadcasted_iota(I32, (tm, tm), 1)
    before = _dot(mb, jnp.where(a < b, 1.0, 0.0).astype(BF16))
    pos = base_s[...] + before
    rank_ref[...] = jnp.concatenate(
        [jnp.sum(jnp.where(ei == ix, pos, 0.0), axis=0, keepdims=True) for ix in idx_rows], axis=0).astype(I32)
    base_s[...] = base_s[...] + _dot(mb, jnp.ones((tm, tm), BF16))
    cnt_ref[...] = base_s[:, :LANES].astype(I32)


def _router(x1, g2, sc2, sh2, wt_hi, wt_lo, bias, seq, tm=512):
    t, d = x1.shape
    ne = wt_hi.shape[0]
    tm = min(tm, seq)
    per = seq // tm
    bvec = pl.BlockSpec((1, 1, d), lambda i: (i // per, 0, 0))
    kt = lambda dt: jax.ShapeDtypeStruct((TOP_K, t), dt)
    kspec = pl.BlockSpec((TOP_K, tm), lambda i: (0, i))
    return pl.pallas_call(
        _router_kernel,
        out_shape=(kt(I32), kt(F32), kt(I32), jax.ShapeDtypeStruct((ne, LANES), I32)),
        grid=(t // tm,),
        in_specs=[pl.BlockSpec((tm, d), lambda i: (i, 0)),
                  pl.BlockSpec((1, d), lambda i: (0, 0)), bvec, bvec,
                  pl.BlockSpec((ne, d), lambda i: (0, 0)),
                  pl.BlockSpec((ne, d), lambda i: (0, 0)),
                  pl.BlockSpec((ne, tm), lambda i: (0, 0))],
        out_specs=(kspec, kspec, kspec, pl.BlockSpec((ne, LANES), lambda i: (0, 0))),
        scratch_shapes=[pltpu.VMEM((ne, tm), F32)],
        compiler_params=_cparams("arbitrary"),
        name="router",
    )(x1, g2, sc2, sh2, wt_hi, wt_lo, jnp.broadcast_to(bias.reshape(ne, 1), (ne, tm)))


def _rows_kernel(eidx_ref, rank_ref, pstart_ref, rows_ref):
    ne, tm = pstart_ref.shape
    ei = lax.broadcasted_iota(I32, (ne, tm), 0)
    ps = pstart_ref[...]
    eidx = eidx_ref[...]
    rows_ref[...] = rank_ref[...] + jnp.concatenate(
        [jnp.sum(jnp.where(ei == eidx[k:k + 1, :], ps, 0), axis=0, keepdims=True) for k in range(TOP_K)], axis=0)


def _pair_rows(eidx, rank, pstart, tm=512):
    t = eidx.shape[1]
    ne = pstart.shape[0]
    tm = min(tm, t)
    kspec = pl.BlockSpec((TOP_K, tm), lambda i: (0, i))
    return pl.pallas_call(
        _rows_kernel,
        out_shape=jax.ShapeDtypeStruct((TOP_K, t), I32),
        grid=(t // tm,),
        in_specs=[kspec, kspec, pl.BlockSpec((ne, tm), lambda i: (0, 0))],
        out_specs=kspec,
        compiler_params=_cparams("arbitrary"),
        name="pair_rows",
    )(eidx, rank, jnp.broadcast_to(pstart.reshape(ne, 1), (ne, tm)))


def _tile(ref, r, n=1):
    return ref.at[pl.ds(pl.multiple_of(r * SUBLANES, SUBLANES), n * SUBLANES), :]


def _token_copy_all(src_of, dst_of, sem, rows_ref, tm):
    def issue(t, carry):
        for k in range(TOP_K):
            row = rows_ref[k, t]
            pltpu.make_async_copy(src_of(k, t, row), dst_of(k, t, row), sem).start(priority=k % 2)
        return carry
    lax.fori_loop(0, tm, issue, 0, unroll=4)


def _swiglu_packed(h_ref, wg_ref, wu_ref, wd_ref):
    lo, hi = _unpack_halves(_load_token_tiles(h_ref))
    lo, hi = lo.astype(BF16), hi.astype(BF16)
    n = lo.shape[1]
    g = _dot(lo, wg_ref[:n, :]) + _dot(hi, wg_ref[n:, :])
    u = _dot(lo, wu_ref[:n, :]) + _dot(hi, wu_ref[n:, :])
    return _dot((g * jax.nn.sigmoid(g) * u).astype(BF16), wd_ref[...])


def _dispatch_kernel(pstart_ref, cnt_ref, nused_ref, h_ref, rows_ref, wg_ref, wu_ref, wd_ref,
                     xs_ref, shared_ref, zbuf, sem, zsem):
    i = pl.program_id(0)
    tm = h_ref.shape[0] // SUBLANES
    bm = zbuf.shape[0] // SUBLANES
    ne = cnt_ref.shape[0]
    nblk = xs_ref.shape[0] // (bm * SUBLANES)

    def zero_fill(act):
        def per_expert(e, carry):
            c = cnt_ref[e]
            pad = lax.rem(bm - lax.rem(c, bm), bm)
            off = pstart_ref[e] + c
            s = bm // 2
            while s >= 1:
                @pl.when((pad & s) != 0)
                def _(s=s, off=off):
                    act(pltpu.make_async_copy(_tile(zbuf, 0, s), _tile(xs_ref, off, s), zsem))
                off = off + (pad & s)
                s //= 2
            return carry
        lax.fori_loop(0, ne, per_expert, 0)

        def per_block(j, carry):
            act(pltpu.make_async_copy(zbuf, _tile(xs_ref, j * bm, bm), zsem))
            return carry
        lax.fori_loop(nused_ref[0], nblk, per_block, 0)

    @pl.when(i == 0)
    def _():
        zbuf[...] = jnp.zeros_like(zbuf)
        zero_fill(lambda cp: cp.start())

    _token_copy_all(lambda k, t, row: _tile(h_ref, t), lambda k, t, row: _tile(xs_ref, row), sem, rows_ref, tm)
    shared_ref[...] = _swiglu_packed(h_ref, wg_ref, wu_ref, wd_ref).astype(BF16)
    for _ in range(TOP_K):
        pltpu.make_async_copy(h_ref, _tile(xs_ref, 0, tm), sem).wait()

    @pl.when(i == 0)
    def _():
        zero_fill(lambda cp: cp.wait())


def _dispatch(h2p, rows, pstart, counts, n_used, sh_wg, sh_wu, sh_wd, n_rows, bm, tm=256):
    t = h2p.shape[0] // SUBLANES
    d, f = sh_wg.shape
    tm = min(tm, t)
    const = lambda shp: pl.BlockSpec(shp, lambda i, *_: (0, 0), pipeline_mode=pl.Buffered(1))
    return pl.pallas_call(
        _dispatch_kernel,
        out_shape=(jax.ShapeDtypeStruct((n_rows * SUBLANES, LANES), U32), jax.ShapeDtypeStruct((t, d), BF16)),
        grid_spec=pltpu.PrefetchScalarGridSpec(
            num_scalar_prefetch=3,
            grid=(t // tm,),
            in_specs=[pl.BlockSpec((tm * SUBLANES, LANES), lambda i, *_: (i, 0)),
                      pl.BlockSpec((TOP_K, tm), lambda i, *_: (0, i), memory_space=pltpu.SMEM),
                      const((d, f)), const((d, f)), const((f, d))],
            out_specs=(pl.BlockSpec(memory_space=pl.ANY), pl.BlockSpec((tm, d), lambda i, *_: (i, 0))),
            scratch_shapes=[pltpu.VMEM((bm * SUBLANES, LANES), U32), pltpu.SemaphoreType.DMA(()),
                            pltpu.SemaphoreType.DMA(())]),
        compiler_params=_cparams("arbitrary"),
        name="dispatch",
    )(pstart, counts, n_used, h2p, rows, sh_wg, sh_wu, sh_wd)


def _expert_kernel(be_ref, first_ref, nxt_ref, slot_ref, nu_ref, xs_ref, wg_hbm, wu_hbm, wd_hbm, y_ref,
                   wg_buf, wu_buf, wd_buf, sems):
    j = pl.program_id(0)
    nu = nu_ref[0]

    def weight_copies(e, s):
        return [pltpu.make_async_copy(hbm.at[e], buf.at[s], sems.at[s, i])
                for i, (hbm, buf) in enumerate(((wg_hbm, wg_buf), (wu_hbm, wu_buf), (wd_hbm, wd_buf)))]

    @pl.when((j == 0) & (nu > 0))
    def _():
        for cp in weight_copies(be_ref[0], slot_ref[0]):
            cp.start()

    @pl.when((j < nu) & (first_ref[j] == 1))
    def _():
        for cp in weight_copies(be_ref[j], slot_ref[j]):
            cp.wait()

        @pl.when(nxt_ref[j] >= 0)
        def _():
            for cp in weight_copies(nxt_ref[j], 1 - slot_ref[j]):
                cp.start()

    @pl.when(j < nu)
    def _():
        s = slot_ref[j]
        lo, hi = _unpack_halves(_load_token_tiles(xs_ref))
        lo, hi = lo.astype(BF16), hi.astype(BF16)
        n = lo.shape[1]
        wg = wg_buf[s].astype(BF16)
        wu = wu_buf[s].astype(BF16)
        g = _dot(lo, wg[:n]) + _dot(hi, wg[n:])
        u = _dot(lo, wu[:n]) + _dot(hi, wu[n:])
        a = (g * jax.nn.sigmoid(g) * u).astype(BF16)
        _store_token_tiles(y_ref, _pack_halves(_dot(a, wd_buf[s].astype(BF16))))

    @pl.when(j >= nu)
    def _():
        y_ref[...] = jnp.zeros_like(y_ref)


def _experts(xs, block_e, block_first, block_next, block_slot, n_used, w_gate, w_up, w_down, bm):
    n_rows = xs.shape[0] // SUBLANES
    ne, d, f = w_gate.shape
    nblk = n_rows // bm
    blk = (bm * SUBLANES, LANES)
    return pl.pallas_call(
        _expert_kernel,
        out_shape=jax.ShapeDtypeStruct(xs.shape, U32),
        grid_spec=pltpu.PrefetchScalarGridSpec(
            num_scalar_prefetch=5,
            grid=(nblk,),
            in_specs=[pl.BlockSpec(blk, lambda j, be, fi, nx, sl, nu: (jnp.minimum(j, nu[0] - 1), 0)),
                      pl.BlockSpec(memory_space=pl.ANY),
                      pl.BlockSpec(memory_space=pl.ANY),
                      pl.BlockSpec(memory_space=pl.ANY)],
            out_specs=pl.BlockSpec(blk, lambda j, *_: (j, 0)),
            scratch_shapes=[pltpu.VMEM((2, d, f), F32), pltpu.VMEM((2, d, f), F32), pltpu.VMEM((2, f, d), F32),
                            pltpu.SemaphoreType.DMA((2, 3))]),
        compiler_params=_cparams("arbitrary"),
        name="experts",
    )(block_e, block_first, block_next, block_slot, n_used, xs, w_gate, w_up, w_down)


def _final_kernel(n_first, x1_ref, sh_ref, gt_ref, wts_ref, fg_ref, rows_ref, rows_nx_ref, y_ref,
                  oa_ref, ob_ref, ybuf, sems):
    i = pl.program_id(0)
    tm = x1_ref.shape[0]
    n = x1_ref.shape[1] // 2
    slot = lax.rem(i, 2)

    def gather(r_ref, s):
        _token_copy_all(lambda k, t, row: _tile(y_ref, row), lambda k, t, row: _tile(ybuf.at[s, k], t),
                        sems.at[s], r_ref, tm)

    pl.when(i == 0)(lambda: gather(rows_ref, 0))
    pl.when(i + 1 < pl.num_programs(0))(lambda: gather(rows_nx_ref, 1 - slot))

    for k in range(TOP_K):
        pltpu.make_async_copy(_tile(y_ref, 0, tm), ybuf.at[slot, k], sems.at[slot]).wait()
    shared = sh_ref[...].astype(F32)
    acc_lo, acc_hi = shared[:, :n], shared[:, n:]
    wts = wts_ref[...].T
    for k in range(TOP_K):
        ylo, yhi = _unpack_halves(_load_token_tiles(ybuf.at[slot, k]))
        wk = wts[:, k:k + 1]
        acc_lo = acc_lo + wk * ylo
        acc_hi = acc_hi + wk * yhi
    gt = gt_ref[0]
    x_lo = x1_ref[:, :n] + gt[:, :n] * acc_lo
    x_hi = x1_ref[:, n:] + gt[:, n:] * acc_hi
    ms = (jnp.sum(x_lo * x_lo, axis=-1, keepdims=True) + jnp.sum(x_hi * x_hi, axis=-1, keepdims=True)) / (2 * n)
    r = lax.rsqrt(ms + RMS_EPS)

    def write(o_ref):
        o_ref[:, :n] = x_lo * r * fg_ref[:, :n]
        o_ref[:, n:] = x_hi * r * fg_ref[:, n:]

    pl.when(i < n_first)(lambda: write(oa_ref))
    pl.when(i >= n_first)(lambda: write(ob_ref))


def _final(x1, shared, gt2, wts, final_g, rows, y, rows_first, seq, tm=256):
    t, d = x1.shape
    tm = min(tm, seq)
    per = seq // tm
    nt = t // tm
    n_first = rows_first // tm
    out_a, out_b = _two_trunk_specs((tm, d), n_first)
    return pl.pallas_call(
        functools.partial(_final_kernel, n_first),
        out_shape=(jax.ShapeDtypeStruct((rows_first, d), F32), jax.ShapeDtypeStruct((t - rows_first, d), F32)),
        grid=(nt,),
        in_specs=[pl.BlockSpec((tm, d), lambda i: (i, 0)),
                  pl.BlockSpec((tm, d), lambda i: (i, 0)),
                  pl.BlockSpec((1, 1, d), lambda i: (i // per, 0, 0)),
                  pl.BlockSpec((TOP_K, tm), lambda i: (0, i)),
                  pl.BlockSpec((1, d), lambda i: (0, 0)),
                  pl.BlockSpec((TOP_K, tm), lambda i: (0, i), memory_space=pltpu.SMEM),
                  pl.BlockSpec((TOP_K, tm), lambda i: (0, jnp.minimum(i + 1, nt - 1)), memory_space=pltpu.SMEM),
                  pl.BlockSpec(memory_space=pl.ANY)],
        out_specs=(out_a, out_b),
        scratch_shapes=[pltpu.VMEM((2, TOP_K, tm * SUBLANES, LANES), U32), pltpu.SemaphoreType.DMA((2,))],
        compiler_params=_cparams("arbitrary"),
        name="final",
    )(x1, shared, gt2, wts, final_g, rows, rows, y)


EXPERT_ROWS = 256


def _prep_moe(w):
    rw = w["router_w"][0]
    d = rw.shape[0]
    wt = rw.T
    wt_hi = wt.astype(BF16)
    return dict(
        wt_hi=wt_hi, wt_lo=(wt - wt_hi.astype(F32)).astype(BF16), bias=w["router_bias"][0],
        w_gate=w["exp_w_gate"][0], w_up=w["exp_w_up"][0], w_down=w["exp_w_down"][0],
        sh_wg=w["sh_w_gate"][0].astype(BF16), sh_wu=w["sh_w_up"][0].astype(BF16),
        sh_wd=w["sh_w_down"][0].astype(BF16), final_g=w["final_g"].reshape(1, d),
    )


def _moe_stage(x1, h2p, mods, pm, pw, seq, rows_first):
    _, _, _, sh2, sc2, gt2 = mods
    t = x1.shape[0]
    ne = pw["wt_hi"].shape[0]
    bm = EXPERT_ROWS
    eidx, wts, rank, cnt = _router(x1, pm["norm_ffn_g"], sc2, sh2, pw["wt_hi"], pw["wt_lo"], pw["bias"], seq)
    counts = cnt[:, 0]
    pcounts = (counts + bm - 1) // bm * bm
    pends = jnp.cumsum(pcounts)
    pstart = (pends - pcounts).astype(I32)
    nblk = (t * TOP_K + ne * (bm - 1) + bm - 1) // bm
    n_used = (pends[-1:] // bm).astype(I32)
    jb = jnp.arange(nblk, dtype=I32)
    be = jnp.minimum(jnp.sum((pends[None, :] <= (jb * bm)[:, None]).astype(I32), axis=1), ne - 1)
    ids = jnp.arange(ne, dtype=I32)
    onehot = be[:, None] == ids[None, :]
    look = lambda table: jnp.sum(jnp.where(onehot, table[None, :], 0), axis=1).astype(I32)
    first = (jb * bm == look(pstart)).astype(I32)
    live = counts > 0
    ordinal = jnp.cumsum(live.astype(I32)) - 1
    after = lax.cummin(jnp.where(live, ids, ne), reverse=True)
    next_live = jnp.concatenate([after[1:], jnp.full((1,), ne, I32)])
    nxt = look(jnp.where(next_live < ne, next_live, -1))
    slot = look(ordinal) % 2
    rows = _pair_rows(eidx, rank, pstart)
    xs, shared = _dispatch(h2p, rows, pstart, counts, n_used, pw["sh_wg"], pw["sh_wu"], pw["sh_wd"],
                           nblk * bm, bm)
    y = _experts(xs, be, first, nxt, slot, n_used, pw["w_gate"], pw["w_up"], pw["w_down"], bm)
    return _final(x1, shared, gt2, wts, pw["final_g"], rows, y, rows_first, seq)


def kernel(x_prompt, x_sample, c_prompt, c_sample, ada_w, ada_b, norm_mix_g, norm_ffn_g, w_in, sgu_ln_g,
           sgu_ln_b, sgu_ws, sgu_b, dn_conv_w, dn_a_log, dn_dt_bias, dn_norm_g, w_branch, w_out, router_w,
           router_bias, exp_w_gate, exp_w_up, exp_w_down, sh_w_gate, sh_w_up, sh_w_down, final_g):
    w = dict(ada_w=ada_w, ada_b=ada_b, norm_mix_g=norm_mix_g, norm_ffn_g=norm_ffn_g, w_in=w_in,
             sgu_ln_g=sgu_ln_g, sgu_ln_b=sgu_ln_b, sgu_ws=sgu_ws, sgu_b=sgu_b, dn_conv_w=dn_conv_w,
             dn_a_log=dn_a_log, dn_dt_bias=dn_dt_bias, dn_norm_g=dn_norm_g, w_branch=w_branch, w_out=w_out,
             router_w=router_w, router_bias=router_bias, exp_w_gate=exp_w_gate, exp_w_up=exp_w_up,
             exp_w_down=exp_w_down, sh_w_gate=sh_w_gate, sh_w_up=sh_w_up, sh_w_down=sh_w_down, final_g=final_g)
    assert w_in.shape[0] == 1, "one layer"
    bp, seq, d = x_prompt.shape
    bs = x_sample.shape[0]
    assert x_sample.shape[1] == seq
    nb = bp + bs
    xa = x_prompt.reshape(bp * seq, d)
    xb = x_sample.reshape(bs * seq, d)
    c = jnp.concatenate([c_prompt, c_sample], axis=0)
    npad = -nb % 8
    mod = _ada(jnp.pad(c, ((0, npad), (0, 0))), ada_w[0], ada_b[0])[:nb]
    mods = [m.reshape(nb, 1, d) for m in jnp.split(mod, 6, axis=-1)]
    pm = _prep_mixer(w, nb, seq)
    pw = _prep_moe(w)
    x1, h2p = _token_mixer_stage(xa, xb, mods, pm)
    ya, yb = _moe_stage(x1, h2p, mods, pm, pw, seq, bp * seq)
    return (ya.reshape(bp, seq, d), yb.reshape(bs, seq, d))
```

```python
import functools

import jax
import jax.numpy as jnp
import numpy as np
from jax import lax
from jax.experimental import pallas as pl
from jax.experimental.pallas import tpu as pltpu

F32 = jnp.float32
BF16 = jnp.bfloat16
I32 = jnp.int32
U32 = jnp.uint32

RMS_EPS = 1e-6
LN_EPS = 1e-5
L2_EPS = 1e-6

SGU_GROUPS = 8
SGU_CHUNK = 128
DN_HEADS = 8
DN_HEAD_DIM = 128
DN_CHUNK = 256
DN_PREP_UNROLL = 2
CONV_WIDTH = 5
TOP_K = 8
N_GROUPS = 8
TOPK_GROUPS = 4
ROUTED_SCALE = 2.5

LANES = 128
SUBLANES = 8
VMEM_LIMIT = 56 * 1024 * 1024


def _cparams(*sem):
    return pltpu.CompilerParams(dimension_semantics=sem, vmem_limit_bytes=VMEM_LIMIT)


def _split3(x):
    hi = x.astype(BF16)
    r = x - hi.astype(F32)
    mid = r.astype(BF16)
    lo = (r - mid.astype(F32)).astype(BF16)
    return hi, mid, lo


def _dot(a, b):
    return jnp.dot(a, b, preferred_element_type=F32)


def _dot_nt(a, b):
    return lax.dot_general(a, b, (((1,), (1,)), ((), ())), preferred_element_type=F32)


def _dot_tn(a, b):
    return lax.dot_general(a, b, (((0,), (0,)), ((), ())), preferred_element_type=F32)


def _rms(x, g):
    return x * lax.rsqrt(jnp.mean(x * x, axis=-1, keepdims=True) + RMS_EPS) * g


def _pack_halves(x):
    n = x.shape[1] // 2
    return pltpu.pack_elementwise([x[:, :n], x[:, n:]], packed_dtype=BF16)


def _unpack_halves(w):
    lo = pltpu.unpack_elementwise(w, index=0, packed_dtype=BF16, unpacked_dtype=F32)
    hi = pltpu.unpack_elementwise(w, index=1, packed_dtype=BF16, unpacked_dtype=F32)
    return lo, hi


def _store_token_tiles(ref, x):
    m = x.shape[0]
    for s in range(SUBLANES):
        ref[pl.ds(s, m, stride=SUBLANES), :] = x[:, s * LANES:(s + 1) * LANES]


def _load_token_tiles(ref):
    m = ref.shape[0] // SUBLANES
    return jnp.concatenate([ref[pl.ds(s, m, stride=SUBLANES), :] for s in range(SUBLANES)], axis=1)


def _ada_kernel(c_ref, w_ref, b_ref, o_ref):
    c = c_ref[...]
    a = c * jax.nn.sigmoid(c)
    a_hi, a_mid, _ = _split3(a)
    w = w_ref[...]
    w_hi = w.astype(BF16)
    w_lo = (w - w_hi.astype(F32)).astype(BF16)
    o_ref[...] = _dot(a_hi, w_hi) + _dot(a_mid, w_hi) + _dot(a_hi, w_lo) + b_ref[...]


def _ada(c, ada_w, ada_b, tn=1024):
    nb, d = c.shape
    n = ada_w.shape[1]
    return pl.pallas_call(
        _ada_kernel,
        out_shape=jax.ShapeDtypeStruct((nb, n), F32),
        grid=(n // tn,),
        in_specs=[pl.BlockSpec((nb, d), lambda j: (0, 0)),
                  pl.BlockSpec((d, tn), lambda j: (0, j)),
                  pl.BlockSpec((1, tn), lambda j: (0, j))],
        out_specs=pl.BlockSpec((nb, tn), lambda j: (0, j)),
        compiler_params=_cparams("arbitrary"),
        name="ada",
    )(c, ada_w, ada_b.reshape(1, n))


def _two_trunk_specs(block, n_first):
    first = pl.BlockSpec(block, lambda i, *_: (jnp.minimum(i, n_first - 1), 0))
    second = pl.BlockSpec(block, lambda i, *_: (jnp.maximum(i - n_first, 0), 0))
    return first, second


def _inproj_kernel(n_first, xa_ref, xb_ref, g_ref, sc_ref, sh_ref, w_ref, wab_ref, z_ref, ab_ref, h_s):
    i = pl.program_id(0)
    j = pl.program_id(1)

    def prologue(x_ref):
        h = _rms(x_ref[...], g_ref[...]) * (1.0 + sc_ref[0]) + sh_ref[0]
        h_s[...] = h.astype(BF16)
        ab_ref[...] = _dot(h_s[...], wab_ref[...])

    pl.when((j == 0) & (i < n_first))(lambda: prologue(xa_ref))
    pl.when((j == 0) & (i >= n_first))(lambda: prologue(xb_ref))
    z_ref[...] = _dot(h_s[...], w_ref[...]).astype(BF16)


def _inproj(xa, xb, g, sc, sh, w_main, w_ab, seq, tm=512, tn=2048):
    d = xa.shape[1]
    t = xa.shape[0] + xb.shape[0]
    n = w_main.shape[1]
    tm = min(tm, seq)
    per = seq // tm
    n_first = xa.shape[0] // tm
    spec_a, spec_b = _two_trunk_specs((tm, d), n_first)
    return pl.pallas_call(
        functools.partial(_inproj_kernel, n_first),
        out_shape=(jax.ShapeDtypeStruct((t, n), BF16), jax.ShapeDtypeStruct((t, LANES), F32)),
        grid=(t // tm, n // tn),
        in_specs=[spec_a, spec_b,
                  pl.BlockSpec((1, d), lambda i, j: (0, 0)),
                  pl.BlockSpec((1, 1, d), lambda i, j: (i // per, 0, 0)),
                  pl.BlockSpec((1, 1, d), lambda i, j: (i // per, 0, 0)),
                  pl.BlockSpec((d, tn), lambda i, j: (0, j)),
                  pl.BlockSpec((d, LANES), lambda i, j: (0, 0))],
        out_specs=(pl.BlockSpec((tm, tn), lambda i, j: (i, j)),
                   pl.BlockSpec((tm, LANES), lambda i, j: (i, 0))),
        scratch_shapes=[pltpu.VMEM((tm, d), BF16)],
        compiler_params=_cparams("arbitrary", "arbitrary"),
        name="inproj",
    )(xa, xb, g, sc, sh, w_main, w_ab)


def _sgu_kernel(u_ref, v_ref, g_ref, b_ref, ws_ref, sb_ref, o_ref):
    v = jax.nn.gelu(v_ref[...].astype(F32))
    mu = jnp.mean(v, axis=-1, keepdims=True)
    vc = v - mu
    var = jnp.mean(vc * vc, axis=-1, keepdims=True)
    vn = (vc * lax.rsqrt(var + LN_EPS) * g_ref[...] + b_ref[...]).astype(BF16)
    tm = vn.shape[0]
    for c in range(tm // SGU_CHUNK):
        r = slice(c * SGU_CHUNK, (c + 1) * SGU_CHUNK)
        for gi in range(SGU_GROUPS):
            l = slice(gi * LANES, (gi + 1) * LANES)
            mixed = _dot(ws_ref[gi], vn[r, l]) + sb_ref[gi]
            o_ref[r, l] = (jax.nn.gelu(u_ref[r, l].astype(F32)) * mixed).astype(BF16)


def _sgu(z, ln_g, ln_b, ws, sb_b, tm=512):
    t = z.shape[0]
    w = SGU_GROUPS * LANES
    tm = np.gcd(tm, t)
    return pl.pallas_call(
        _sgu_kernel,
        out_shape=jax.ShapeDtypeStruct((t, w), BF16),
        grid=(t // tm,),
        in_specs=[pl.BlockSpec((tm, w), lambda i: (i, 0)),
                  pl.BlockSpec((tm, w), lambda i: (i, 1)),
                  pl.BlockSpec((1, w), lambda i: (0, 0)),
                  pl.BlockSpec((1, w), lambda i: (0, 0)),
                  pl.BlockSpec((SGU_GROUPS, SGU_CHUNK, SGU_CHUNK), lambda i: (0, 0, 0)),
                  pl.BlockSpec((SGU_GROUPS, SGU_CHUNK, LANES), lambda i: (0, 0, 0))],
        out_specs=pl.BlockSpec((tm, w), lambda i: (i, 0)),
        compiler_params=_cparams("arbitrary"),
        name="sgu",
    )(z, z, ln_g, ln_b, ws, sb_b)


def _gate_kernel(ab_ref, alog_ref, dt_ref, gcol_ref, grow_ref):
    ab = ab_ref[...]
    tr = ab.shape[0]
    lane = lax.broadcasted_iota(I32, ab.shape, 1)
    z = ab + dt_ref[...]
    softplus = jnp.maximum(z, 0.0) + jnp.log1p(jnp.exp(-jnp.abs(z)))
    g = -jnp.exp(alog_ref[...]) * softplus
    beta = jax.nn.sigmoid(ab)
    nh = DN_HEADS
    g = jnp.where(lane < 2 * nh, g, 0.0)
    i = lax.broadcasted_iota(I32, (tr, tr), 0)
    j = lax.broadcasted_iota(I32, (tr, tr), 1)
    same = (i // DN_CHUNK) == (j // DN_CHUNK)
    lower = jnp.where(same & (j <= i), 1.0, 0.0).astype(BF16)
    upper = jnp.where(same & (j >= i), 1.0, 0.0).astype(BF16)
    g_hi, g_mid, g_lo = _split3(g)
    pre = _dot(lower, g_hi) + _dot(lower, g_mid) + _dot(lower, g_lo)
    suf = _dot(upper, g_hi) + _dot(upper, g_mid) + _dot(upper, g_lo)
    out = jnp.where(lane < nh, pre, jnp.where(lane < 2 * nh, suf, jnp.where(lane < 4 * nh, beta, 0.0)))
    gcol_ref[...] = out
    grow_ref[0] = out.T[:2 * nh, :]


def _gates(ab, alog_row, dt_row, nb, seq, tr=512):
    t = ab.shape[0]
    per = seq // tr
    return pl.pallas_call(
        _gate_kernel,
        out_shape=(jax.ShapeDtypeStruct((t, LANES), F32),
                   jax.ShapeDtypeStruct((nb, 2 * DN_HEADS, seq), F32)),
        grid=(t // tr,),
        in_specs=[pl.BlockSpec((tr, LANES), lambda i: (i, 0)),
                  pl.BlockSpec((1, LANES), lambda i: (0, 0)),
                  pl.BlockSpec((1, LANES), lambda i: (0, 0))],
        out_specs=(pl.BlockSpec((tr, LANES), lambda i: (i, 0)),
                   pl.BlockSpec((1, 2 * DN_HEADS, tr), lambda i: (i // per, 0, i % per))),
        compiler_params=_cparams("arbitrary"),
        name="gates",
    )(ab, alog_row, dt_row)


def _conv_silu(x_ref, w_ref):
    x = x_ref[...].astype(F32)
    s = x.shape[0]
    row = lax.broadcasted_iota(I32, x.shape, 0)
    pad = (CONV_WIDTH - 1) // 2
    acc = x * w_ref[pad:pad + 1, :]
    for j in range(CONV_WIDTH):
        d = j - pad
        if d == 0:
            continue
        xs = pltpu.roll(x, (-d) % s, 0)
        ok = (row + d >= 0) & (row + d < s)
        acc = acc + jnp.where(ok, xs, 0.0) * w_ref[j:j + 1, :]
    return acc * jax.nn.sigmoid(acc)


def _l2n(x):
    return x * lax.rsqrt(jnp.sum(x * x, axis=-1, keepdims=True) + L2_EPS)


def _unit_tri_inverses(mats, i, j):
    c = mats[0].shape[0]
    x = i ^ j
    eye = jnp.where(i == j, 1.0, 0.0)
    tinv = [eye - jnp.where(x < 2, a, 0.0) for a in mats]
    s = 2
    while s < c:
        level = (x >= s) & (x < 2 * s)
        tb = [t.astype(BF16) for t in tinv]
        m = [_dot(t, jnp.where(level, a, 0.0).astype(BF16)).astype(BF16) for t, a in zip(tb, mats)]
        tinv = [t - _dot(mm, b) for t, mm, b in zip(tinv, m, tb)]
        s *= 2
    return tinv


def _dn_prep(items):
    c, hd = items[0][1].shape
    i = lax.broadcasted_iota(I32, (c, c), 0)
    j = lax.broadcasted_iota(I32, (c, c), 1)
    decs, kbs, kbfs = [], [], []
    for forward, q, k, v, gam, grow, beta in items:
        incl = (i >= j) if forward else (i <= j)
        gi = jnp.concatenate([gam] * (c // LANES), axis=1)
        decs.append(jnp.where(incl, jnp.exp(gi - grow), 0.0))
        kbs.append(k * beta)
        kbfs.append(k.astype(BF16))
    kk = [_dot_nt(kb.astype(BF16), kbf) for kb, kbf in zip(kbs, kbfs)]
    mats = [jnp.where((i > j) if it[0] else (i < j), m * dec, 0.0) for it, m, dec in zip(items, kk, decs)]
    tinvs = _unit_tri_inverses(mats, i, j)
    egs = [jnp.exp(it[4]) for it in items]
    uws = [_dot(t.astype(BF16), jnp.concatenate([it[3] * it[6], kb * eg], axis=1).astype(BF16))
           for t, it, kb, eg in zip(tinvs, items, kbs, egs)]
    attns = [_dot_nt(it[1].astype(BF16), kbf) * dec for it, kbf, dec in zip(items, kbfs, decs)]
    outs = []
    for (forward, q, k, v, gam, grow, beta), uw, eg, attn in zip(items, uws, egs, attns):
        glast = gam[c - 1:c, :] if forward else gam[0:1, :]
        kd = k * jnp.exp(glast - gam)
        outs.append((uw[:, :hd], uw[:, hd:].astype(BF16), (q * eg).astype(BF16), kd.T.astype(BF16),
                     attn.astype(BF16)))
    return outs


def _dn_kernel(q_ref, k_ref, v_ref, og_ref, wq_ref, wk_ref, wv_ref, gcol_ref, grow_ref, ng_ref, o_ref,
               q_s, k_s, v_s, gam_s, beta_s, u_s, w_s, qd_s, kdt_s, attn_s, o_s):
    h = pl.program_id(1)
    nh = DN_HEADS
    hd = DN_HEAD_DIM
    q_s[...] = _l2n(_conv_silu(q_ref, wq_ref)) * (hd ** -0.5)
    k_s[...] = _l2n(_conv_silu(k_ref, wk_ref))
    v_s[...] = _conv_silu(v_ref, wv_ref)

    g_hi, g_mid, g_lo = _split3(gcol_ref[...])
    rr = lax.broadcasted_iota(I32, (LANES, 2 * LANES), 0)
    cc = lax.broadcasted_iota(I32, (LANES, 2 * LANES), 1)
    sel = lambda off: jnp.where(rr == h + off + jnp.where(cc < LANES, 0, nh), 1.0, 0.0).astype(BF16)
    sel_g = sel(0)
    gam2 = (_dot(jnp.concatenate([g_hi, g_mid], axis=1), jnp.concatenate([sel_g, sel_g], axis=0))
            + _dot(g_lo, sel_g))
    beta2 = _dot(g_hi, sel(2 * nh))
    for d in range(2):
        gam_s[d] = gam2[:, d * LANES:(d + 1) * LANES]
        beta_s[d] = beta2[:, d * LANES:(d + 1) * LANES]

    s = q_s.shape[0]
    c = DN_CHUNK
    n = s // c

    def prep(it, carry):
        where, items = [], []
        for sub in range(DN_PREP_UNROLL):
            rows = pl.ds(pl.multiple_of((it * DN_PREP_UNROLL + sub) * c, c), c)
            for d in range(2):
                where.append((d, rows))
                items.append((d == 0, q_s[rows, :], k_s[rows, :], v_s[rows, :], gam_s[d, rows, :],
                              grow_ref[0, pl.ds(d * nh + h, 1), rows], beta_s[d, rows, :]))
        for (d, rows), (u, w, qd, kdt, attn) in zip(where, _dn_prep(items)):
            u_s[d, rows, :] = u
            w_s[d, rows, :] = w
            qd_s[d, rows, :] = qd
            kdt_s[d, :, rows] = kdt
            attn_s[d, rows, :] = attn
        return carry

    lax.fori_loop(0, n // DN_PREP_UNROLL, prep, 0)

    def scan(it, states):
        r0s = [pl.multiple_of((it if d == 0 else n - 1 - it) * c, c) for d in range(2)]
        rows = [pl.ds(r0, c) for r0 in r0s]
        rs = [_dot(jnp.concatenate([w_s[d, rows[d], :], qd_s[d, rows[d], :]], axis=0), states[d].astype(BF16))
              for d in range(2)]
        vbs = [(u_s[d, rows[d], :] - rs[d][:c]).astype(BF16) for d in range(2)]
        for d in range(2):
            o_s[d, rows[d], :] = rs[d][c:] + _dot(attn_s[d, rows[d], :], vbs[d])
        new = []
        for d in range(2):
            glast = gam_s[d, pl.ds(r0s[d] + (c - 1 if d == 0 else 0), 1), :]
            new.append(states[d] * jnp.exp(glast) + _dot(kdt_s[d, :, rows[d]], vbs[d]))
        return tuple(new)

    zero = jnp.zeros((hd, hd), F32)
    lax.fori_loop(0, n, scan, (zero, zero))

    o = o_s[0] + o_s[1]
    og = og_ref[...].astype(F32)
    o_ref[...] = (_rms(o, ng_ref[...]) * (og * jax.nn.sigmoid(og))).astype(BF16)


def _deltanet(z, conv_w, gcol, grow, norm_g, nb, seq, qkv_blk, og_blk):
    t = z.shape[0]
    nh, hd, c = DN_HEADS, DN_HEAD_DIM, DN_CHUNK
    assert seq % (c * DN_PREP_UNROLL) == 0
    zspec = lambda off: pl.BlockSpec((seq, hd), lambda b, h: (b, off + h))
    wspec = lambda off: pl.BlockSpec((CONV_WIDTH, hd), lambda b, h: (0, off + h))
    return pl.pallas_call(
        _dn_kernel,
        out_shape=jax.ShapeDtypeStruct((t, nh * hd), BF16),
        grid=(nb, nh),
        in_specs=[zspec(qkv_blk), zspec(qkv_blk + nh), zspec(qkv_blk + 2 * nh), zspec(og_blk),
                  wspec(0), wspec(nh), wspec(2 * nh),
                  pl.BlockSpec((seq, LANES), lambda b, h: (b, 0)),
                  pl.BlockSpec((1, 2 * nh, seq), lambda b, h: (b, 0, 0)),
                  pl.BlockSpec((1, hd), lambda b, h: (0, 0))],
        out_specs=pl.BlockSpec((seq, hd), lambda b, h: (b, h)),
        scratch_shapes=[pltpu.VMEM((seq, hd), F32)] * 3
                       + [pltpu.VMEM((2, seq, hd), F32)] * 3
                       + [pltpu.VMEM((2, seq, hd), BF16)] * 2
                       + [pltpu.VMEM((2, hd, seq), BF16), pltpu.VMEM((2, seq, c), BF16),
                          pltpu.VMEM((2, seq, hd), F32)],
        compiler_params=_cparams("arbitrary", "arbitrary"),
        name="deltanet",
    )(z, z, z, z, conv_w, conv_w, conv_w, gcol, grow, norm_g)


def _merge_kernel(n_first, a_ref, b_ref, ga_ref, gb_ref, xa_ref, xb_ref, gt_ref, wa_ref, wb_ref, wo_ref, g2_ref,
                  sc_ref, sh_ref, x1_ref, h2_ref):
    i = pl.program_id(0)
    tm = a_ref.shape[0]
    first = i < n_first
    nh = 2
    for hf in range(nh):
        r = slice(hf * tm // nh, (hf + 1) * tm // nh)
        ya = _dot(a_ref[r, :], wa_ref[...])
        yb = _dot(b_ref[r, :], wb_ref[...])
        m = (jax.nn.sigmoid(ga_ref[r, :].astype(F32)) * ya + jax.nn.sigmoid(gb_ref[r, :].astype(F32)) * yb)
        y = _dot(m.astype(BF16), wo_ref[...])
        x1 = jnp.where(first, xa_ref[r, :], xb_ref[r, :]) + gt_ref[0] * y
        x1_ref[r, :] = x1
        h2 = _rms(x1, g2_ref[...]) * (1.0 + sc_ref[0]) + sh_ref[0]
        tiles = h2_ref.at[pl.ds(hf * tm // nh * SUBLANES, tm // nh * SUBLANES), :]
        _store_token_tiles(tiles, _pack_halves(h2))


def _merge(a_out, b_out, z, xa, xb, gt1, wa, wb, wo, g2, sc2, sh2, seq, gate_blk, tm=256):
    d = xa.shape[1]
    t = xa.shape[0] + xb.shape[0]
    wdt = a_out.shape[1]
    tm = min(tm, seq)
    per = seq // tm
    n_first = xa.shape[0] // tm
    spec_a, spec_b = _two_trunk_specs((tm, d), n_first)
    bvec = pl.BlockSpec((1, 1, d), lambda i: (i // per, 0, 0))
    const = lambda shp: pl.BlockSpec(shp, lambda i: (0, 0), pipeline_mode=pl.Buffered(1))
    return pl.pallas_call(
        functools.partial(_merge_kernel, n_first),
        out_shape=(jax.ShapeDtypeStruct((t, d), F32), jax.ShapeDtypeStruct((t * SUBLANES, LANES), U32)),
        grid=(t // tm,),
        in_specs=[pl.BlockSpec((tm, wdt), lambda i: (i, 0)),
                  pl.BlockSpec((tm, wdt), lambda i: (i, 0)),
                  pl.BlockSpec((tm, d), lambda i: (i, gate_blk)),
                  pl.BlockSpec((tm, d), lambda i: (i, gate_blk + 1)),
                  spec_a, spec_b,
                  bvec, const((wdt, d)), const((wdt, d)), const((d, d)),
                  pl.BlockSpec((1, d), lambda i: (0, 0)), bvec, bvec],
        out_specs=(pl.BlockSpec((tm, d), lambda i: (i, 0)),
                   pl.BlockSpec((tm * SUBLANES, LANES), lambda i: (i, 0))),
        compiler_params=_cparams("arbitrary"),
        name="merge",
    )(a_out, b_out, z, z, xa, xb, gt1, wa, wb, wo, g2, sc2, sh2)


def _prep_mixer(w, nb, seq):
    wdt = SGU_GROUPS * LANES
    nh, hd = DN_HEADS, DN_HEAD_DIM
    w_in = w["w_in"][0]
    d = w_in.shape[0]
    ab0 = 2 * wdt + 4 * nh * hd
    w_main = jnp.concatenate([w_in[:, :ab0], w_in[:, ab0 + 4 * nh:]], axis=1).astype(BF16)
    w_ab = jnp.pad(w_in[:, ab0:ab0 + 4 * nh], ((0, 0), (0, LANES - 4 * nh))).astype(BF16)
    pad_row = lambda v: jnp.pad(v.reshape(1, 2 * nh), ((0, 0), (0, LANES - 2 * nh)))
    sb = w["sgu_b"][0]
    wbr = w["w_branch"][0]
    return dict(
        nb=nb, seq=seq,
        norm_mix_g=w["norm_mix_g"][0].reshape(1, d), norm_ffn_g=w["norm_ffn_g"][0].reshape(1, d),
        w_main=w_main, w_ab=w_ab,
        sgu_ln_g=w["sgu_ln_g"][0].reshape(1, wdt), sgu_ln_b=w["sgu_ln_b"][0].reshape(1, wdt),
        sgu_ws=w["sgu_ws"][0].astype(BF16),
        sgu_sb=jnp.broadcast_to(sb[:, :, None], sb.shape + (LANES,)),
        alog_row=pad_row(w["dn_a_log"][0]), dt_row=pad_row(w["dn_dt_bias"][0]),
        dn_conv_w=w["dn_conv_w"][0], dn_norm_g=w["dn_norm_g"][0].reshape(1, hd),
        wa=wbr[:wdt].astype(BF16), wb=wbr[wdt:].astype(BF16), wo=w["w_out"][0].astype(BF16),
    )


def _token_mixer_stage(xa, xb, mods, p):
    nb, seq = p["nb"], p["seq"]
    d = xa.shape[1]
    sh1, sc1, gt1, sh2, sc2, _ = mods
    wdt = SGU_GROUPS * LANES
    nh = DN_HEADS
    z, ab = _inproj(xa, xb, p["norm_mix_g"], sc1, sh1, p["w_main"], p["w_ab"], seq)
    a_out = _sgu(z, p["sgu_ln_g"], p["sgu_ln_b"], p["sgu_ws"], p["sgu_sb"])
    gcol, grow = _gates(ab, p["alog_row"], p["dt_row"], nb, seq)
    qkv_blk = 2 * wdt // LANES
    b_out = _deltanet(z, p["dn_conv_w"], gcol, grow, p["dn_norm_g"], nb, seq,
                      qkv_blk=qkv_blk, og_blk=qkv_blk + 3 * nh)
    gate_blk = (2 * wdt + 4 * nh * DN_HEAD_DIM) // d
    return _merge(a_out, b_out, z, xa, xb, gt1, p["wa"], p["wb"], p["wo"], p["norm_ffn_g"], sc2, sh2, seq,
                  gate_blk)


def _first_argmax(vals, idx, n, axis):
    mx = jnp.max(vals, axis=axis, keepdims=True)
    ix = jnp.min(jnp.where(vals == mx, idx, n), axis=axis, keepdims=True)
    return mx, ix


def _router_kernel(x_ref, g_ref, sc_ref, sh_ref, whi_ref, wlo_ref, bias_ref,
                   eidx_ref, wts_ref, rank_ref, cnt_ref, base_s):
    i = pl.program_id(0)
    ne, tm = bias_ref.shape
    neg = -jnp.inf

    @pl.when(i == 0)
    def _():
        base_s[...] = jnp.zeros_like(base_s)

    h = _rms(x_ref[...], g_ref[...]) * (1.0 + sc_ref[0]) + sh_ref[0]
    h_hi, h_mid, _ = _split3(h)
    whi = whi_ref[...]
    logits = _dot_nt(whi, h_hi) + _dot_nt(wlo_ref[...], h_hi) + _dot_nt(whi, h_mid)
    scores = jax.nn.sigmoid(logits)
    sel = scores + bias_ref[...]

    per = ne // N_GROUPS
    sel3 = sel.reshape(N_GROUPS, per, tm)
    ri = lax.broadcasted_iota(I32, sel3.shape, 1)
    m1, i1 = _first_argmax(sel3, ri, per, 1)
    m2 = jnp.max(jnp.where(ri == i1, neg, sel3), axis=1, keepdims=True)
    grp = (m1 + m2).reshape(N_GROUPS, tm)

    gi = lax.broadcasted_iota(I32, grp.shape, 0)
    chosen = jnp.zeros(grp.shape, F32)
    for _ in range(TOPK_GROUPS):
        _, ix = _first_argmax(grp, gi, N_GROUPS, 0)
        hit = gi == ix
        chosen = jnp.where(hit, 1.0, chosen)
        grp = jnp.where(hit, neg, grp)
    masked = jnp.where(chosen.reshape(N_GROUPS, 1, tm) > 0.0, sel3, neg).reshape(ne, tm)

    ei = lax.broadcasted_iota(I32, (ne, tm), 0)
    msel = jnp.zeros((ne, tm), F32)
    idx_rows, w_rows = [], []
    for _ in range(TOP_K):
        _, ix = _first_argmax(masked, ei, ne, 0)
        hit = ei == ix
        w_rows.append(jnp.sum(jnp.where(hit, scores, 0.0), axis=0, keepdims=True))
        idx_rows.append(ix)
        msel = jnp.where(hit, 1.0, msel)
        masked = jnp.where(hit, neg, masked)
    w = jnp.concatenate(w_rows, axis=0)
    eidx_ref[...] = jnp.concatenate(idx_rows, axis=0)
    wts_ref[...] = w / (jnp.sum(w, axis=0, keepdims=True) + 1e-20) * ROUTED_SCALE

    mb = msel.astype(BF16)
    a = lax.broadcasted_iota(I32, (tm, tm), 0)
    b = lax.broadcasted_iota(I32, (tm, tm), 1)
    before = _dot(mb, jnp.where(a < b, 1.0, 0.0).astype(BF16))
    pos = base_s[...] + before
    rank_ref[...] = jnp.concatenate(
        [jnp.sum(jnp.where(ei == ix, pos, 0.0), axis=0, keepdims=True) for ix in idx_rows], axis=0).astype(I32)
    base_s[...] = base_s[...] + _dot(mb, jnp.ones((tm, tm), BF16))
    cnt_ref[...] = base_s[:, :LANES].astype(I32)


def _router(x1, g2, sc2, sh2, wt_hi, wt_lo, bias, seq, tm=512):
    t, d = x1.shape
    ne = wt_hi.shape[0]
    tm = min(tm, seq)
    per = seq // tm
    bvec = pl.BlockSpec((1, 1, d), lambda i: (i // per, 0, 0))
    kt = lambda dt: jax.ShapeDtypeStruct((TOP_K, t), dt)
    kspec = pl.BlockSpec((TOP_K, tm), lambda i: (0, i))
    return pl.pallas_call(
        _router_kernel,
        out_shape=(kt(I32), kt(F32), kt(I32), jax.ShapeDtypeStruct((ne, LANES), I32)),
        grid=(t // tm,),
        in_specs=[pl.BlockSpec((tm, d), lambda i: (i, 0)),
                  pl.BlockSpec((1, d), lambda i: (0, 0)), bvec, bvec,
                  pl.BlockSpec((ne, d), lambda i: (0, 0)),
                  pl.BlockSpec((ne, d), lambda i: (0, 0)),
                  pl.BlockSpec((ne, tm), lambda i: (0, 0))],
        out_specs=(kspec, kspec, kspec, pl.BlockSpec((ne, LANES), lambda i: (0, 0))),
        scratch_shapes=[pltpu.VMEM((ne, tm), F32)],
        compiler_params=_cparams("arbitrary"),
        name="router",
    )(x1, g2, sc2, sh2, wt_hi, wt_lo, jnp.broadcast_to(bias.reshape(ne, 1), (ne, tm)))


def _rows_kernel(eidx_ref, rank_ref, pstart_ref, rows_ref):
    ne, tm = pstart_ref.shape
    ei = lax.broadcasted_iota(I32, (ne, tm), 0)
    ps = pstart_ref[...]
    eidx = eidx_ref[...]
    rows_ref[...] = rank_ref[...] + jnp.concatenate(
        [jnp.sum(jnp.where(ei == eidx[k:k + 1, :], ps, 0), axis=0, keepdims=True) for k in range(TOP_K)], axis=0)


def _pair_rows(eidx, rank, pstart, tm=512):
    t = eidx.shape[1]
    ne = pstart.shape[0]
    tm = min(tm, t)
    kspec = pl.BlockSpec((TOP_K, tm), lambda i: (0, i))
    return pl.pallas_call(
        _rows_kernel,
        out_shape=jax.ShapeDtypeStruct((TOP_K, t), I32),
        grid=(t // tm,),
        in_specs=[kspec, kspec, pl.BlockSpec((ne, tm), lambda i: (0, 0))],
        out_specs=kspec,
        compiler_params=_cparams("arbitrary"),
        name="pair_rows",
    )(eidx, rank, jnp.broadcast_to(pstart.reshape(ne, 1), (ne, tm)))


def _tile(ref, r, n=1):
    return ref.at[pl.ds(pl.multiple_of(r * SUBLANES, SUBLANES), n * SUBLANES), :]


def _token_copy_all(src_of, dst_of, sem, rows_ref, tm):
    def issue(t, carry):
        for k in range(TOP_K):
            row = rows_ref[k, t]
            pltpu.make_async_copy(src_of(k, t, row), dst_of(k, t, row), sem).start(priority=k % 2)
        return carry
    lax.fori_loop(0, tm, issue, 0, unroll=4)


def _swiglu_packed(h_ref, wg_ref, wu_ref, wd_ref):
    lo, hi = _unpack_halves(_load_token_tiles(h_ref))
    lo, hi = lo.astype(BF16), hi.astype(BF16)
    n = lo.shape[1]
    g = _dot(lo, wg_ref[:n, :]) + _dot(hi, wg_ref[n:, :])
    u = _dot(lo, wu_ref[:n, :]) + _dot(hi, wu_ref[n:, :])
    return _dot((g * jax.nn.sigmoid(g) * u).astype(BF16), wd_ref[...])


def _dispatch_kernel(pstart_ref, cnt_ref, nused_ref, h_ref, rows_ref, wg_ref, wu_ref, wd_ref,
                     xs_ref, shared_ref, zbuf, sem, zsem):
    i = pl.program_id(0)
    tm = h_ref.shape[0] // SUBLANES
    bm = zbuf.shape[0] // SUBLANES
    ne = cnt_ref.shape[0]
    nblk = xs_ref.shape[0] // (bm * SUBLANES)

    def zero_fill(act):
        def per_expert(e, carry):
            c = cnt_ref[e]
            pad = lax.rem(bm - lax.rem(c, bm), bm)
            off = pstart_ref[e] + c
            s = bm // 2
            while s >= 1:
                @pl.when((pad & s) != 0)
                def _(s=s, off=off):
                    act(pltpu.make_async_copy(_tile(zbuf, 0, s), _tile(xs_ref, off, s), zsem))
                off = off + (pad & s)
                s //= 2
            return carry
        lax.fori_loop(0, ne, per_expert, 0)

        def per_block(j, carry):
            act(pltpu.make_async_copy(zbuf, _tile(xs_ref, j * bm, bm), zsem))
            return carry
        lax.fori_loop(nused_ref[0], nblk, per_block, 0)

    @pl.when(i == 0)
    def _():
        zbuf[...] = jnp.zeros_like(zbuf)
        zero_fill(lambda cp: cp.start())

    _token_copy_all(lambda k, t, row: _tile(h_ref, t), lambda k, t, row: _tile(xs_ref, row), sem, rows_ref, tm)
    shared_ref[...] = _swiglu_packed(h_ref, wg_ref, wu_ref, wd_ref).astype(BF16)
    for _ in range(TOP_K):
        pltpu.make_async_copy(h_ref, _tile(xs_ref, 0, tm), sem).wait()

    @pl.when(i == 0)
    def _():
        zero_fill(lambda cp: cp.wait())


def _dispatch(h2p, rows, pstart, counts, n_used, sh_wg, sh_wu, sh_wd, n_rows, bm, tm=256):
    t = h2p.shape[0] // SUBLANES
    d, f = sh_wg.shape
    tm = min(tm, t)
    const = lambda shp: pl.BlockSpec(shp, lambda i, *_: (0, 0), pipeline_mode=pl.Buffered(1))
    return pl.pallas_call(
        _dispatch_kernel,
        out_shape=(jax.ShapeDtypeStruct((n_rows * SUBLANES, LANES), U32), jax.ShapeDtypeStruct((t, d), BF16)),
        grid_spec=pltpu.PrefetchScalarGridSpec(
            num_scalar_prefetch=3,
            grid=(t // tm,),
            in_specs=[pl.BlockSpec((tm * SUBLANES, LANES), lambda i, *_: (i, 0)),
                      pl.BlockSpec((TOP_K, tm), lambda i, *_: (0, i), memory_space=pltpu.SMEM),
                      const((d, f)), const((d, f)), const((f, d))],
            out_specs=(pl.BlockSpec(memory_space=pl.ANY), pl.BlockSpec((tm, d), lambda i, *_: (i, 0))),
            scratch_shapes=[pltpu.VMEM((bm * SUBLANES, LANES), U32), pltpu.SemaphoreType.DMA(()),
                            pltpu.SemaphoreType.DMA(())]),
        compiler_params=_cparams("arbitrary"),
        name="dispatch",
    )(pstart, counts, n_used, h2p, rows, sh_wg, sh_wu, sh_wd)


def _expert_kernel(be_ref, first_ref, nxt_ref, slot_ref, nu_ref, xs_ref, wg_hbm, wu_hbm, wd_hbm, y_ref,
                   wg_buf, wu_buf, wd_buf, wg_b, wu_b, wd_b, sems):
    j = pl.program_id(0)
    nu = nu_ref[0]

    def weight_copies(e, s):
        return [pltpu.make_async_copy(hbm.at[e], buf.at[s], sems.at[s, i])
                for i, (hbm, buf) in enumerate(((wg_hbm, wg_buf), (wu_hbm, wu_buf), (wd_hbm, wd_buf)))]

    @pl.when((j == 0) & (nu > 0))
    def _():
        for cp in weight_copies(be_ref[0], slot_ref[0]):
            cp.start()

    @pl.when((j < nu) & (first_ref[j] == 1))
    def _():
        for cp in weight_copies(be_ref[j], slot_ref[j]):
            cp.wait()

        @pl.when(nxt_ref[j] >= 0)
        def _():
            for cp in weight_copies(nxt_ref[j], 1 - slot_ref[j]):
                cp.start()

    def ffn(wg, wu, wd):
        lo, hi = _unpack_halves(_load_token_tiles(xs_ref))
        lo, hi = lo.astype(BF16), hi.astype(BF16)
        n = lo.shape[1]
        g = _dot(lo, wg[:n]) + _dot(hi, wg[n:])
        u = _dot(lo, wu[:n]) + _dot(hi, wu[n:])
        a = (g * jax.nn.sigmoid(g) * u).astype(BF16)
        _store_token_tiles(y_ref, _pack_halves(_dot(a, wd)))

    @pl.when((j < nu) & (first_ref[j] == 1))
    def _():
        s = slot_ref[j]
        wg_b[...] = wg_buf[s].astype(BF16)
        wu_b[...] = wu_buf[s].astype(BF16)
        wd_b[...] = wd_buf[s].astype(BF16)
        ffn(wg_b[...], wu_b[...], wd_b[...])

    @pl.when((j < nu) & (first_ref[j] == 0))
    def _():
        ffn(wg_b[...], wu_b[...], wd_b[...])

    @pl.when(j >= nu)
    def _():
        y_ref[...] = jnp.zeros_like(y_ref)


def _experts(xs, block_e, block_first, block_next, block_slot, n_used, w_gate, w_up, w_down, bm):
    n_rows = xs.shape[0] // SUBLANES
    ne, d, f = w_gate.shape
    nblk = n_rows // bm
    blk = (bm * SUBLANES, LANES)
    return pl.pallas_call(
        _expert_kernel,
        out_shape=jax.ShapeDtypeStruct(xs.shape, U32),
        grid_spec=pltpu.PrefetchScalarGridSpec(
            num_scalar_prefetch=5,
            grid=(nblk,),
            in_specs=[pl.BlockSpec(blk, lambda j, be, fi, nx, sl, nu: (jnp.minimum(j, nu[0] - 1), 0)),
                      pl.BlockSpec(memory_space=pl.ANY),
                      pl.BlockSpec(memory_space=pl.ANY),
                      pl.BlockSpec(memory_space=pl.ANY)],
            out_specs=pl.BlockSpec(blk, lambda j, *_: (j, 0)),
            scratch_shapes=[pltpu.VMEM((2, d, f), F32), pltpu.VMEM((2, d, f), F32), pltpu.VMEM((2, f, d), F32),
                            pltpu.VMEM((d, f), BF16), pltpu.VMEM((d, f), BF16), pltpu.VMEM((f, d), BF16),
                            pltpu.SemaphoreType.DMA((2, 3))]),
        compiler_params=_cparams("arbitrary"),
        name="experts",
    )(block_e, block_first, block_next, block_slot, n_used, xs, w_gate, w_up, w_down)


def _final_kernel(n_first, x1_ref, sh_ref, gt_ref, wts_ref, fg_ref, rows_ref, rows_nx_ref, y_ref,
                  oa_ref, ob_ref, ybuf, sems):
    i = pl.program_id(0)
    tm = x1_ref.shape[0]
    n = x1_ref.shape[1] // 2
    slot = lax.rem(i, 2)

    def gather(r_ref, s):
        _token_copy_all(lambda k, t, row: _tile(y_ref, row), lambda k, t, row: _tile(ybuf.at[s, k], t),
                        sems.at[s], r_ref, tm)

    pl.when(i == 0)(lambda: gather(rows_ref, 0))
    pl.when(i + 1 < pl.num_programs(0))(lambda: gather(rows_nx_ref, 1 - slot))

    for k in range(TOP_K):
        pltpu.make_async_copy(_tile(y_ref, 0, tm), ybuf.at[slot, k], sems.at[slot]).wait()
    shared = sh_ref[...].astype(F32)
    acc_lo, acc_hi = shared[:, :n], shared[:, n:]
    wts = wts_ref[...].T
    for k in range(TOP_K):
        ylo, yhi = _unpack_halves(_load_token_tiles(ybuf.at[slot, k]))
        wk = wts[:, k:k + 1]
        acc_lo = acc_lo + wk * ylo
        acc_hi = acc_hi + wk * yhi
    gt = gt_ref[0]
    x_lo = x1_ref[:, :n] + gt[:, :n] * acc_lo
    x_hi = x1_ref[:, n:] + gt[:, n:] * acc_hi
    ms = (jnp.sum(x_lo * x_lo, axis=-1, keepdims=True) + jnp.sum(x_hi * x_hi, axis=-1, keepdims=True)) / (2 * n)
    r = lax.rsqrt(ms + RMS_EPS)

    def write(o_ref):
        o_ref[:, :n] = x_lo * r * fg_ref[:, :n]
        o_ref[:, n:] = x_hi * r * fg_ref[:, n:]

    pl.when(i < n_first)(lambda: write(oa_ref))
    pl.when(i >= n_first)(lambda: write(ob_ref))


def _final(x1, shared, gt2, wts, final_g, rows, y, rows_first, seq, tm=256):
    t, d = x1.shape
    tm = min(tm, seq)
    per = seq // tm
    nt = t // tm
    n_first = rows_first // tm
    out_a, out_b = _two_trunk_specs((tm, d), n_first)
    return pl.pallas_call(
        functools.partial(_final_kernel, n_first),
        out_shape=(jax.ShapeDtypeStruct((rows_first, d), F32), jax.ShapeDtypeStruct((t - rows_first, d), F32)),
        grid=(nt,),
        in_specs=[pl.BlockSpec((tm, d), lambda i: (i, 0)),
                  pl.BlockSpec((tm, d), lambda i: (i, 0)),
                  pl.BlockSpec((1, 1, d), lambda i: (i // per, 0, 0)),
                  pl.BlockSpec((TOP_K, tm), lambda i: (0, i)),
                  pl.BlockSpec((1, d), lambda i: (0, 0)),
                  pl.BlockSpec((TOP_K, tm), lambda i: (0, i), memory_space=pltpu.SMEM),
                  pl.BlockSpec((TOP_K, tm), lambda i: (0, jnp.minimum(i + 1, nt - 1)), memory_space=pltpu.SMEM),
                  pl.BlockSpec(memory_space=pl.ANY)],
        out_specs=(out_a, out_b),
        scratch_shapes=[pltpu.VMEM((2, TOP_K, tm * SUBLANES, LANES), U32), pltpu.SemaphoreType.DMA((2,))],
        compiler_params=_cparams("arbitrary"),
        name="final",
    )(x1, shared, gt2, wts, final_g, rows, rows, y)


EXPERT_ROWS = 256


def _prep_moe(w):
    rw = w["router_w"][0]
    d = rw.shape[0]
    wt = rw.T
    wt_hi = wt.astype(BF16)
    return dict(
        wt_hi=wt_hi, wt_lo=(wt - wt_hi.astype(F32)).astype(BF16), bias=w["router_bias"][0],
        w_gate=w["exp_w_gate"][0], w_up=w["exp_w_up"][0], w_down=w["exp_w_down"][0],
        sh_wg=w["sh_w_gate"][0].astype(BF16), sh_wu=w["sh_w_up"][0].astype(BF16),
        sh_wd=w["sh_w_down"][0].astype(BF16), final_g=w["final_g"].reshape(1, d),
    )


def _moe_stage(x1, h2p, mods, pm, pw, seq, rows_first):
    _, _, _, sh2, sc2, gt2 = mods
    t = x1.shape[0]
    ne = pw["wt_hi"].shape[0]
    bm = EXPERT_ROWS
    eidx, wts, rank, cnt = _router(x1, pm["norm_ffn_g"], sc2, sh2, pw["wt_hi"], pw["wt_lo"], pw["bias"], seq)
    counts = cnt[:, 0]
    pcounts = (counts + bm - 1) // bm * bm
    pends = jnp.cumsum(pcounts)
    pstart = (pends - pcounts).astype(I32)
    nblk = (t * TOP_K + ne * (bm - 1) + bm - 1) // bm
    n_used = (pends[-1:] // bm).astype(I32)
    jb = jnp.arange(nblk, dtype=I32)
    be = jnp.minimum(jnp.sum((pends[None, :] <= (jb * bm)[:, None]).astype(I32), axis=1), ne - 1)
    ids = jnp.arange(ne, dtype=I32)
    onehot = be[:, None] == ids[None, :]
    look = lambda table: jnp.sum(jnp.where(onehot, table[None, :], 0), axis=1).astype(I32)
    first = (jb * bm == look(pstart)).astype(I32)
    live = counts > 0
    ordinal = jnp.cumsum(live.astype(I32)) - 1
    after = lax.cummin(jnp.where(live, ids, ne), reverse=True)
    next_live = jnp.concatenate([after[1:], jnp.full((1,), ne, I32)])
    nxt = look(jnp.where(next_live < ne, next_live, -1))
    slot = look(ordinal) % 2
    rows = _pair_rows(eidx, rank, pstart)
    xs, shared = _dispatch(h2p, rows, pstart, counts, n_used, pw["sh_wg"], pw["sh_wu"], pw["sh_wd"],
                           nblk * bm, bm)
    y = _experts(xs, be, first, nxt, slot, n_used, pw["w_gate"], pw["w_up"], pw["w_down"], bm)
    return _final(x1, shared, gt2, wts, pw["final_g"], rows, y, rows_first, seq)


def kernel(x_prompt, x_sample, c_prompt, c_sample, ada_w, ada_b, norm_mix_g, norm_ffn_g, w_in, sgu_ln_g,
           sgu_ln_b, sgu_ws, sgu_b, dn_conv_w, dn_a_log, dn_dt_bias, dn_norm_g, w_branch, w_out, router_w,
           router_bias, exp_w_gate, exp_w_up, exp_w_down, sh_w_gate, sh_w_up, sh_w_down, final_g):
    w = dict(ada_w=ada_w, ada_b=ada_b, norm_mix_g=norm_mix_g, norm_ffn_g=norm_ffn_g, w_in=w_in,
             sgu_ln_g=sgu_ln_g, sgu_ln_b=sgu_ln_b, sgu_ws=sgu_ws, sgu_b=sgu_b, dn_conv_w=dn_conv_w,
             dn_a_log=dn_a_log, dn_dt_bias=dn_dt_bias, dn_norm_g=dn_norm_g, w_branch=w_branch, w_out=w_out,
             router_w=router_w, router_bias=router_bias, exp_w_gate=exp_w_gate, exp_w_up=exp_w_up,
             exp_w_down=exp_w_down, sh_w_gate=sh_w_gate, sh_w_up=sh_w_up, sh_w_down=sh_w_down, final_g=final_g)
    assert w_in.shape[0] == 1, "one layer"
    bp, seq, d = x_prompt.shape
    bs = x_sample.shape[0]
    assert x_sample.shape[1] == seq
    nb = bp + bs
    xa = x_prompt.reshape(bp * seq, d)
    xb = x_sample.reshape(bs * seq, d)
    c = jnp.concatenate([c_prompt, c_sample], axis=0)
    npad = -nb % 8
    mod = _ada(jnp.pad(c, ((0, npad), (0, 0))), ada_w[0], ada_b[0])[:nb]
    mods = [m.reshape(nb, 1, d) for m in jnp.split(mod, 6, axis=-1)]
    pm = _prep_mixer(w, nb, seq)
    pw = _prep_moe(w)
    x1, h2p = _token_mixer_stage(xa, xb, mods, pm)
    ya, yb = _moe_stage(x1, h2p, mods, pm, pw, seq, bp * seq)
    return (ya.reshape(bp, seq, d), yb.reshape(bs, seq, d))
```
